```python
import jax, jax.numpy as jnp
from jax import lax
import numpy as np

D_MODEL = 4096
BATCH = 4
SEQ = 2048
DEPTH = 2
DEC_BATCH = 8
DEC_SEQ = 4
PAST_LEN = 16384
PAGE_SIZE = 128

N_HEADS = 32
HEAD_DIM = D_MODEL // N_HEADS
N_KV_HEADS = 8
KV_REP = N_HEADS // N_KV_HEADS
IDX_HEADS = 32
IDX_DIM = 64
TOPK_MAX = 256
CONV_WIDTH = 3
CONV_DIM = D_MODEL
D_FF = 4 * D_MODEL
ROPE_THETA = 10000.0
NORM_EPS = 1e-6
Q_BLOCK = 64
NEG = -1e30

SPLITS = (N_HEADS * HEAD_DIM,
          N_KV_HEADS * HEAD_DIM,
          N_KV_HEADS * HEAD_DIM,
          IDX_HEADS * IDX_DIM,
          IDX_DIM,
          IDX_HEADS,
          CONV_DIM,
          CONV_DIM,
          CONV_DIM,
          N_HEADS * HEAD_DIM,
          CONV_DIM)
D_IN_PROJ = sum(SPLITS)
SPLIT_POINTS = [int(s) for s in np.cumsum(SPLITS)[:-1]]

kernel_name = "hybrid_dsa_shortconv_decoder_step"


def rmsnorm(x, g):
    xf = x.astype(jnp.float32)
    y = xf * lax.rsqrt(jnp.mean(xf * xf, axis=-1, keepdims=True) + NORM_EPS)
    return (y * g.astype(jnp.float32)).astype(x.dtype)


def rope(x, pos):
    half = x.shape[-1] // 2
    inv = ROPE_THETA ** (-jnp.arange(half, dtype=jnp.float32) / half)
    ang = pos.astype(jnp.float32)[:, None] * inv[None, :]
    cos = jnp.cos(ang)[:, None, :]
    sin = jnp.sin(ang)[:, None, :]
    xf = x.astype(jnp.float32)
    x1, x2 = xf[..., :half], xf[..., half:]
    return jnp.concatenate([x1 * cos - x2 * sin, x2 * cos + x1 * sin], axis=-1).astype(x.dtype)


def project(h, w_in, pos):
    B, T = h.shape[:2]
    z = h @ w_in
    q, k, v, qi, ki, wi, cb, cc, ch, ga, gc = jnp.split(z, SPLIT_POINTS, axis=-1)
    q = rope(q.reshape(B, T, N_HEADS, HEAD_DIM), pos)
    k = rope(k.reshape(B, T, N_KV_HEADS, HEAD_DIM), pos)
    v = v.reshape(B, T, N_KV_HEADS, HEAD_DIM)
    qi = rope(qi.reshape(B, T, IDX_HEADS, IDX_DIM), pos)
    ki = rope(ki.reshape(B, T, 1, IDX_DIM), pos)[:, :, 0]
    return q, k, v, qi, ki, wi, cb, cc, ch, ga, gc


def indexer_scores(qi, wi, ki):
    dots = jnp.einsum('bthd,bsd->bths', qi.astype(jnp.float32), ki.astype(jnp.float32))
    w = wi.astype(jnp.float32) * (IDX_HEADS ** -0.5 * IDX_DIM ** -0.5)
    return jnp.einsum('bths,bth->bts', jax.nn.relu(dots), w)


def select_keys(scores, q_pos, n_sel):
    S = scores.shape[-1]
    admissible = jnp.arange(S)[None, :] <= q_pos[:, None]
    masked = jnp.where(admissible[None], scores, NEG)
    _, idx = lax.top_k(masked, n_sel)
    valid = idx <= q_pos[None, :, None]
    return idx, valid


def sparse_attn(q, k_sel, v_sel, valid):
    B, T = q.shape[:2]
    qg = q.reshape(B, T, N_KV_HEADS, KV_REP, HEAD_DIM).astype(jnp.float32)
    s = jnp.einsum('btgrd,btkgd->btgrk', qg, k_sel.astype(jnp.float32)) * (HEAD_DIM ** -0.5)
    s = jnp.where(valid[:, :, None, None, :], s, NEG)
    p = jax.nn.softmax(s, axis=-1)
    o = jnp.einsum('btgrk,btkgd->btgrd', p, v_sel.astype(jnp.float32))
    return o.reshape(B, T, N_HEADS * HEAD_DIM).astype(q.dtype)


_gather_rows = jax.vmap(lambda rows, ids: rows[ids])


def prompt_attention(q, k, v, qi, wi, ki):
    B, S = k.shape[:2]
    n_sel = min(TOPK_MAX, S // 4)
    qb = min(Q_BLOCK, S)
    n_blocks = S // qb

    def block(j):
        start = j * qb
        q_b = lax.dynamic_slice_in_dim(q, start, qb, axis=1)
        qi_b = lax.dynamic_slice_in_dim(qi, start, qb, axis=1)
        wi_b = lax.dynamic_slice_in_dim(wi, start, qb, axis=1)
        pos = start + jnp.arange(qb)
        idx, valid = select_keys(indexer_scores(qi_b, wi_b, ki), pos, n_sel)
        return sparse_attn(q_b, _gather_rows(k, idx), _gather_rows(v, idx), valid)

    out = lax.map(block, jnp.arange(n_blocks))
    return out.transpose(1, 0, 2, 3).reshape(B, S, N_HEADS * HEAD_DIM)


def sample_attention(q, k_new, v_new, qi, wi, ki_new, pool_k, pool_v, pool_ki, page_table):
    Bd, T = q.shape[:2]
    past = page_table.shape[1] * PAGE_SIZE
    n_sel = min(TOPK_MAX, (past + T) // 4)
    ki_past = pool_ki[page_table].reshape(Bd, past, IDX_DIM)
    ki_all = jnp.concatenate([ki_past, ki_new.astype(ki_past.dtype)], axis=1)
    pos = past + jnp.arange(T)
    idx, valid = select_keys(indexer_scores(qi, wi, ki_all), pos, n_sel)
    in_past = idx < past
    pidx = jnp.minimum(idx, past - 1)
    phys = page_table[jnp.arange(Bd)[:, None, None], pidx // PAGE_SIZE]
    off = pidx % PAGE_SIZE
    nidx = jnp.clip(idx - past, 0, T - 1)
    sel = in_past[..., None, None]
    k_sel = jnp.where(sel, pool_k[phys, off], _gather_rows(k_new, nidx).astype(pool_k.dtype))
    v_sel = jnp.where(sel, pool_v[phys, off], _gather_rows(v_new, nidx).astype(pool_v.dtype))
    return sparse_attn(q, k_sel, v_sel, valid)


def short_conv(u, u_state, taps):
    T = u.shape[1]
    ext = jnp.concatenate([u_state.astype(u.dtype), u], axis=1)
    y = ext[:, 0:T] * taps[0]
    for j in range(1, CONV_WIDTH):
        y = y + ext[:, j:j + T] * taps[j]
    return y, ext[:, -(CONV_WIDTH - 1):]


def layer(x, pos, conv_state, attend, w_in, conv_w, w_o, g_mix, g_mlp, w_up, w_down):
    h = rmsnorm(x, g_mix)
    q, k, v, qi, ki, wi, cb, cc, ch, ga, gc = project(h, w_in, pos)
    attn = attend(q, k, v, qi, wi, ki)
    conv_y, new_conv = short_conv(cc * ch, conv_state, conv_w)
    merged = jax.nn.sigmoid(ga) * attn + jax.nn.sigmoid(gc) * (cb * conv_y)
    x = x + merged @ w_o
    h2 = rmsnorm(x, g_mlp)
    x = x + jnp.square(jax.nn.relu(h2 @ w_up)) @ w_down
    return x, k, v, ki, new_conv


def setup_inputs(seed: int = 0) -> dict:
    key = jax.random.key(seed)
    ks = jax.random.split(key, 20)
    n_pages = PAST_LEN // PAGE_SIZE
    n_used = DEC_BATCH * n_pages
    n_pool = n_used + max(1, n_used // 4)
    f32 = jnp.float32
    page_table = jax.random.permutation(ks[0], n_pool)[:n_used].reshape(DEC_BATCH, n_pages).astype(jnp.int32)
    return {
        "x_prompt": jax.random.normal(ks[1], (BATCH, SEQ, D_MODEL), f32),
        "x_sample": jax.random.normal(ks[2], (DEC_BATCH, DEC_SEQ, D_MODEL), f32),
        "cache_k": jax.random.normal(ks[3], (DEPTH, n_pool, PAGE_SIZE, N_KV_HEADS, HEAD_DIM), f32),
        "cache_v": jax.random.normal(ks[4], (DEPTH, n_pool, PAGE_SIZE, N_KV_HEADS, HEAD_DIM), f32),
        "cache_kidx": jax.random.normal(ks[5], (DEPTH, n_pool, PAGE_SIZE, IDX_DIM), f32),
        "state_conv": jax.random.normal(ks[6], (DEPTH, DEC_BATCH, CONV_WIDTH - 1, CONV_DIM), f32),
        "page_table": page_table,
        "w_in": jax.random.normal(ks[7], (DEPTH, D_MODEL, D_IN_PROJ), f32) * D_MODEL ** -0.5,
        "conv_w": jax.random.normal(ks[8], (DEPTH, CONV_WIDTH, CONV_DIM), f32) * CONV_WIDTH ** -0.5,
        "w_o": jax.random.normal(ks[9], (DEPTH, D_MODEL, D_MODEL), f32) * D_MODEL ** -0.5,
        "g_mix": 1.0 + 0.01 * jax.random.normal(ks[10], (DEPTH, D_MODEL), f32),
        "g_mlp": 1.0 + 0.01 * jax.random.normal(ks[11], (DEPTH, D_MODEL), f32),
        "w_up": jax.random.normal(ks[12], (DEPTH, D_MODEL, D_FF), f32) * D_MODEL ** -0.5,
        "w_down": jax.random.normal(ks[13], (DEPTH, D_FF, D_MODEL), f32) * D_FF ** -0.5,
        "g_final": 1.0 + 0.01 * jax.random.normal(ks[14], (D_MODEL,), f32),
    }


def reference(x_prompt, x_sample, cache_k, cache_v, cache_kidx, state_conv, page_table,
              w_in, conv_w, w_o, g_mix, g_mlp, w_up, w_down, g_final):
    B, S = x_prompt.shape[:2]
    Bd, T = x_sample.shape[:2]
    past = page_table.shape[1] * PAGE_SIZE
    pos_p = jnp.arange(S)
    pos_s = past + jnp.arange(T)
    conv_zero = jnp.zeros((B, CONV_WIDTH - 1, CONV_DIM), x_prompt.dtype)

    xp, xs = x_prompt, x_sample
    kp_l, vp_l, kip_l, cp_l = [], [], [], []
    ks_l, vs_l, kis_l, cs_l = [], [], [], []
    for l in range(DEPTH):
        xp, kp, vp, kip, cp = layer(xp, pos_p, conv_zero, prompt_attention,
                                    w_in[l], conv_w[l], w_o[l], g_mix[l], g_mlp[l], w_up[l], w_down[l])
        pool_k, pool_v, pool_ki = cache_k[l], cache_v[l], cache_kidx[l]
        attend_s = lambda q, k, v, qi, wi, ki: sample_attention(
            q, k, v, qi, wi, ki, pool_k, pool_v, pool_ki, page_table)
        xs, ksn, vsn, kisn, csn = layer(xs, pos_s, state_conv[l], attend_s,
                                        w_in[l], conv_w[l], w_o[l], g_mix[l], g_mlp[l], w_up[l], w_down[l])
        kp_l.append(kp); vp_l.append(vp); kip_l.append(kip); cp_l.append(cp)
        ks_l.append(ksn); vs_l.append(vsn); kis_l.append(kisn); cs_l.append(csn)

    y_prompt = rmsnorm(xp, g_final)
    y_sample = rmsnorm(xs, g_final)
    return (y_prompt, y_sample,
            jnp.stack(kp_l), jnp.stack(vp_l), jnp.stack(kip_l), jnp.stack(cp_l),
            jnp.stack(ks_l), jnp.stack(vs_l), jnp.stack(kis_l), jnp.stack(cs_l))
```

```python
import functools

import numpy as np
import jax
import jax.numpy as jnp
from jax import lax
from jax.experimental import pallas as pl
from jax.experimental.pallas import tpu as pltpu

HEAD_DIM = 128
N_HEADS = 32
N_KV_HEADS = 8
IDX_HEADS = 32
IDX_DIM = 64
TOPK_MAX = 256
CONV_WIDTH = 3
PAGE_SIZE = 128
ROPE_THETA = 10000.0
NORM_EPS = 1e-6
NEG = -1e30

LANES = 128
SUBLANES = 8
VMEM_LIMIT = 48 * 1024 * 1024
PAGES_PER_STEP = 8
INT_MIN = -2 ** 31

_F32 = jnp.float32
_BF16 = jnp.bfloat16


def _cparams(sem):
    return pltpu.CompilerParams(dimension_semantics=sem, vmem_limit_bytes=VMEM_LIMIT)


def _rmsnorm_kernel(x_ref, g_ref, o_ref):
    x = x_ref[...]
    y = x * lax.rsqrt(jnp.mean(x * x, axis=-1, keepdims=True) + NORM_EPS)
    o_ref[...] = (y * g_ref[...]).astype(o_ref.dtype)


def _rmsnorm(x, g, out_dtype):
    m, d = x.shape
    tm = min(m, 256)
    return pl.pallas_call(
        _rmsnorm_kernel,
        grid=(m // tm,),
        in_specs=[pl.BlockSpec((tm, d), lambda i: (i, 0)),
                  pl.BlockSpec((1, d), lambda i: (0, 0))],
        out_specs=pl.BlockSpec((tm, d), lambda i: (i, 0)),
        out_shape=jax.ShapeDtypeStruct((m, d), out_dtype),
        compiler_params=_cparams(("parallel",)),
        name="rmsnorm",
    )(x, g.reshape(1, d))


def _rope_slab(z, cos, sin, half):
    if 2 * half == LANES:
        partner = pltpu.roll(z, half, 1)
    else:
        lane = lax.broadcasted_iota(jnp.int32, z.shape, 1)
        first = (lane % (2 * half)) < half
        partner = jnp.where(first, pltpu.roll(z, LANES - half, 1), pltpu.roll(z, half, 1))
    return z * cos + partner * sin


def _mm_kernel(*refs, nk, epi, rope_half, scale, pair_major):
    it = iter(refs)
    x_ref, w_ref = next(it), next(it)
    cos_ref = sin_ref = res_ref = None
    if rope_half:
        cos_ref, sin_ref = next(it), next(it)
    if epi == "residual":
        res_ref = next(it)
    o_ref = next(it)
    acc_ref = next(it) if nk > 1 else None

    part = jnp.dot(x_ref[...], w_ref[...], preferred_element_type=_F32)

    def finish(z):
        if rope_half:
            cos, sin = cos_ref[...], sin_ref[...]
            for s in range(z.shape[1] // LANES):
                slab = _rope_slab(z[:, s * LANES:(s + 1) * LANES], cos, sin, rope_half)
                if scale != 1.0:
                    slab = slab * scale
                if pair_major:
                    o_ref[s] = slab.astype(o_ref.dtype)
                else:
                    o_ref[:, s * LANES:(s + 1) * LANES] = slab.astype(o_ref.dtype)
            return
        if epi == "relu2":
            r = jnp.maximum(z, 0.0)
            z = r * r
        elif epi == "residual":
            z = res_ref[...] + z
        o_ref[...] = z.astype(o_ref.dtype)

    if nk == 1:
        finish(part)
    else:
        k = pl.program_id(2)

        @pl.when(k == 0)
        def _():
            acc_ref[...] = part

        @pl.when(k > 0)
        def _():
            acc_ref[...] += part

        @pl.when(k == nk - 1)
        def _():
            finish(acc_ref[...])


def _matmul(x, w, *, out_dtype, epi="none", rope=None, scale=1.0, residual=None,
            pair_major=False):
    m, kdim = x.shape
    n = w.shape[1]
    tm = min(m, 512)
    tn = min(n, 512)
    tk = min(kdim, 4096)
    nk = kdim // tk
    grid = (n // tn, m // tm, nk)
    in_specs = [pl.BlockSpec((tm, tk), lambda j, i, k: (i, k)),
                pl.BlockSpec((tk, tn), lambda j, i, k: (k, j))]
    args = [x, w]
    rope_half = 0
    if rope is not None:
        rope_half, cos, sin = rope
        nper = cos.shape[0] // tm
        tspec = pl.BlockSpec((tm, LANES), lambda j, i, k: (i % nper, 0))
        in_specs += [tspec, tspec]
        args += [cos, sin]
    if epi == "residual":
        in_specs.append(pl.BlockSpec((tm, tn), lambda j, i, k: (i, j)))
        args.append(residual)
    if pair_major:
        out_shape = jax.ShapeDtypeStruct((n // LANES, m, LANES), out_dtype)
        out_spec = pl.BlockSpec((tn // LANES, tm, LANES), lambda j, i, k: (j, i, 0))
    else:
        out_shape = jax.ShapeDtypeStruct((m, n), out_dtype)
        out_spec = pl.BlockSpec((tm, tn), lambda j, i, k: (i, j))
    scratch = [pltpu.VMEM((tm, tn), _F32)] if nk > 1 else []
    return pl.pallas_call(
        functools.partial(_mm_kernel, nk=nk, epi=epi, rope_half=rope_half, scale=scale,
                          pair_major=pair_major),
        grid=grid,
        in_specs=in_specs,
        out_specs=out_spec,
        out_shape=out_shape,
        scratch_shapes=scratch,
        compiler_params=_cparams(("parallel", "parallel", "arbitrary")),
        name="matmul_" + epi + ("_rope%d" % rope_half if rope_half else ""),
    )(*args)


def _sortable(x):
    bits = pltpu.bitcast(x, jnp.int32)
    return jnp.where(bits < 0, bits ^ jnp.int32(0x7FFFFFFF), bits)


def _kth_largest(keys, n_sel, count_fn):
    cnt = count_fn((keys >= 0).astype(jnp.int32))
    thr0 = jnp.where(cnt >= n_sel, jnp.int32(0), jnp.int32(INT_MIN))

    def body(i, thr):
        cand = thr | jnp.left_shift(jnp.int32(1), 30 - i)
        cnt = count_fn((keys >= cand).astype(jnp.int32))
        return jnp.where(cnt >= n_sel, cand, thr)

    return lax.fori_loop(0, 31, body, thr0)


def _pattn_kernel(qi_ref, wit_ref, ki_ref, q_ref, k_ref, v_ref, o_ref, sc_ref, bias_ref, *,
                  n_sel, kv_rep):
    jq = pl.program_id(1)
    g = pl.program_id(2)
    s_len, tq = sc_ref.shape
    n_pairs = qi_ref.shape[0]

    @pl.when(g == 0)
    def _():
        kb = ki_ref[...].astype(_BF16)
        zeros = jnp.zeros_like(kb)
        k_even = jnp.concatenate([kb, zeros], axis=1)
        k_odd = jnp.concatenate([zeros, kb], axis=1)
        w_scale = (IDX_HEADS ** -0.5) * (IDX_DIM ** -0.5)
        nt = (((1,), (1,)), ((), ()))
        sc_ref[...] = jnp.zeros_like(sc_ref)

        def body(p, carry):
            x = qi_ref[p]
            d0 = lax.dot_general(k_even, x, nt, preferred_element_type=_F32)
            d1 = lax.dot_general(k_odd, x, nt, preferred_element_type=_F32)
            w0 = wit_ref[pl.ds(2 * p, 1), :] * w_scale
            w1 = wit_ref[pl.ds(2 * p + 1, 1), :] * w_scale
            sc_ref[...] += jnp.maximum(d0, 0.0) * w0 + jnp.maximum(d1, 0.0) * w1
            return carry

        lax.fori_loop(0, n_pairs, body, 0)

        key_pos = lax.broadcasted_iota(jnp.int32, (s_len, tq), 0)
        q_pos = jq * tq + lax.broadcasted_iota(jnp.int32, (s_len, tq), 1)
        adm = key_pos <= q_pos
        keys = _sortable(jnp.where(adm, sc_ref[...], NEG))
        thr = _kth_largest(keys, n_sel, lambda c: jnp.sum(c, axis=0, keepdims=True))
        sel = (keys >= thr) & adm
        bias_ref[...] = jnp.where(sel, 0.0, NEG).astype(_F32).T

    kg = k_ref[...].astype(_BF16)
    vg = v_ref[...].astype(_BF16)
    bias = bias_ref[...]
    nt = (((1,), (1,)), ((), ()))
    for r in range(kv_rep):
        qh = q_ref[:, r * HEAD_DIM:(r + 1) * HEAD_DIM]
        s = lax.dot_general(qh, kg, nt, preferred_element_type=_F32) + bias
        m = jnp.max(s, axis=1, keepdims=True)
        p = jnp.exp(s - m)
        l = jnp.sum(p, axis=1, keepdims=True)
        o = jnp.dot(p.astype(_BF16), vg, preferred_element_type=_F32) / l
        o_ref[:, r * HEAD_DIM:(r + 1) * HEAD_DIM] = o.astype(o_ref.dtype)


def _prompt_attention(qi_pm, wit, ki, q, k, v, *, batch, seq):
    m, d = q.shape
    n_kv = k.shape[1] // HEAD_DIM
    kv_rep = (d // HEAD_DIM) // n_kv
    tq = min(seq, 256)
    nq = seq // tq
    n_sel = min(TOPK_MAX, seq // 4)
    n_pairs = qi_pm.shape[0]
    gw = kv_rep * HEAD_DIM
    return pl.pallas_call(
        functools.partial(_pattn_kernel, n_sel=n_sel, kv_rep=kv_rep),
        grid=(batch, nq, n_kv),
        in_specs=[
            pl.BlockSpec((n_pairs, tq, LANES), lambda b, j, g: (0, b * nq + j, 0)),
            pl.BlockSpec((wit.shape[0], tq), lambda b, j, g: (0, b * nq + j)),
            pl.BlockSpec((seq, ki.shape[1]), lambda b, j, g: (b, 0)),
            pl.BlockSpec((tq, gw), lambda b, j, g: (b * nq + j, g)),
            pl.BlockSpec((seq, HEAD_DIM), lambda b, j, g: (b, g)),
            pl.BlockSpec((seq, HEAD_DIM), lambda b, j, g: (b, g)),
        ],
        out_specs=pl.BlockSpec((tq, gw), lambda b, j, g: (b * nq + j, g)),
        out_shape=jax.ShapeDtypeStruct((m, d), _BF16),
        scratch_shapes=[pltpu.VMEM((seq, tq), _F32), pltpu.VMEM((tq, seq), _F32)],
        compiler_params=_cparams(("parallel", "parallel", "arbitrary")),
        name="prompt_attention",
    )(qi_pm, wit, ki, q, k, v)


def _sidx_kernel(pt_ref, qi_ref, w_ref, kin_ref, *rest, n_sel, t_real, n_chunks):
    pages = rest[:PAGES_PER_STEP]
    bias_ref, sc_ref = rest[PAGES_PER_STEP], rest[PAGES_PER_STEP + 1]
    c = pl.program_id(1)
    rows = SUBLANES
    chunk = PAGES_PER_STEP * PAGE_SIZE
    nt = (((1,), (1,)), ((), ()))
    w = w_ref[0] * ((IDX_HEADS ** -0.5) * (IDX_DIM ** -0.5))
    qi = qi_ref[0]

    def scores(keys_bf16):
        d = lax.dot_general(qi, keys_bf16, nt, preferred_element_type=_F32)
        d = jnp.maximum(d, 0.0) * w
        acc = d[0:rows]
        for h in range(1, IDX_HEADS):
            acc = acc + d[h * rows:(h + 1) * rows]
        return acc

    @pl.when(c < n_chunks)
    def _():
        keys = jnp.concatenate([p[0] for p in pages], axis=0).astype(_BF16)
        sc_ref[c] = scores(keys)

    @pl.when(c == n_chunks)
    def _():
        kn = kin_ref[...].astype(_BF16)
        kn = jnp.concatenate([kn, jnp.zeros((PAGE_SIZE - rows, kn.shape[1]), _BF16)], axis=0)
        s_new = scores(kn)
        s_new = jnp.concatenate([s_new, jnp.zeros((rows, chunk - PAGE_SIZE), _F32)], axis=1)
        shape = (n_chunks + 1, rows, chunk)
        ci = lax.broadcasted_iota(jnp.int32, shape, 0)
        qrow = lax.broadcasted_iota(jnp.int32, shape, 1)
        lane = lax.broadcasted_iota(jnp.int32, shape, 2)
        adm = (ci < n_chunks) | ((lane <= qrow) & (lane < t_real))
        sc_ref[n_chunks] = s_new
        keys = _sortable(jnp.where(adm, sc_ref[...], NEG))
        count = lambda x: jnp.sum(jnp.sum(x, axis=0, keepdims=True), axis=2, keepdims=True)
        thr = _kth_largest(keys, n_sel, count)
        sel = (keys >= thr) & adm
        bias_ref[0] = jnp.where(sel, 0.0, NEG).astype(_F32)


def _sample_index(page_table, qi_hm, w_hm, ki_new, pool_ki, *, t_real):
    bd, n_pages = page_table.shape
    n_chunks = n_pages // PAGES_PER_STEP
    chunk = PAGES_PER_STEP * PAGE_SIZE
    past = n_pages * PAGE_SIZE
    n_sel = min(TOPK_MAX, (past + t_real) // 4)
    hr = qi_hm.shape[1]

    def page_spec(i):
        return pl.BlockSpec(
            (1, PAGE_SIZE, IDX_DIM),
            lambda b, c, pt: (pt[b, jnp.minimum(c, n_chunks - 1) * PAGES_PER_STEP + i], 0, 0))

    grid_spec = pltpu.PrefetchScalarGridSpec(
        num_scalar_prefetch=1,
        grid=(bd, n_chunks + 1),
        in_specs=[pl.BlockSpec((1, hr, IDX_DIM), lambda b, c, pt: (b, 0, 0)),
                  pl.BlockSpec((1, hr, 1), lambda b, c, pt: (b, 0, 0)),
                  pl.BlockSpec((SUBLANES, IDX_DIM), lambda b, c, pt: (b, 0))]
                 + [page_spec(i) for i in range(PAGES_PER_STEP)],
        out_specs=pl.BlockSpec((1, n_chunks + 1, SUBLANES, chunk), lambda b, c, pt: (b, 0, 0, 0)),
        scratch_shapes=[pltpu.VMEM((n_chunks + 1, SUBLANES, chunk), _F32)],
    )
    return pl.pallas_call(
        functools.partial(_sidx_kernel, n_sel=n_sel, t_real=t_real, n_chunks=n_chunks),
        grid_spec=grid_spec,
        out_shape=jax.ShapeDtypeStruct((bd, n_chunks + 1, SUBLANES, chunk), _F32),
        compiler_params=_cparams(("parallel", "arbitrary")),
        name="sample_index",
    )(page_table, qi_hm, w_hm, ki_new, *([pool_ki] * PAGES_PER_STEP))


def _sattn_kernel(pt_ref, q_ref, bias_ref, kn_ref, vn_ref, *rest, n_chunks, n_kv, kv_rep):
    kpages = rest[:PAGES_PER_STEP]
    vpages = rest[PAGES_PER_STEP:2 * PAGES_PER_STEP]
    o_ref, m_ref, l_ref, acc_ref = rest[2 * PAGES_PER_STEP:]
    c = pl.program_id(1)
    rows = SUBLANES
    nt = (((1,), (1,)), ((), ()))

    @pl.when(c == 0)
    def _():
        m_ref[...] = jnp.full_like(m_ref, -jnp.inf)
        l_ref[...] = jnp.zeros_like(l_ref)
        acc_ref[...] = jnp.zeros_like(acc_ref)

    def update(g, kg, vg, bias):
        qg = jnp.concatenate(
            [q_ref[:, (g * kv_rep + r) * HEAD_DIM:(g * kv_rep + r + 1) * HEAD_DIM]
             for r in range(kv_rep)], axis=0).astype(_BF16)
        bg = jnp.concatenate([bias] * kv_rep, axis=0)
        s = lax.dot_general(qg, kg, nt, preferred_element_type=_F32) + bg
        m_old = m_ref[g]
        m_new = jnp.maximum(m_old, jnp.max(s, axis=1, keepdims=True))
        alpha = jnp.exp(m_old - m_new)
        p = jnp.exp(s - m_new)
        l_ref[g] = alpha * l_ref[g] + jnp.sum(p, axis=1, keepdims=True)
        acc_ref[g] = alpha * acc_ref[g] + jnp.dot(p.astype(_BF16), vg,
                                                  preferred_element_type=_F32)
        m_ref[g] = m_new

    @pl.when(c < n_chunks)
    def _():
        bias = bias_ref[0, 0]
        for g in range(n_kv):
            sl = slice(g * HEAD_DIM, (g + 1) * HEAD_DIM)
            kg = jnp.concatenate([p[0, :, sl] for p in kpages], axis=0).astype(_BF16)
            vg = jnp.concatenate([p[0, :, sl] for p in vpages], axis=0).astype(_BF16)
            update(g, kg, vg, bias)

    @pl.when(c == n_chunks)
    def _():
        bias = bias_ref[0, 0][:, :PAGE_SIZE]
        pad = jnp.zeros((PAGE_SIZE - rows, HEAD_DIM), _BF16)
        for g in range(n_kv):
            sl = slice(g * HEAD_DIM, (g + 1) * HEAD_DIM)
            kg = jnp.concatenate([kn_ref[:, sl].astype(_BF16), pad], axis=0)
            vg = jnp.concatenate([vn_ref[:, sl].astype(_BF16), pad], axis=0)
            update(g, kg, vg, bias)
            o = acc_ref[g] / l_ref[g]
            for r in range(kv_rep):
                h = g * kv_rep + r
                o_ref[:, h * HEAD_DIM:(h + 1) * HEAD_DIM] = (
                    o[r * rows:(r + 1) * rows].astype(o_ref.dtype))


def _sample_attention(page_table, q, bias, k_new, v_new, pool_k, pool_v):
    bd, n_pages = page_table.shape
    n_chunks = n_pages // PAGES_PER_STEP
    chunk = PAGES_PER_STEP * PAGE_SIZE
    m, d = q.shape
    kvw = k_new.shape[1]
    n_kv = kvw // HEAD_DIM
    kv_rep = (d // HEAD_DIM) // n_kv

    def page_spec(i):
        return pl.BlockSpec(
            (1, PAGE_SIZE, kvw),
            lambda b, c, pt: (pt[b, jnp.minimum(c, n_chunks - 1) * PAGES_PER_STEP + i], 0, 0))

    grid_spec = pltpu.PrefetchScalarGridSpec(
        num_scalar_prefetch=1,
        grid=(bd, n_chunks + 1),
        in_specs=[pl.BlockSpec((SUBLANES, d), lambda b, c, pt: (b, 0)),
                  pl.BlockSpec((1, 1, SUBLANES, chunk), lambda b, c, pt: (b, c, 0, 0)),
                  pl.BlockSpec((SUBLANES, kvw), lambda b, c, pt: (b, 0)),
                  pl.BlockSpec((SUBLANES, kvw), lambda b, c, pt: (b, 0))]
                 + [page_spec(i) for i in range(PAGES_PER_STEP)] * 2,
        out_specs=pl.BlockSpec((SUBLANES, d), lambda b, c, pt: (b, 0)),
        scratch_shapes=[pltpu.VMEM((n_kv, kv_rep * SUBLANES, 1), _F32),
                        pltpu.VMEM((n_kv, kv_rep * SUBLANES, 1), _F32),
                        pltpu.VMEM((n_kv, kv_rep * SUBLANES, HEAD_DIM), _F32)],
    )
    return pl.pallas_call(
        functools.partial(_sattn_kernel, n_chunks=n_chunks, n_kv=n_kv, kv_rep=kv_rep),
        grid_spec=grid_spec,
        out_shape=jax.ShapeDtypeStruct((m, d), _F32),
        compiler_params=_cparams(("parallel", "arbitrary")),
        name="sample_attention",
    )(page_table, q, bias, k_new, v_new,
      *([pool_k] * PAGES_PER_STEP), *([pool_v] * PAGES_PER_STEP))


def _merge_kernel(cb_ref, cc_ref, ch_ref, ga_ref, gc_ref, attn_ref, ccp_ref, chp_ref, st_ref,
                  taps_ref, o_ref, tail_ref, u_ref):
    i = pl.program_id(2)
    tm = cb_ref.shape[0]
    u = cc_ref[...] * ch_ref[...]
    u_ref[SUBLANES:, :] = u

    @pl.when(i == 0)
    def _():
        u_ref[0:SUBLANES, :] = st_ref[0]

    @pl.when(i > 0)
    def _():
        u_ref[0:SUBLANES, :] = ccp_ref[...] * chp_ref[...]

    y = u * taps_ref[CONV_WIDTH - 1:CONV_WIDTH, :]
    for j in range(CONV_WIDTH - 1):
        off = SUBLANES - (CONV_WIDTH - 1) + j
        y = y + u_ref[off:off + tm, :] * taps_ref[j:j + 1, :]
    merged = (jax.nn.sigmoid(ga_ref[...]) * attn_ref[...].astype(_F32)
              + jax.nn.sigmoid(gc_ref[...]) * (cb_ref[...] * y))
    o_ref[...] = merged.astype(o_ref.dtype)
    tail_ref[0] = u[tm - SUBLANES:, :]


def _merge(cb, cc, ch, ga, gc, attn, state8, taps, *, batch, seq):
    m, c = cb.shape
    tm = min(seq, 256)
    nt = seq // tm
    tc = min(c, 512)
    blocks_per_tile = tm // SUBLANES
    main = pl.BlockSpec((tm, tc), lambda b, j, i: (b * nt + i, j))
    prev = pl.BlockSpec(
        (SUBLANES, tc),
        lambda b, j, i: (jnp.maximum((b * nt + i) * blocks_per_tile - 1, 0), j))
    return pl.pallas_call(
        _merge_kernel,
        grid=(batch, c // tc, nt),
        in_specs=[main, main, main, main, main, main, prev, prev,
                  pl.BlockSpec((1, SUBLANES, tc), lambda b, j, i: (b, 0, j)),
                  pl.BlockSpec((CONV_WIDTH, tc), lambda b, j, i: (0, j))],
        out_specs=[main, pl.BlockSpec((1, SUBLANES, tc), lambda b, j, i: (b, 0, j))],
        out_shape=[jax.ShapeDtypeStruct((m, c), attn.dtype),
                   jax.ShapeDtypeStruct((batch, SUBLANES, c), _F32)],
        scratch_shapes=[pltpu.VMEM((tm + SUBLANES, tc), _F32)],
        compiler_params=_cparams(("parallel", "parallel", "arbitrary")),
        name="conv_merge",
    )(cb, cc, ch, ga, gc, attn, cc, ch, state8, taps)


def _rope_tables(positions, half, n_rot_lanes=LANES):
    pos = np.asarray(positions, np.float64)[:, None]
    lane = np.arange(LANES)
    inv = ROPE_THETA ** (-(lane % half).astype(np.float64) / half)
    ang = pos * inv[None, :]
    sign = np.where((lane % (2 * half)) < half, -1.0, 1.0)
    rot = (lane < n_rot_lanes)[None, :]
    cos = np.where(rot, np.cos(ang), 1.0)
    sin = np.where(rot, np.sin(ang) * sign[None, :], 0.0)
    return jnp.asarray(cos, _F32), jnp.asarray(sin, _F32)


def _split_points(d_model):
    widths = (N_HEADS * HEAD_DIM, N_KV_HEADS * HEAD_DIM, N_KV_HEADS * HEAD_DIM,
              IDX_HEADS * IDX_DIM, IDX_DIM, IDX_HEADS,
              d_model, d_model, d_model, N_HEADS * HEAD_DIM, d_model)
    starts = np.concatenate([[0], np.cumsum(widths)[:-1]])
    return [(int(s), int(w)) for s, w in zip(starts, widths)]


def _prep_weights(w_in_l, w_o_l, w_up_l, w_down_l):
    segs = _split_points(w_in_l.shape[0])
    cols = lambda i: w_in_l[:, segs[i][0]:segs[i][0] + segs[i][1]].astype(_BF16)
    kiwi = w_in_l[:, segs[4][0]:segs[4][0] + IDX_DIM + IDX_HEADS]
    kiwi = jnp.pad(kiwi, ((0, 0), (0, LANES - kiwi.shape[1]))).astype(_BF16)
    return dict(q=cols(0), k=cols(1), v=cols(2), qi=cols(3), kiwi=kiwi,
                cb=cols(6), cc=cols(7), ch=cols(8), ga=cols(9), gc=cols(10),
                o=w_o_l.astype(_BF16), up=w_up_l.astype(_BF16), down=w_down_l.astype(_BF16))


def _project(x, w, g_mix, tabs, q_dtype):
    h = _rmsnorm(x, g_mix, _BF16)
    r128, r64, rkiwi = tabs
    out = dict(
        q=_matmul(h, w["q"], out_dtype=q_dtype, rope=(HEAD_DIM // 2,) + r128,
                  scale=HEAD_DIM ** -0.5),
        k=_matmul(h, w["k"], out_dtype=_F32, rope=(HEAD_DIM // 2,) + r128),
        v=_matmul(h, w["v"], out_dtype=_F32),
        kiwi=_matmul(h, w["kiwi"], out_dtype=_F32, rope=(IDX_DIM // 2,) + rkiwi),
    )
    for name in ("cb", "cc", "ch", "ga", "gc"):
        out[name] = _matmul(h, w[name], out_dtype=_F32)
    return h, out


def _finish_layer(x, pr, attn, state8, taps, w, g_mlp, *, batch, seq):
    merged, tail = _merge(pr["cb"], pr["cc"], pr["ch"], pr["ga"], pr["gc"], attn, state8, taps,
                          batch=batch, seq=seq)
    x = _matmul(merged.astype(_BF16), w["o"], out_dtype=_F32, epi="residual", residual=x)
    h2 = _rmsnorm(x, g_mlp, _BF16)
    hid = _matmul(h2, w["up"], out_dtype=_BF16, epi="relu2")
    x = _matmul(hid, w["down"], out_dtype=_F32, epi="residual", residual=x)
    return x, tail


def kernel(x_prompt, x_sample, cache_k, cache_v, cache_kidx, state_conv, page_table,
           w_in, conv_w, w_o, g_mix, g_mlp, w_up, w_down, g_final):
    B, S, D = x_prompt.shape
    Bd, T, _ = x_sample.shape
    depth = w_in.shape[0]
    n_pool = cache_k.shape[1]
    past = page_table.shape[1] * PAGE_SIZE
    R = SUBLANES
    kvw = N_KV_HEADS * HEAD_DIM
    half, ihalf = HEAD_DIM // 2, IDX_DIM // 2

    tm_p = min(B * S, 512)
    pos_p = np.arange(max(S, tm_p)) % S
    tabs_p = (_rope_tables(pos_p, half), _rope_tables(pos_p, ihalf),
              _rope_tables(pos_p, ihalf, IDX_DIM))
    pos_s = np.tile(past + np.arange(R), Bd)
    tabs_s = (_rope_tables(pos_s, half), _rope_tables(pos_s, ihalf),
              _rope_tables(pos_s, ihalf, IDX_DIM))

    xp = x_prompt.reshape(B * S, D)
    xs = jnp.pad(x_sample, ((0, 0), (0, R - T), (0, 0))).reshape(Bd * R, D)
    zero_state = jnp.zeros((B, R, D), _F32)

    outs = {n: [] for n in ("kp", "vp", "kip", "cp", "ks", "vs", "kis", "cs")}
    for l in range(depth):
        w = _prep_weights(w_in[l], w_o[l], w_up[l], w_down[l])
        taps = conv_w[l]

        h, pr = _project(xp, w, g_mix[l], tabs_p, _BF16)
        qi_pm = _matmul(h, w["qi"], out_dtype=_BF16, rope=(ihalf,) + tabs_p[1], pair_major=True)
        ki = pr["kiwi"][:, :IDX_DIM]
        wit = pr["kiwi"][:, IDX_DIM:IDX_DIM + IDX_HEADS].T
        attn = _prompt_attention(qi_pm, wit, ki, pr["q"], pr["k"], pr["v"],
                                 batch=B, seq=S)
        xp, tail = _finish_layer(xp, pr, attn, zero_state, taps, w, g_mlp[l], batch=B, seq=S)
        outs["kp"].append(pr["k"].reshape(B, S, N_KV_HEADS, HEAD_DIM))
        outs["vp"].append(pr["v"].reshape(B, S, N_KV_HEADS, HEAD_DIM))
        outs["kip"].append(ki.reshape(B, S, IDX_DIM))
        outs["cp"].append(tail[:, R - (CONV_WIDTH - 1):, :])

        h, pr = _project(xs, w, g_mix[l], tabs_s, _F32)
        qi = _matmul(h, w["qi"], out_dtype=_BF16, rope=(ihalf,) + tabs_s[1])
        qi_hm = qi.reshape(Bd, R, IDX_HEADS, IDX_DIM).transpose(0, 2, 1, 3).reshape(
            Bd, IDX_HEADS * R, IDX_DIM)
        ki = pr["kiwi"][:, :IDX_DIM]
        w_hm = pr["kiwi"][:, IDX_DIM:IDX_DIM + IDX_HEADS].reshape(Bd, R, IDX_HEADS).transpose(
            0, 2, 1).reshape(Bd, IDX_HEADS * R, 1)
        bias = _sample_index(page_table, qi_hm, w_hm, ki, cache_kidx[l], t_real=T)
        attn = _sample_attention(page_table, pr["q"], bias, pr["k"], pr["v"],
                                 cache_k[l].reshape(n_pool, PAGE_SIZE, kvw),
                                 cache_v[l].reshape(n_pool, PAGE_SIZE, kvw))
        state8 = jnp.pad(state_conv[l], ((0, 0), (R - (CONV_WIDTH - 1), 0), (0, 0)))
        xs, tail = _finish_layer(xs, pr, attn, state8, taps, w, g_mlp[l], batch=Bd, seq=R)
        outs["ks"].append(pr["k"].reshape(Bd, R, N_KV_HEADS, HEAD_DIM)[:, :T])
        outs["vs"].append(pr["v"].reshape(Bd, R, N_KV_HEADS, HEAD_DIM)[:, :T])
        outs["kis"].append(ki.reshape(Bd, R, IDX_DIM)[:, :T])
        outs["cs"].append(tail[:, T - (CONV_WIDTH - 1):T, :])

    y_prompt = _rmsnorm(xp, g_final, _F32).reshape(B, S, D)
    y_sample = _rmsnorm(xs, g_final, _F32).reshape(Bd, R, D)[:, :T]
    st = lambda n: jnp.stack(outs[n])
    return (y_prompt, y_sample, st("kp"), st("vp"), st("kip"), st("cp"),
            st("ks"), st("vs"), st("kis"), st("cs"))
```

```python
import functools

import numpy as np
import jax
import jax.numpy as jnp
from jax import lax
from jax.experimental import pallas as pl
from jax.experimental.pallas import tpu as pltpu

HEAD_DIM = 128
N_HEADS = 32
N_KV_HEADS = 8
IDX_HEADS = 32
IDX_DIM = 64
TOPK_MAX = 256
CONV_WIDTH = 3
PAGE_SIZE = 128
ROPE_THETA = 10000.0
NORM_EPS = 1e-6
NEG = -1e30

LANES = 128
SUBLANES = 8
VMEM_LIMIT = 56 * 1024 * 1024
PAGES_PER_STEP = 8
INT_MIN = -2 ** 31
TN = 512
TAIL_SHIFT = IDX_DIM + IDX_HEADS

_F32 = jnp.float32
_BF16 = jnp.bfloat16
_NT = (((1,), (1,)), ((), ()))


def _cparams(sem):
    return pltpu.CompilerParams(dimension_semantics=sem, vmem_limit_bytes=VMEM_LIMIT)


def _rmsnorm_kernel(x_ref, g_ref, o_ref):
    x = x_ref[...]
    y = x * lax.rsqrt(jnp.mean(x * x, axis=-1, keepdims=True) + NORM_EPS)
    o_ref[...] = (y * g_ref[...]).astype(o_ref.dtype)


def _rmsnorm(x, g, out_dtype):
    m, d = x.shape
    tm = min(m, 256)
    return pl.pallas_call(
        _rmsnorm_kernel,
        grid=(m // tm,),
        in_specs=[pl.BlockSpec((tm, d), lambda i: (i, 0)),
                  pl.BlockSpec((1, d), lambda i: (0, 0))],
        out_specs=pl.BlockSpec((tm, d), lambda i: (i, 0)),
        out_shape=jax.ShapeDtypeStruct((m, d), out_dtype),
        compiler_params=_cparams(("arbitrary",)),
        name="rmsnorm",
    )(x, g.reshape(1, d))


def _rope_slab(z, cos, sin, half):
    if 2 * half == LANES:
        partner = pltpu.roll(z, half, 1)
    else:
        lane = lax.broadcasted_iota(jnp.int32, z.shape, 1)
        first = (lane % (2 * half)) < half
        partner = jnp.where(first, pltpu.roll(z, LANES - half, 1), pltpu.roll(z, half, 1))
    return z * cos + partner * sin


def _mm_kernel(*refs, starts, ntiles, nk, epi, rope_half, scale, pair_major):
    ng = len(starts)
    it = iter(refs)
    w_ref = next(it)
    groups = []
    for _ in range(ng):
        g = dict(x=next(it))
        if rope_half:
            g["cos"], g["sin"] = next(it), next(it)
        if epi == "residual":
            g["res"] = next(it)
        groups.append(g)
    for gi, g in enumerate(groups):
        g["o"] = next(it)
        g["pm"] = pair_major[gi]
    i = pl.program_id(1)
    k = pl.program_id(2)

    if nk == 1:
        wb_ref = next(it)

        @pl.when(i == 0)
        def _():
            wb_ref[...] = w_ref[0].astype(_BF16)

        w_val = lambda: wb_ref[...]
    else:
        w_val = lambda: w_ref[0].astype(_BF16)

    def finish(g, z):
        o_ref = g["o"]
        if rope_half:
            cos, sin = g["cos"][...], g["sin"][...]
            for s in range(z.shape[1] // LANES):
                slab = _rope_slab(z[:, s * LANES:(s + 1) * LANES], cos, sin, rope_half)
                if scale != 1.0:
                    slab = slab * scale
                if g["pm"]:
                    o_ref[s] = slab.astype(o_ref.dtype)
                else:
                    o_ref[:, s * LANES:(s + 1) * LANES] = slab.astype(o_ref.dtype)
            return
        if epi == "relu2":
            r = jnp.maximum(z, 0.0)
            z = r * r
        elif epi == "residual":
            z = g["res"][...] + z
        o_ref[...] = z.astype(o_ref.dtype)

    for gi, g in enumerate(groups):
        active = (i >= starts[gi]) & (i < starts[gi] + ntiles[gi])

        @pl.when(active)
        def _(g=g):
            if nk == 1:
                finish(g, jnp.dot(g["x"][...], w_val(), preferred_element_type=_F32))
            else:
                @pl.when(k == 0)
                def _():
                    g["o"][...] = g["res"][...]

                g["o"][...] += jnp.dot(g["x"][...], w_val(), preferred_element_type=_F32)


def _matmul(xs, w, layer, *, col_start, n_cols, out_dtype, tms, epi="none", rope=None,
            scale=1.0, residuals=None, pair_major=None, tn=TN, tk=None):
    kdim = w.shape[1]
    tn = min(tn, n_cols)
    tk = kdim if tk is None else tk
    nk = kdim // tk
    nj = -(-n_cols // tn)
    cb0 = col_start // tn
    assert col_start % tn == 0 and kdim % tk == 0
    assert nk == 1 or (epi == "residual" and out_dtype == _F32)
    ng = len(xs)
    pair_major = [False] * ng if pair_major is None else pair_major
    ntiles = [x.shape[0] // tm for x, tm in zip(xs, tms)]
    starts = [int(s) for s in np.concatenate([[0], np.cumsum(ntiles)[:-1]])]

    def tile(gi):
        return lambda i: jnp.clip(i - starts[gi], 0, ntiles[gi] - 1)

    in_specs = [pl.BlockSpec((1, tk, tn), lambda j, i, k: (layer, k, cb0 + j))]
    args = [w]
    rope_half = 0
    for gi in range(ng):
        tm, t = tms[gi], tile(gi)
        in_specs.append(pl.BlockSpec((tm, tk), lambda j, i, k, t=t: (t(i), k)))
        args.append(xs[gi])
        if rope is not None:
            rope_half = rope[0]
            cos, sin = rope[1][gi]
            nper = cos.shape[0] // tm
            tspec = pl.BlockSpec((tm, LANES), lambda j, i, k, t=t, nper=nper: (t(i) % nper, 0))
            in_specs += [tspec, tspec]
            args += [cos, sin]
        if epi == "residual":
            mode = dict(pipeline_mode=pl.Buffered(1)) if nk > 1 else {}
            in_specs.append(pl.BlockSpec((tm, tn), lambda j, i, k, t=t: (t(i), j), **mode))
            args.append(residuals[gi])
    out_shapes, out_specs = [], []
    for gi in range(ng):
        tm, t, m = tms[gi], tile(gi), xs[gi].shape[0]
        if pair_major[gi]:
            out_shapes.append(jax.ShapeDtypeStruct((nj * tn // LANES, m, LANES), out_dtype))
            out_specs.append(pl.BlockSpec((tn // LANES, tm, LANES),
                                          lambda j, i, k, t=t: (j, t(i), 0)))
        else:
            out_shapes.append(jax.ShapeDtypeStruct((m, nj * tn), out_dtype))
            out_specs.append(pl.BlockSpec((tm, tn), lambda j, i, k, t=t: (t(i), j)))
    scratch = [pltpu.VMEM((tk, tn), _BF16)] if nk == 1 else []
    return pl.pallas_call(
        functools.partial(_mm_kernel, starts=starts, ntiles=ntiles, nk=nk, epi=epi,
                          rope_half=rope_half, scale=scale, pair_major=tuple(pair_major)),
        grid=(nj, sum(ntiles), nk),
        in_specs=in_specs,
        out_specs=out_specs,
        out_shape=out_shapes,
        scratch_shapes=scratch,
        compiler_params=_cparams(("arbitrary", "arbitrary", "arbitrary")),
        name="matmul_" + epi + ("_rope%d" % rope_half if rope_half else ""),
    )(*args)


def _sortable(x):
    bits = pltpu.bitcast(x, jnp.int32)
    return jnp.where(bits < 0, bits ^ jnp.int32(0x7FFFFFFF), bits)


def _kth_largest(keys, n_sel, count_fn):
    cnt = count_fn((keys >= 0).astype(jnp.int32))
    thr0 = jnp.where(cnt >= n_sel, jnp.int32(0), jnp.int32(INT_MIN))

    def body(i, thr):
        cand = thr | jnp.left_shift(jnp.int32(1), 30 - i)
        cnt = count_fn((keys >= cand).astype(jnp.int32))
        return jnp.where(cnt >= n_sel, cand, thr)

    return lax.fori_loop(0, 31, body, thr0)


def _pattn_kernel(qi_ref, wit_ref, ki_ref, q_ref, k_ref, v_ref, o_ref, sc_ref, bias_ref, *,
                  n_sel, kv_rep):
    jq = pl.program_id(1)
    g = pl.program_id(2)
    s_len, tq = sc_ref.shape
    n_pairs = qi_ref.shape[0]

    def select(klen):
        kb = ki_ref[0:klen, :].astype(_BF16)
        zeros = jnp.zeros_like(kb)
        k_even = jnp.concatenate([kb, zeros], axis=1)
        k_odd = jnp.concatenate([zeros, kb], axis=1)
        w_scale = (IDX_HEADS ** -0.5) * (IDX_DIM ** -0.5)
        sc_ref[0:klen, :] = jnp.zeros((klen, tq), _F32)

        def body(p, carry):
            x = qi_ref[p]
            d0 = lax.dot_general(k_even, x, _NT, preferred_element_type=_F32)
            d1 = lax.dot_general(k_odd, x, _NT, preferred_element_type=_F32)
            w0 = wit_ref[pl.ds(2 * p, 1), :] * w_scale
            w1 = wit_ref[pl.ds(2 * p + 1, 1), :] * w_scale
            sc_ref[0:klen, :] += jnp.maximum(d0, 0.0) * w0 + jnp.maximum(d1, 0.0) * w1
            return carry

        lax.fori_loop(0, n_pairs, body, 0)

        key_pos = lax.broadcasted_iota(jnp.int32, (klen, tq), 0)
        q_pos = jq * tq + lax.broadcasted_iota(jnp.int32, (klen, tq), 1)
        adm = key_pos <= q_pos
        keys = _sortable(jnp.where(adm, sc_ref[0:klen, :], NEG))
        thr = _kth_largest(keys, n_sel, lambda c: jnp.sum(c, axis=0, keepdims=True))
        sel = (keys >= thr) & adm
        bias_ref[:, 0:klen] = jnp.where(sel, 0.0, NEG).astype(_F32).T

    def attend(klen):
        kg = k_ref[0:klen, :].astype(_BF16)
        vg = v_ref[0:klen, :].astype(_BF16)
        bias = bias_ref[:, 0:klen]
        for r in range(kv_rep):
            qh = q_ref[:, r * HEAD_DIM:(r + 1) * HEAD_DIM]
            s = lax.dot_general(qh, kg, _NT, preferred_element_type=_F32) + bias
            m = jnp.max(s, axis=1, keepdims=True)
            p = jnp.exp(s - m)
            l = jnp.sum(p, axis=1, keepdims=True)
            o = jnp.dot(p.astype(_BF16), vg, preferred_element_type=_F32) / l
            o_ref[:, r * HEAD_DIM:(r + 1) * HEAD_DIM] = o.astype(o_ref.dtype)

    for n in range(1, s_len // tq + 1):
        @pl.when(jq == n - 1)
        def _(klen=n * tq):
            @pl.when(g == 0)
            def _():
                select(klen)

            attend(klen)


def _prompt_attention(qi_pm, wit, ki, q, k, v, *, batch, seq):
    m, d = q.shape
    n_kv = k.shape[1] // HEAD_DIM
    kv_rep = (d // HEAD_DIM) // n_kv
    tq = min(seq, 256)
    nq = seq // tq
    n_sel = min(TOPK_MAX, seq // 4)
    n_pairs = qi_pm.shape[0]
    gw = kv_rep * HEAD_DIM
    return pl.pallas_call(
        functools.partial(_pattn_kernel, n_sel=n_sel, kv_rep=kv_rep),
        grid=(batch, nq, n_kv),
        in_specs=[
            pl.BlockSpec((n_pairs, tq, LANES), lambda b, j, g: (0, b * nq + j, 0)),
            pl.BlockSpec((wit.shape[0], tq), lambda b, j, g: (0, b * nq + j)),
            pl.BlockSpec((seq, ki.shape[1]), lambda b, j, g: (b, 0)),
            pl.BlockSpec((tq, gw), lambda b, j, g: (b * nq + j, g)),
            pl.BlockSpec((seq, HEAD_DIM), lambda b, j, g: (b, g)),
            pl.BlockSpec((seq, HEAD_DIM), lambda b, j, g: (b, g)),
        ],
        out_specs=pl.BlockSpec((tq, gw), lambda b, j, g: (b * nq + j, g)),
        out_shape=jax.ShapeDtypeStruct((m, d), _BF16),
        scratch_shapes=[pltpu.VMEM((seq, tq), _F32), pltpu.VMEM((tq, seq), _F32)],
        compiler_params=_cparams(("arbitrary", "arbitrary", "arbitrary")),
        name="prompt_attention",
    )(qi_pm, wit, ki, q, k, v)


def _sidx_kernel(pt_ref, qi_ref, w_ref, kin_ref, *rest, n_sel, t_real, n_chunks):
    pages = rest[:PAGES_PER_STEP]
    bias_ref, sc_ref = rest[PAGES_PER_STEP], rest[PAGES_PER_STEP + 1]
    c = pl.program_id(1)
    rows = SUBLANES
    chunk = PAGES_PER_STEP * PAGE_SIZE
    w = w_ref[0] * ((IDX_HEADS ** -0.5) * (IDX_DIM ** -0.5))
    qi = qi_ref[0]

    def scores(keys_bf16):
        d = lax.dot_general(qi, keys_bf16, _NT, preferred_element_type=_F32)
        d = jnp.maximum(d, 0.0) * w
        acc = d[0:rows]
        for h in range(1, IDX_HEADS):
            acc = acc + d[h * rows:(h + 1) * rows]
        return acc

    @pl.when(c < n_chunks)
    def _():
        keys = jnp.concatenate([p[0] for p in pages], axis=0).astype(_BF16)
        sc_ref[c] = scores(keys)

    @pl.when(c == n_chunks)
    def _():
        kn = kin_ref[...].astype(_BF16)
        kn = jnp.concatenate([kn, jnp.zeros((PAGE_SIZE - rows, kn.shape[1]), _BF16)], axis=0)
        s_new = scores(kn)
        s_new = jnp.concatenate([s_new, jnp.zeros((rows, chunk - PAGE_SIZE), _F32)], axis=1)
        shape = (n_chunks + 1, rows, chunk)
        ci = lax.broadcasted_iota(jnp.int32, shape, 0)
        qrow = lax.broadcasted_iota(jnp.int32, shape, 1)
        lane = lax.broadcasted_iota(jnp.int32, shape, 2)
        adm = (ci < n_chunks) | ((lane <= qrow) & (lane < t_real))
        sc_ref[n_chunks] = s_new
        keys = _sortable(jnp.where(adm, sc_ref[...], NEG))
        count = lambda x: jnp.sum(jnp.sum(x, axis=0, keepdims=True), axis=2, keepdims=True)
        thr = _kth_largest(keys, n_sel, count)
        sel = (keys >= thr) & adm
        bias_ref[0] = jnp.where(sel, 0.0, NEG).astype(_F32)


def _page_spec(width, n_chunks, page0, i):
    return pl.BlockSpec(
        (1, PAGE_SIZE, width),
        lambda b, c, pt: (page0 + pt[b, jnp.minimum(c, n_chunks - 1) * PAGES_PER_STEP + i], 0, 0))


def _sample_index(page_table, qi_hm, w_hm, ki_new, pool_ki, page0, row0, *, t_real):
    bd, n_pages = page_table.shape
    n_chunks = n_pages // PAGES_PER_STEP
    chunk = PAGES_PER_STEP * PAGE_SIZE
    past = n_pages * PAGE_SIZE
    n_sel = min(TOPK_MAX, (past + t_real) // 4)
    hr = qi_hm.shape[1]
    grid_spec = pltpu.PrefetchScalarGridSpec(
        num_scalar_prefetch=1,
        grid=(bd, n_chunks + 1),
        in_specs=[pl.BlockSpec((1, hr, IDX_DIM), lambda b, c, pt: (b, 0, 0)),
                  pl.BlockSpec((1, hr, 1), lambda b, c, pt: (b, 0, 0)),
                  pl.BlockSpec((SUBLANES, IDX_DIM), lambda b, c, pt: (row0 + b, 0))]
                 + [_page_spec(IDX_DIM, n_chunks, page0, i) for i in range(PAGES_PER_STEP)],
        out_specs=pl.BlockSpec((1, n_chunks + 1, SUBLANES, chunk), lambda b, c, pt: (b, 0, 0, 0)),
        scratch_shapes=[pltpu.VMEM((n_chunks + 1, SUBLANES, chunk), _F32)],
    )
    return pl.pallas_call(
        functools.partial(_sidx_kernel, n_sel=n_sel, t_real=t_real, n_chunks=n_chunks),
        grid_spec=grid_spec,
        out_shape=jax.ShapeDtypeStruct((bd, n_chunks + 1, SUBLANES, chunk), _F32),
        compiler_params=_cparams(("arbitrary", "arbitrary")),
        name="sample_index",
    )(page_table, qi_hm, w_hm, ki_new, *([pool_ki] * PAGES_PER_STEP))


def _sattn_kernel(pt_ref, q_ref, bias_ref, kn_ref, vn_ref, *rest, n_chunks, n_kv, kv_rep):
    kpages = rest[:PAGES_PER_STEP]
    vpages = rest[PAGES_PER_STEP:2 * PAGES_PER_STEP]
    o_ref, m_ref, l_ref, acc_ref = rest[2 * PAGES_PER_STEP:]
    c = pl.program_id(1)
    rows = SUBLANES

    @pl.when(c == 0)
    def _():
        m_ref[...] = jnp.full_like(m_ref, -jnp.inf)
        l_ref[...] = jnp.zeros_like(l_ref)
        acc_ref[...] = jnp.zeros_like(acc_ref)

    def update(g, kg, vg, bias):
        qg = jnp.concatenate(
            [q_ref[:, (g * kv_rep + r) * HEAD_DIM:(g * kv_rep + r + 1) * HEAD_DIM]
             for r in range(kv_rep)], axis=0).astype(_BF16)
        bg = jnp.concatenate([bias] * kv_rep, axis=0)
        s = lax.dot_general(qg, kg, _NT, preferred_element_type=_F32) + bg
        m_old = m_ref[g]
        m_new = jnp.maximum(m_old, jnp.max(s, axis=1, keepdims=True))
        alpha = jnp.exp(m_old - m_new)
        p = jnp.exp(s - m_new)
        l_ref[g] = alpha * l_ref[g] + jnp.sum(p, axis=1, keepdims=True)
        acc_ref[g] = alpha * acc_ref[g] + jnp.dot(p.astype(_BF16), vg,
                                                  preferred_element_type=_F32)
        m_ref[g] = m_new

    @pl.when(c < n_chunks)
    def _():
        bias = bias_ref[0, 0]
        for g in range(n_kv):
            sl = slice(g * HEAD_DIM, (g + 1) * HEAD_DIM)
            kg = jnp.concatenate([p[0, :, sl] for p in kpages], axis=0).astype(_BF16)
            vg = jnp.concatenate([p[0, :, sl] for p in vpages], axis=0).astype(_BF16)
            update(g, kg, vg, bias)

    @pl.when(c == n_chunks)
    def _():
        bias = bias_ref[0, 0][:, :PAGE_SIZE]
        pad = jnp.zeros((PAGE_SIZE - rows, HEAD_DIM), _BF16)
        for g in range(n_kv):
            sl = slice(g * HEAD_DIM, (g + 1) * HEAD_DIM)
            kg = jnp.concatenate([kn_ref[:, sl].astype(_BF16), pad], axis=0)
            vg = jnp.concatenate([vn_ref[:, sl].astype(_BF16), pad], axis=0)
            update(g, kg, vg, bias)
            o = acc_ref[g] / l_ref[g]
            for r in range(kv_rep):
                h = g * kv_rep + r
                o_ref[:, h * HEAD_DIM:(h + 1) * HEAD_DIM] = (
                    o[r * rows:(r + 1) * rows].astype(o_ref.dtype))


def _sample_attention(page_table, q, bias, k_new, v_new, pool_k, pool_v, page0):
    bd, n_pages = page_table.shape
    n_chunks = n_pages // PAGES_PER_STEP
    chunk = PAGES_PER_STEP * PAGE_SIZE
    m, d = q.shape
    kvw = k_new.shape[1]
    n_kv = kvw // HEAD_DIM
    kv_rep = (d // HEAD_DIM) // n_kv
    grid_spec = pltpu.PrefetchScalarGridSpec(
        num_scalar_prefetch=1,
        grid=(bd, n_chunks + 1),
        in_specs=[pl.BlockSpec((SUBLANES, d), lambda b, c, pt: (b, 0)),
                  pl.BlockSpec((1, 1, SUBLANES, chunk), lambda b, c, pt: (b, c, 0, 0)),
                  pl.BlockSpec((SUBLANES, kvw), lambda b, c, pt: (b, 0)),
                  pl.BlockSpec((SUBLANES, kvw), lambda b, c, pt: (b, 0))]
                 + [_page_spec(kvw, n_chunks, page0, i) for i in range(PAGES_PER_STEP)] * 2,
        out_specs=pl.BlockSpec((SUBLANES, d), lambda b, c, pt: (b, 0)),
        scratch_shapes=[pltpu.VMEM((n_kv, kv_rep * SUBLANES, 1), _F32),
                        pltpu.VMEM((n_kv, kv_rep * SUBLANES, 1), _F32),
                        pltpu.VMEM((n_kv, kv_rep * SUBLANES, HEAD_DIM), _F32)],
    )
    return pl.pallas_call(
        functools.partial(_sattn_kernel, n_chunks=n_chunks, n_kv=n_kv, kv_rep=kv_rep),
        grid_spec=grid_spec,
        out_shape=jax.ShapeDtypeStruct((m, d), _F32),
        compiler_params=_cparams(("arbitrary", "arbitrary")),
        name="sample_attention",
    )(page_table, q, bias, k_new, v_new,
      *([pool_k] * PAGES_PER_STEP), *([pool_v] * PAGES_PER_STEP))


def _merge_kernel(cb_ref, cc_ref, ch_ref, ga_ref, gc_ref, attn_ref, attnp_ref, ccp_ref, chp_ref,
                  st_ref, taps_ref, o_ref, tail_ref, u_ref, *, n_ch):
    j = pl.program_id(1)
    i = pl.program_id(2)
    tm, tc = cb_ref.shape
    u = (cc_ref[...] * ch_ref[...]).astype(_F32)
    u_ref[SUBLANES:, :] = u

    @pl.when(i == 0)
    def _():
        u_ref[0:SUBLANES, :] = st_ref[0]

    @pl.when(i > 0)
    def _():
        u_ref[0:SUBLANES, :] = (ccp_ref[...] * chp_ref[...]).astype(_F32)

    y = u * taps_ref[CONV_WIDTH - 1:CONV_WIDTH, :]
    for t in range(CONV_WIDTH - 1):
        off = SUBLANES - (CONV_WIDTH - 1) + t
        y = y + u_ref[off:off + tm, :] * taps_ref[t:t + 1, :]
    attn = jnp.concatenate(
        [attnp_ref[...].astype(_F32)[:, LANES - TAIL_SHIFT:],
         attn_ref[...].astype(_F32)[:, :tc - TAIL_SHIFT]], axis=1)
    merged = (jax.nn.sigmoid(ga_ref[...].astype(_F32)) * attn
              + jax.nn.sigmoid(gc_ref[...].astype(_F32)) * (cb_ref[...].astype(_F32) * y))
    cs = j * tc + lax.broadcasted_iota(jnp.int32, (tm, tc), 1)
    valid = (cs >= TAIL_SHIFT) & (cs < TAIL_SHIFT + n_ch)
    o_ref[...] = jnp.where(valid, merged, 0.0).astype(o_ref.dtype)
    tail_ref[0] = u[tm - SUBLANES:, :]


def _merge(z, attn, state_s, taps_s, *, batch, seq, n_ch, out_dtype):
    m = z.shape[0]
    tm = min(seq, 256)
    nt = seq // tm
    tc = TN
    nseg = n_ch // tc
    nj = nseg + 1
    bpt = tm // SUBLANES
    apb = tc // LANES
    row = lambda b, i: b * nt + i
    seg = lambda s: pl.BlockSpec((tm, tc), lambda b, j, i: (row(b, i), j + s * nseg))
    prev = lambda s: pl.BlockSpec(
        (SUBLANES, tc), lambda b, j, i: (jnp.maximum(row(b, i) * bpt - 1, 0), j + s * nseg))
    return pl.pallas_call(
        functools.partial(_merge_kernel, n_ch=n_ch),
        grid=(batch, nj, nt),
        in_specs=[seg(0), seg(1), seg(2), seg(3), seg(4),
                  pl.BlockSpec((tm, tc), lambda b, j, i: (row(b, i), jnp.minimum(j, nseg - 1))),
                  pl.BlockSpec((tm, LANES),
                               lambda b, j, i: (row(b, i), jnp.maximum(j * apb - 1, 0))),
                  prev(1), prev(2),
                  pl.BlockSpec((1, SUBLANES, tc), lambda b, j, i: (b, 0, j)),
                  pl.BlockSpec((CONV_WIDTH, tc), lambda b, j, i: (0, j))],
        out_specs=[pl.BlockSpec((tm, tc), lambda b, j, i: (row(b, i), j)),
                   pl.BlockSpec((1, SUBLANES, tc), lambda b, j, i: (b, 0, j))],
        out_shape=[jax.ShapeDtypeStruct((m, nj * tc), out_dtype),
                   jax.ShapeDtypeStruct((batch, SUBLANES, nj * tc), _F32)],
        scratch_shapes=[pltpu.VMEM((tm + SUBLANES, tc), _F32)],
        compiler_params=_cparams(("arbitrary", "arbitrary", "arbitrary")),
        name="conv_merge",
    )(z, z, z, z, z, attn, attn, z, z, state_s, taps_s)


def _rope_tables(positions, half, n_rot_lanes=LANES):
    pos = np.asarray(positions, np.float64)[:, None]
    lane = np.arange(LANES)
    inv = ROPE_THETA ** (-(lane % half).astype(np.float64) / half)
    ang = pos * inv[None, :]
    sign = np.where((lane % (2 * half)) < half, -1.0, 1.0)
    rot = (lane < n_rot_lanes)[None, :]
    cos = np.where(rot, np.cos(ang), 1.0)
    sin = np.where(rot, np.sin(ang) * sign[None, :], 0.0)
    return jnp.asarray(cos, _F32), jnp.asarray(sin, _F32)


def _shift_pad(a, width):
    pads = [(0, 0)] * (a.ndim - 1) + [(TAIL_SHIFT, width - TAIL_SHIFT - a.shape[-1])]
    return jnp.pad(a, pads)


def kernel(x_prompt, x_sample, cache_k, cache_v, cache_kidx, state_conv, page_table,
           w_in, conv_w, w_o, g_mix, g_mlp, w_up, w_down, g_final):
    B, S, D = x_prompt.shape
    Bd, T, _ = x_sample.shape
    depth = w_in.shape[0]
    n_pool = cache_k.shape[1]
    past = page_table.shape[1] * PAGE_SIZE
    R = SUBLANES
    kvw = N_KV_HEADS * HEAD_DIM
    half, ihalf = HEAD_DIM // 2, IDX_DIM // 2
    Mp, Ms = B * S, Bd * R
    tms = [min(Mp, 1024), Ms]
    tms_down = [min(Mp, 2048), Ms]

    c_q, c_k = 0, N_HEADS * HEAD_DIM
    c_v = c_k + kvw
    c_qi = c_v + kvw
    c_tail = c_qi + IDX_HEADS * IDX_DIM
    n_tail = TAIL_SHIFT + 5 * D
    Ds = D + TN

    pos_p = np.arange(max(S, tms[0])) % S
    pos_s = np.tile(past + np.arange(R), Bd)
    r128 = [_rope_tables(pos_p, half), _rope_tables(pos_s, half)]
    r64 = [_rope_tables(pos_p, ihalf), _rope_tables(pos_s, ihalf)]
    rkiwi = [_rope_tables(pos_p, ihalf, IDX_DIM), _rope_tables(pos_s, ihalf, IDX_DIM)]

    xp = x_prompt.reshape(Mp, D)
    xs = jnp.pad(x_sample, ((0, 0), (0, R - T), (0, 0))).reshape(Ms, D)
    pool_k = cache_k.reshape(depth * n_pool, PAGE_SIZE, kvw)
    pool_v = cache_v.reshape(depth * n_pool, PAGE_SIZE, kvw)
    pool_ki = cache_kidx.reshape(depth * n_pool, PAGE_SIZE, IDX_DIM)
    zero_state = jnp.zeros((B, R, Ds), _F32)

    outs = {n: [] for n in ("kp", "vp", "kip", "cp", "ks", "vs", "kis", "cs")}
    for l in range(depth):
        hs = [_rmsnorm(xp, g_mix[l], _BF16), _rmsnorm(xs, g_mix[l], _BF16)]
        mm = functools.partial(_matmul, hs, w_in, l, tms=tms)
        qp, qs = mm(col_start=c_q, n_cols=N_HEADS * HEAD_DIM, out_dtype=_BF16,
                    rope=(half, r128), scale=HEAD_DIM ** -0.5)
        kp, ks = mm(col_start=c_k, n_cols=kvw, out_dtype=_F32, rope=(half, r128))
        vp, vs = mm(col_start=c_v, n_cols=kvw, out_dtype=_F32)
        qip, qis = mm(col_start=c_qi, n_cols=IDX_HEADS * IDX_DIM, out_dtype=_BF16,
                      rope=(ihalf, r64), pair_major=[True, False])
        kwp, kws = mm(col_start=c_tail, n_cols=LANES, out_dtype=_F32, rope=(ihalf, rkiwi),
                      tn=LANES)
        zp, zs = mm(col_start=c_tail, n_cols=n_tail, out_dtype=_F32)
        page0 = l * n_pool

        kip = kwp[:, :IDX_DIM]
        wit = kwp[:, IDX_DIM:IDX_DIM + IDX_HEADS].T
        attn_p = _prompt_attention(qip, wit, kip, qp, kp, vp, batch=B, seq=S)

        qi_hm = qis.reshape(Bd, R, IDX_HEADS, IDX_DIM).transpose(0, 2, 1, 3).reshape(
            Bd, IDX_HEADS * R, IDX_DIM)
        kis = kws[:, :IDX_DIM]
        w_hm = kws[:, IDX_DIM:IDX_DIM + IDX_HEADS].reshape(Bd, R, IDX_HEADS).transpose(
            0, 2, 1).reshape(Bd, IDX_HEADS * R, 1)
        bias = _sample_index(page_table, qi_hm, w_hm, kis, pool_ki, page0, 0, t_real=T)
        attn_s = _sample_attention(page_table, qs.astype(_F32), bias, ks, vs, pool_k, pool_v,
                                   page0)

        taps_s = _shift_pad(conv_w[l], Ds)
        state_s = _shift_pad(
            jnp.pad(state_conv[l], ((0, 0), (R - (CONV_WIDTH - 1), 0), (0, 0))), Ds)
        mg_p, tail_p = _merge(zp, attn_p, zero_state, taps_s, batch=B, seq=S, n_ch=D,
                              out_dtype=_BF16)
        mg_s, tail_s = _merge(zs, attn_s, state_s, taps_s, batch=Bd, seq=R, n_ch=D,
                              out_dtype=_F32)
        w_o_s = jnp.pad(w_o[l], ((TAIL_SHIFT, Ds - TAIL_SHIFT - D), (0, 0)))[None]
        xp, xs = _matmul([mg_p, mg_s.astype(_BF16)], w_o_s, 0, col_start=0, n_cols=D,
                         out_dtype=_F32, tms=tms, epi="residual", residuals=[xp, xs])
        h2 = [_rmsnorm(xp, g_mlp[l], _BF16), _rmsnorm(xs, g_mlp[l], _BF16)]
        hid = _matmul(h2, w_up, l, col_start=0, n_cols=w_up.shape[2], out_dtype=_BF16, tms=tms,
                      epi="relu2")
        xp, xs = _matmul(hid, w_down, l, col_start=0, n_cols=D, out_dtype=_F32, tms=tms_down,
                         epi="residual", residuals=[xp, xs], tn=min(D, 1024),
                         tk=min(w_down.shape[1], 1024))

        outs["kp"].append(kp.reshape(B, S, N_KV_HEADS, HEAD_DIM))
        outs["vp"].append(vp.reshape(B, S, N_KV_HEADS, HEAD_DIM))
        outs["kip"].append(kip.reshape(B, S, IDX_DIM))
        outs["cp"].append(tail_p[:, R - (CONV_WIDTH - 1):, TAIL_SHIFT:TAIL_SHIFT + D])
        outs["ks"].append(ks.reshape(Bd, R, N_KV_HEADS, HEAD_DIM)[:, :T])
        outs["vs"].append(vs.reshape(Bd, R, N_KV_HEADS, HEAD_DIM)[:, :T])
        outs["kis"].append(kis.reshape(Bd, R, IDX_DIM)[:, :T])
        outs["cs"].append(tail_s[:, T - (CONV_WIDTH - 1):T, TAIL_SHIFT:TAIL_SHIFT + D])

    y_prompt = _rmsnorm(xp, g_final, _F32).reshape(B, S, D)
    y_sample = _rmsnorm(xs, g_final, _F32).reshape(Bd, R, D)[:, :T]
    st = lambda n: jnp.stack(outs[n])
    return (y_prompt, y_sample, st("kp"), st("vp"), st("kip"), st("cp"),
            st("ks"), st("vs"), st("kis"), st("cs"))
```

```python
import functools

import numpy as np
import jax
import jax.numpy as jnp
from jax import lax
from jax.experimental import pallas as pl
from jax.experimental.pallas import tpu as pltpu

HEAD_DIM = 128
N_HEADS = 32
N_KV_HEADS = 8
IDX_HEADS = 32
IDX_DIM = 64
TOPK_MAX = 256
CONV_WIDTH = 3
PAGE_SIZE = 128
ROPE_THETA = 10000.0
NORM_EPS = 1e-6
NEG = -1e30

LANES = 128
SUBLANES = 8
VMEM_LIMIT = 56 * 1024 * 1024
PAGES_PER_STEP = 8
INT_MIN = -2 ** 31
TN = 512
TAIL_SHIFT = IDX_DIM + IDX_HEADS

_F32 = jnp.float32
_BF16 = jnp.bfloat16
_NT = (((1,), (1,)), ((), ()))


def _cparams(sem):
    return pltpu.CompilerParams(dimension_semantics=sem, vmem_limit_bytes=VMEM_LIMIT)


def _rmsnorm_kernel(x_ref, g_ref, o_ref):
    x = x_ref[...]
    y = x * lax.rsqrt(jnp.mean(x * x, axis=-1, keepdims=True) + NORM_EPS)
    o_ref[...] = (y * g_ref[...]).astype(o_ref.dtype)


def _rmsnorm(x, g, out_dtype):
    m, d = x.shape
    tm = min(m, 256)
    return pl.pallas_call(
        _rmsnorm_kernel,
        grid=(m // tm,),
        in_specs=[pl.BlockSpec((tm, d), lambda i: (i, 0)),
                  pl.BlockSpec((1, d), lambda i: (0, 0))],
        out_specs=pl.BlockSpec((tm, d), lambda i: (i, 0)),
        out_shape=jax.ShapeDtypeStruct((m, d), out_dtype),
        compiler_params=_cparams(("arbitrary",)),
        name="rmsnorm",
    )(x, g.reshape(1, d))


def _rope_slab(z, cos, sin, half):
    if 2 * half == LANES:
        partner = pltpu.roll(z, half, 1)
    else:
        lane = lax.broadcasted_iota(jnp.int32, z.shape, 1)
        first = (lane % (2 * half)) < half
        partner = jnp.where(first, pltpu.roll(z, LANES - half, 1), pltpu.roll(z, half, 1))
    return z * cos + partner * sin


def _mm_kernel(*refs, starts, ntiles, nk, epi, rope_half, scale, pair_major, w_t):
    ng = len(starts)
    it = iter(refs)
    w_ref = next(it)
    groups = []
    for _ in range(ng):
        g = dict(x=next(it))
        if rope_half:
            g["cos"], g["sin"] = next(it), next(it)
        if epi == "residual":
            g["res"] = next(it)
        groups.append(g)
    for gi, g in enumerate(groups):
        g["o"] = next(it)
        g["pm"] = pair_major[gi]
    i = pl.program_id(1)
    k = pl.program_id(2)

    if nk == 1:
        wb_ref = next(it)

        @pl.when(i == 0)
        def _():
            wb_ref[...] = w_ref[0].astype(_BF16)

        w_val = lambda: wb_ref[...]
    else:
        w_val = lambda: w_ref[0].astype(_BF16)

    def finish(g, z):
        o_ref = g["o"]
        if rope_half:
            cos, sin = g["cos"][...], g["sin"][...]
            for s in range(z.shape[1] // LANES):
                slab = _rope_slab(z[:, s * LANES:(s + 1) * LANES], cos, sin, rope_half)
                if scale != 1.0:
                    slab = slab * scale
                if g["pm"]:
                    o_ref[s] = slab.astype(o_ref.dtype)
                else:
                    o_ref[:, s * LANES:(s + 1) * LANES] = slab.astype(o_ref.dtype)
            return
        if epi == "relu2":
            r = jnp.maximum(z, 0.0)
            z = r * r
        elif epi == "residual":
            z = g["res"][...] + z
        o_ref[...] = z.astype(o_ref.dtype)

    for gi, g in enumerate(groups):
        active = (i >= starts[gi]) & (i < starts[gi] + ntiles[gi])

        @pl.when(active)
        def _(g=g):
            if nk == 1:
                if w_t:
                    z = lax.dot_general(g["x"][...], w_val(), _NT, preferred_element_type=_F32)
                else:
                    z = jnp.dot(g["x"][...], w_val(), preferred_element_type=_F32)
                finish(g, z)
            else:
                @pl.when(k == 0)
                def _():
                    g["o"][...] = g["res"][...]

                g["o"][...] += jnp.dot(g["x"][...], w_val(), preferred_element_type=_F32)


def _matmul(xs, w, layer, *, col_start, n_cols, out_dtype, tms, epi="none", rope=None,
            scale=1.0, residuals=None, pair_major=None, tn=TN, tk=None, w_t=False):
    kdim = w.shape[2] if w_t else w.shape[1]
    tn = min(tn, n_cols)
    tk = kdim if tk is None else tk
    nk = kdim // tk
    nj = n_cols // tn
    cb0 = col_start // tn
    assert n_cols % tn == 0 and kdim % tk == 0
    assert col_start % (SUBLANES if w_t else tn) == 0
    assert nk == 1 or (epi == "residual" and out_dtype == _F32 and not w_t)
    ng = len(xs)
    pair_major = [False] * ng if pair_major is None else pair_major
    ntiles = [x.shape[0] // tm for x, tm in zip(xs, tms)]
    starts = [int(s) for s in np.concatenate([[0], np.cumsum(ntiles)[:-1]])]

    def tile(gi):
        return lambda i: jnp.clip(i - starts[gi], 0, ntiles[gi] - 1)

    if w_t:
        in_specs = [pl.BlockSpec((pl.Element(1), pl.Element(tn), pl.Element(tk)),
                                 lambda j, i, k: (layer,
                                                  pl.multiple_of(col_start + j * tn, SUBLANES),
                                                  0))]
    else:
        in_specs = [pl.BlockSpec((1, tk, tn), lambda j, i, k: (layer, k, cb0 + j))]
    args = [w]
    rope_half = 0
    for gi in range(ng):
        tm, t = tms[gi], tile(gi)
        in_specs.append(pl.BlockSpec((tm, tk), lambda j, i, k, t=t: (t(i), k)))
        args.append(xs[gi])
        if rope is not None:
            rope_half = rope[0]
            cos, sin = rope[1][gi]
            nper = cos.shape[0] // tm
            tspec = pl.BlockSpec((tm, LANES), lambda j, i, k, t=t, nper=nper: (t(i) % nper, 0))
            in_specs += [tspec, tspec]
            args += [cos, sin]
        if epi == "residual":
            mode = dict(pipeline_mode=pl.Buffered(1)) if nk > 1 else {}
            in_specs.append(pl.BlockSpec((tm, tn), lambda j, i, k, t=t: (t(i), j), **mode))
            args.append(residuals[gi])
    out_shapes, out_specs = [], []
    for gi in range(ng):
        tm, t, m = tms[gi], tile(gi), xs[gi].shape[0]
        if pair_major[gi]:
            out_shapes.append(jax.ShapeDtypeStruct((nj * tn // LANES, m, LANES), out_dtype))
            out_specs.append(pl.BlockSpec((tn // LANES, tm, LANES),
                                          lambda j, i, k, t=t: (j, t(i), 0)))
        else:
            out_shapes.append(jax.ShapeDtypeStruct((m, nj * tn), out_dtype))
            out_specs.append(pl.BlockSpec((tm, tn), lambda j, i, k, t=t: (t(i), j)))
    scratch = [pltpu.VMEM((tn, tk) if w_t else (tk, tn), _BF16)] if nk == 1 else []
    return pl.pallas_call(
        functools.partial(_mm_kernel, starts=starts, ntiles=ntiles, nk=nk, epi=epi,
                          rope_half=rope_half, scale=scale, pair_major=tuple(pair_major),
                          w_t=w_t),
        grid=(nj, sum(ntiles), nk),
        in_specs=in_specs,
        out_specs=out_specs,
        out_shape=out_shapes,
        scratch_shapes=scratch,
        compiler_params=_cparams(("arbitrary", "arbitrary", "arbitrary")),
        name="matmul_" + epi + ("_rope%d" % rope_half if rope_half else ""),
    )(*args)


def _sortable(x):
    bits = pltpu.bitcast(x, jnp.int32)
    return jnp.where(bits < 0, bits ^ jnp.int32(0x7FFFFFFF), bits)


def _kth_largest(keys, n_sel, count_fn):
    cnt = count_fn((keys >= 0).astype(jnp.int32))
    thr0 = jnp.where(cnt >= n_sel, jnp.int32(0), jnp.int32(INT_MIN))

    def body(i, thr):
        cand = thr | jnp.left_shift(jnp.int32(1), 30 - i)
        cnt = count_fn((keys >= cand).astype(jnp.int32))
        return jnp.where(cnt >= n_sel, cand, thr)

    return lax.fori_loop(0, 31, body, thr0)


def _pattn_kernel(qi_ref, wit_ref, ki_ref, q_ref, k_ref, v_ref, o_ref, sc_ref, bias_ref, *,
                  n_sel, kv_rep):
    jq = pl.program_id(1)
    g = pl.program_id(2)
    s_len, tq = sc_ref.shape
    n_pairs = qi_ref.shape[0]

    def select(klen):
        kb = ki_ref[0:klen, :].astype(_BF16)
        zeros = jnp.zeros_like(kb)
        k_even = jnp.concatenate([kb, zeros], axis=1)
        k_odd = jnp.concatenate([zeros, kb], axis=1)
        w_scale = (IDX_HEADS ** -0.5) * (IDX_DIM ** -0.5)
        sc_ref[0:klen, :] = jnp.zeros((klen, tq), _F32)

        def body(p, carry):
            x = qi_ref[p]
            d0 = lax.dot_general(k_even, x, _NT, preferred_element_type=_F32)
            d1 = lax.dot_general(k_odd, x, _NT, preferred_element_type=_F32)
            w0 = wit_ref[pl.ds(2 * p, 1), :] * w_scale
            w1 = wit_ref[pl.ds(2 * p + 1, 1), :] * w_scale
            sc_ref[0:klen, :] += jnp.maximum(d0, 0.0) * w0 + jnp.maximum(d1, 0.0) * w1
            return carry

        lax.fori_loop(0, n_pairs, body, 0)

        key_pos = lax.broadcasted_iota(jnp.int32, (klen, tq), 0)
        q_pos = jq * tq + lax.broadcasted_iota(jnp.int32, (klen, tq), 1)
        adm = key_pos <= q_pos
        keys = _sortable(jnp.where(adm, sc_ref[0:klen, :], NEG))
        thr = _kth_largest(keys, n_sel, lambda c: jnp.sum(c, axis=0, keepdims=True))
        sel = (keys >= thr) & adm
        bias_ref[:, 0:klen] = jnp.where(sel, 0.0, NEG).astype(_F32).T

    def attend(klen):
        kg = k_ref[0:klen, :].astype(_BF16)
        vg = v_ref[0:klen, :].astype(_BF16)
        bias = bias_ref[:, 0:klen]
        for r in range(kv_rep):
            qh = q_ref[:, r * HEAD_DIM:(r + 1) * HEAD_DIM]
            s = lax.dot_general(qh, kg, _NT, preferred_element_type=_F32) + bias
            m = jnp.max(s, axis=1, keepdims=True)
            p = jnp.exp(s - m)
            l = jnp.sum(p, axis=1, keepdims=True)
            o = jnp.dot(p.astype(_BF16), vg, preferred_element_type=_F32) / l
            o_ref[:, r * HEAD_DIM:(r + 1) * HEAD_DIM] = o.astype(o_ref.dtype)

    for n in range(1, s_len // tq + 1):
        @pl.when(jq == n - 1)
        def _(klen=n * tq):
            @pl.when(g == 0)
            def _():
                select(klen)

            attend(klen)


def _prompt_attention(qi_pm, wit, ki, q, k, v, *, batch, seq):
    m, d = q.shape
    n_kv = k.shape[1] // HEAD_DIM
    kv_rep = (d // HEAD_DIM) // n_kv
    tq = min(seq, 256)
    nq = seq // tq
    n_sel = min(TOPK_MAX, seq // 4)
    n_pairs = qi_pm.shape[0]
    gw = kv_rep * HEAD_DIM
    return pl.pallas_call(
        functools.partial(_pattn_kernel, n_sel=n_sel, kv_rep=kv_rep),
        grid=(batch, nq, n_kv),
        in_specs=[
            pl.BlockSpec((n_pairs, tq, LANES), lambda b, j, g: (0, b * nq + j, 0)),
            pl.BlockSpec((wit.shape[0], tq), lambda b, j, g: (0, b * nq + j)),
            pl.BlockSpec((seq, ki.shape[1]), lambda b, j, g: (b, 0)),
            pl.BlockSpec((tq, gw), lambda b, j, g: (b * nq + j, g)),
            pl.BlockSpec((seq, HEAD_DIM), lambda b, j, g: (b, g)),
            pl.BlockSpec((seq, HEAD_DIM), lambda b, j, g: (b, g)),
        ],
        out_specs=pl.BlockSpec((tq, gw), lambda b, j, g: (b * nq + j, g)),
        out_shape=jax.ShapeDtypeStruct((m, d), _BF16),
        scratch_shapes=[pltpu.VMEM((seq, tq), _F32), pltpu.VMEM((tq, seq), _F32)],
        compiler_params=_cparams(("arbitrary", "arbitrary", "arbitrary")),
        name="prompt_attention",
    )(qi_pm, wit, ki, q, k, v)


def _sidx_kernel(pt_ref, qi_ref, w_ref, kin_ref, *rest, n_sel, t_real, n_chunks):
    pages = rest[:PAGES_PER_STEP]
    bias_ref, sc_ref = rest[PAGES_PER_STEP], rest[PAGES_PER_STEP + 1]
    c = pl.program_id(1)
    rows = SUBLANES
    chunk = PAGES_PER_STEP * PAGE_SIZE
    w = w_ref[0] * ((IDX_HEADS ** -0.5) * (IDX_DIM ** -0.5))
    qi = qi_ref[0]

    def scores(d):
        d = jnp.maximum(d, 0.0) * w
        acc = d[0:rows]
        for h in range(1, IDX_HEADS):
            acc = acc + d[h * rows:(h + 1) * rows]
        return acc

    @pl.when(c < n_chunks)
    def _():
        keys_t = jnp.concatenate([p[0] for p in pages], axis=1).astype(_BF16)
        sc_ref[c] = scores(jnp.dot(qi, keys_t, preferred_element_type=_F32))

    @pl.when(c == n_chunks)
    def _():
        kn = kin_ref[...].astype(_BF16)
        kn = jnp.concatenate([kn, jnp.zeros((PAGE_SIZE - rows, kn.shape[1]), _BF16)], axis=0)
        s_new = scores(lax.dot_general(qi, kn, _NT, preferred_element_type=_F32))
        s_new = jnp.concatenate([s_new, jnp.zeros((rows, chunk - PAGE_SIZE), _F32)], axis=1)
        shape = (n_chunks + 1, rows, chunk)
        ci = lax.broadcasted_iota(jnp.int32, shape, 0)
        qrow = lax.broadcasted_iota(jnp.int32, shape, 1)
        lane = lax.broadcasted_iota(jnp.int32, shape, 2)
        adm = (ci < n_chunks) | ((lane <= qrow) & (lane < t_real))
        sc_ref[n_chunks] = s_new
        keys = _sortable(jnp.where(adm, sc_ref[...], NEG))
        count = lambda x: jnp.sum(jnp.sum(x, axis=0, keepdims=True), axis=2, keepdims=True)
        thr = _kth_largest(keys, n_sel, count)
        sel = (keys >= thr) & adm
        bias_ref[0] = jnp.where(sel, 0.0, NEG).astype(_F32)


def _page_spec(block, n_chunks, page0, i):
    zeros = (0,) * (len(block) - 1)
    return pl.BlockSpec(
        block,
        lambda b, c, pt: (page0 + pt[b, jnp.minimum(c, n_chunks - 1) * PAGES_PER_STEP + i],)
        + zeros)


def _sample_index(page_table, qi_hm, w_hm, ki_new, pool_ki, page0, row0, *, t_real):
    bd, n_pages = page_table.shape
    n_chunks = n_pages // PAGES_PER_STEP
    chunk = PAGES_PER_STEP * PAGE_SIZE
    past = n_pages * PAGE_SIZE
    n_sel = min(TOPK_MAX, (past + t_real) // 4)
    hr = qi_hm.shape[1]
    grid_spec = pltpu.PrefetchScalarGridSpec(
        num_scalar_prefetch=1,
        grid=(bd, n_chunks + 1),
        in_specs=[pl.BlockSpec((1, hr, IDX_DIM), lambda b, c, pt: (b, 0, 0)),
                  pl.BlockSpec((1, hr, 1), lambda b, c, pt: (b, 0, 0)),
                  pl.BlockSpec((SUBLANES, IDX_DIM), lambda b, c, pt: (row0 + b, 0))]
                 + [_page_spec((1, IDX_DIM, PAGE_SIZE), n_chunks, page0, i)
                    for i in range(PAGES_PER_STEP)],
        out_specs=pl.BlockSpec((1, n_chunks + 1, SUBLANES, chunk), lambda b, c, pt: (b, 0, 0, 0)),
        scratch_shapes=[pltpu.VMEM((n_chunks + 1, SUBLANES, chunk), _F32)],
    )
    return pl.pallas_call(
        functools.partial(_sidx_kernel, n_sel=n_sel, t_real=t_real, n_chunks=n_chunks),
        grid_spec=grid_spec,
        out_shape=jax.ShapeDtypeStruct((bd, n_chunks + 1, SUBLANES, chunk), _F32),
        compiler_params=_cparams(("arbitrary", "arbitrary")),
        name="sample_index",
    )(page_table, qi_hm, w_hm, ki_new, *([pool_ki] * PAGES_PER_STEP))


def _sattn_kernel(pt_ref, q_ref, bias_ref, kn_ref, vn_ref, *rest, n_chunks, n_kv, kv_rep):
    kpages = rest[:PAGES_PER_STEP]
    vpages = rest[PAGES_PER_STEP:2 * PAGES_PER_STEP]
    o_ref, m_ref, l_ref, acc_ref = rest[2 * PAGES_PER_STEP:]
    c = pl.program_id(1)
    rows = SUBLANES

    @pl.when(c == 0)
    def _():
        m_ref[...] = jnp.full_like(m_ref, -jnp.inf)
        l_ref[...] = jnp.zeros_like(l_ref)
        acc_ref[...] = jnp.zeros_like(acc_ref)

    def update(g, kg, vg, bias):
        qg = jnp.concatenate(
            [q_ref[:, (g * kv_rep + r) * HEAD_DIM:(g * kv_rep + r + 1) * HEAD_DIM]
             for r in range(kv_rep)], axis=0).astype(_BF16)
        bg = jnp.concatenate([bias] * kv_rep, axis=0)
        s = lax.dot_general(qg, kg, _NT, preferred_element_type=_F32) + bg
        m_old = m_ref[g]
        m_new = jnp.maximum(m_old, jnp.max(s, axis=1, keepdims=True))
        alpha = jnp.exp(m_old - m_new)
        p = jnp.exp(s - m_new)
        l_ref[g] = alpha * l_ref[g] + jnp.sum(p, axis=1, keepdims=True)
        acc_ref[g] = alpha * acc_ref[g] + jnp.dot(p.astype(_BF16), vg,
                                                  preferred_element_type=_F32)
        m_ref[g] = m_new

    @pl.when(c < n_chunks)
    def _():
        bias = bias_ref[0, 0]
        for g in range(n_kv):
            rows_g = pl.ds(g, PAGE_SIZE, stride=n_kv)
            kg = jnp.concatenate([p[rows_g, :] for p in kpages], axis=0).astype(_BF16)
            vg = jnp.concatenate([p[rows_g, :] for p in vpages], axis=0).astype(_BF16)
            update(g, kg, vg, bias)

    @pl.when(c == n_chunks)
    def _():
        bias = bias_ref[0, 0][:, :PAGE_SIZE]
        pad = jnp.zeros((PAGE_SIZE - rows, HEAD_DIM), _BF16)
        for g in range(n_kv):
            sl = slice(g * HEAD_DIM, (g + 1) * HEAD_DIM)
            kg = jnp.concatenate([kn_ref[:, sl].astype(_BF16), pad], axis=0)
            vg = jnp.concatenate([vn_ref[:, sl].astype(_BF16), pad], axis=0)
            update(g, kg, vg, bias)
            o = acc_ref[g] / l_ref[g]
            for r in range(kv_rep):
                h = g * kv_rep + r
                o_ref[:, h * HEAD_DIM:(h + 1) * HEAD_DIM] = (
                    o[r * rows:(r + 1) * rows].astype(o_ref.dtype))


def _sample_attention(page_table, q, bias, k_new, v_new, pool_k, pool_v, page0):
    bd, n_pages = page_table.shape
    n_chunks = n_pages // PAGES_PER_STEP
    chunk = PAGES_PER_STEP * PAGE_SIZE
    m, d = q.shape
    kvw = k_new.shape[1]
    n_kv = kvw // HEAD_DIM
    kv_rep = (d // HEAD_DIM) // n_kv
    grid_spec = pltpu.PrefetchScalarGridSpec(
        num_scalar_prefetch=1,
        grid=(bd, n_chunks + 1),
        in_specs=[pl.BlockSpec((SUBLANES, d), lambda b, c, pt: (b, 0)),
                  pl.BlockSpec((1, 1, SUBLANES, chunk), lambda b, c, pt: (b, c, 0, 0)),
                  pl.BlockSpec((SUBLANES, kvw), lambda b, c, pt: (b, 0)),
                  pl.BlockSpec((SUBLANES, kvw), lambda b, c, pt: (b, 0))]
                 + [_page_spec((PAGE_SIZE * n_kv, HEAD_DIM), n_chunks, page0, i)
                    for i in range(PAGES_PER_STEP)] * 2,
        out_specs=pl.BlockSpec((SUBLANES, d), lambda b, c, pt: (b, 0)),
        scratch_shapes=[pltpu.VMEM((n_kv, kv_rep * SUBLANES, 1), _F32),
                        pltpu.VMEM((n_kv, kv_rep * SUBLANES, 1), _F32),
                        pltpu.VMEM((n_kv, kv_rep * SUBLANES, HEAD_DIM), _F32)],
    )
    return pl.pallas_call(
        functools.partial(_sattn_kernel, n_chunks=n_chunks, n_kv=n_kv, kv_rep=kv_rep),
        grid_spec=grid_spec,
        out_shape=jax.ShapeDtypeStruct((m, d), _F32),
        compiler_params=_cparams(("arbitrary", "arbitrary")),
        name="sample_attention",
    )(page_table, q, bias, k_new, v_new,
      *([pool_k] * PAGES_PER_STEP), *([pool_v] * PAGES_PER_STEP))


def _merge_kernel(cb_ref, cc_ref, ch_ref, ga_ref, gc_ref, attn_ref, ccp_ref, chp_ref, st_ref,
                  taps_ref, o_ref, tail_ref, u_ref):
    i = pl.program_id(2)
    tm = cb_ref.shape[0]
    u = cc_ref[...] * ch_ref[...]
    u_ref[SUBLANES:, :] = u

    @pl.when(i == 0)
    def _():
        u_ref[0:SUBLANES, :] = st_ref[0]

    @pl.when(i > 0)
    def _():
        u_ref[0:SUBLANES, :] = ccp_ref[...] * chp_ref[...]

    y = u * taps_ref[CONV_WIDTH - 1:CONV_WIDTH, :]
    for t in range(CONV_WIDTH - 1):
        off = SUBLANES - (CONV_WIDTH - 1) + t
        y = y + u_ref[off:off + tm, :] * taps_ref[t:t + 1, :]
    merged = (jax.nn.sigmoid(ga_ref[...]) * attn_ref[...].astype(_F32)
              + jax.nn.sigmoid(gc_ref[...]) * (cb_ref[...] * y))
    o_ref[...] = merged.astype(o_ref.dtype)
    tail_ref[0] = u[tm - SUBLANES:, :]


def _merge(z, attn, state8, taps, *, batch, seq, out_dtype):
    m, c = attn.shape
    tm = min(seq, 256)
    nt = seq // tm
    tc = min(c, TN)
    nseg = c // tc
    bpt = tm // SUBLANES
    row = lambda b, i: b * nt + i
    seg = lambda s: pl.BlockSpec((tm, tc), lambda b, j, i: (row(b, i), j + s * nseg))
    prev = lambda s: pl.BlockSpec(
        (SUBLANES, tc), lambda b, j, i: (jnp.maximum(row(b, i) * bpt - 1, 0), j + s * nseg))
    return pl.pallas_call(
        _merge_kernel,
        grid=(batch, nseg, nt),
        in_specs=[seg(0), seg(1), seg(2), seg(3), seg(4),
                  pl.BlockSpec((tm, tc), lambda b, j, i: (row(b, i), j)),
                  prev(1), prev(2),
                  pl.BlockSpec((1, SUBLANES, tc), lambda b, j, i: (b, 0, j)),
                  pl.BlockSpec((CONV_WIDTH, tc), lambda b, j, i: (0, j))],
        out_specs=[pl.BlockSpec((tm, tc), lambda b, j, i: (row(b, i), j)),
                   pl.BlockSpec((1, SUBLANES, tc), lambda b, j, i: (b, 0, j))],
        out_shape=[jax.ShapeDtypeStruct((m, c), out_dtype),
                   jax.ShapeDtypeStruct((batch, SUBLANES, c), _F32)],
        scratch_shapes=[pltpu.VMEM((tm + SUBLANES, tc), _F32)],
        compiler_params=_cparams(("arbitrary", "arbitrary", "arbitrary")),
        name="conv_merge",
    )(z, z, z, z, z, attn, z, z, state8, taps)


def _rope_tables(positions, half, n_rot_lanes=LANES):
    pos = np.asarray(positions, np.float64)[:, None]
    lane = np.arange(LANES)
    inv = ROPE_THETA ** (-(lane % half).astype(np.float64) / half)
    ang = pos * inv[None, :]
    sign = np.where((lane % (2 * half)) < half, -1.0, 1.0)
    rot = (lane < n_rot_lanes)[None, :]
    cos = np.where(rot, np.cos(ang), 1.0)
    sin = np.where(rot, np.sin(ang) * sign[None, :], 0.0)
    return jnp.asarray(cos, _F32), jnp.asarray(sin, _F32)


def kernel(x_prompt, x_sample, cache_k, cache_v, cache_kidx, state_conv, page_table,
           w_in, conv_w, w_o, g_mix, g_mlp, w_up, w_down, g_final):
    B, S, D = x_prompt.shape
    Bd, T, _ = x_sample.shape
    depth = w_in.shape[0]
    n_pool = cache_k.shape[1]
    past = page_table.shape[1] * PAGE_SIZE
    R = SUBLANES
    kvw = N_KV_HEADS * HEAD_DIM
    half, ihalf = HEAD_DIM // 2, IDX_DIM // 2
    Mp, Ms = B * S, Bd * R
    tms = [min(Mp, 1024), Ms]
    tms_down = [min(Mp, 2048), Ms]

    c_q, c_k = 0, N_HEADS * HEAD_DIM
    c_v = c_k + kvw
    c_qi = c_v + kvw
    c_kiwi = c_qi + IDX_HEADS * IDX_DIM
    c_gate = c_kiwi + IDX_DIM + IDX_HEADS

    pos_p = np.arange(max(S, tms[0])) % S
    pos_s = np.tile(past + np.arange(R), Bd)
    r128 = [_rope_tables(pos_p, half), _rope_tables(pos_s, half)]
    r64 = [_rope_tables(pos_p, ihalf), _rope_tables(pos_s, ihalf)]
    rkiwi = [_rope_tables(pos_p, ihalf, IDX_DIM), _rope_tables(pos_s, ihalf, IDX_DIM)]

    xp = x_prompt.reshape(Mp, D)
    xs = jnp.pad(x_sample, ((0, 0), (0, R - T), (0, 0))).reshape(Ms, D)
    w_in_t = jnp.swapaxes(w_in, 1, 2)
    pool_k = cache_k.reshape(depth * n_pool * PAGE_SIZE * N_KV_HEADS, HEAD_DIM)
    pool_v = cache_v.reshape(depth * n_pool * PAGE_SIZE * N_KV_HEADS, HEAD_DIM)
    pool_ki_t = jnp.swapaxes(cache_kidx, 2, 3).reshape(depth * n_pool, IDX_DIM, PAGE_SIZE)
    zero_state = jnp.zeros((B, R, D), _F32)

    outs = {n: [] for n in ("kp", "vp", "kip", "cp", "ks", "vs", "kis", "cs")}
    for l in range(depth):
        hs = [_rmsnorm(xp, g_mix[l], _BF16), _rmsnorm(xs, g_mix[l], _BF16)]
        mm = functools.partial(_matmul, hs, w_in_t, l, tms=tms, w_t=True)
        qp, qs = mm(col_start=c_q, n_cols=N_HEADS * HEAD_DIM, out_dtype=_BF16,
                    rope=(half, r128), scale=HEAD_DIM ** -0.5)
        kp, ks = mm(col_start=c_k, n_cols=kvw, out_dtype=_F32, rope=(half, r128))
        vp, vs = mm(col_start=c_v, n_cols=kvw, out_dtype=_F32)
        qip, qis = mm(col_start=c_qi, n_cols=IDX_HEADS * IDX_DIM, out_dtype=_BF16,
                      rope=(ihalf, r64), pair_major=[True, False])
        kwp, kws = mm(col_start=c_kiwi, n_cols=LANES, out_dtype=_F32, rope=(ihalf, rkiwi))
        zp, zs = mm(col_start=c_gate, n_cols=5 * D, out_dtype=_F32)
        page0 = l * n_pool

        kip = kwp[:, :IDX_DIM]
        wit = kwp[:, IDX_DIM:IDX_DIM + IDX_HEADS].T
        attn_p = _prompt_attention(qip, wit, kip, qp, kp, vp, batch=B, seq=S)

        qi_hm = qis.reshape(Bd, R, IDX_HEADS, IDX_DIM).transpose(0, 2, 1, 3).reshape(
            Bd, IDX_HEADS * R, IDX_DIM)
        kis = kws[:, :IDX_DIM]
        w_hm = kws[:, IDX_DIM:IDX_DIM + IDX_HEADS].reshape(Bd, R, IDX_HEADS).transpose(
            0, 2, 1).reshape(Bd, IDX_HEADS * R, 1)
        bias = _sample_index(page_table, qi_hm, w_hm, kis, pool_ki_t, page0, 0, t_real=T)
        attn_s = _sample_attention(page_table, qs.astype(_F32), bias, ks, vs, pool_k, pool_v,
                                   page0)

        state8 = jnp.pad(state_conv[l], ((0, 0), (R - (CONV_WIDTH - 1), 0), (0, 0)))
        mg_p, tail_p = _merge(zp, attn_p, zero_state, conv_w[l], batch=B, seq=S, out_dtype=_BF16)
        mg_s, tail_s = _merge(zs, attn_s, state8, conv_w[l], batch=Bd, seq=R, out_dtype=_F32)
        xp, xs = _matmul([mg_p, mg_s.astype(_BF16)], w_o, l, col_start=0, n_cols=D,
                         out_dtype=_F32, tms=tms, epi="residual", residuals=[xp, xs])
        h2 = [_rmsnorm(xp, g_mlp[l], _BF16), _rmsnorm(xs, g_mlp[l], _BF16)]
        hid = _matmul(h2, w_up, l, col_start=0, n_cols=w_up.shape[2], out_dtype=_BF16, tms=tms,
                      epi="relu2")
        xp, xs = _matmul(hid, w_down, l, col_start=0, n_cols=D, out_dtype=_F32, tms=tms_down,
                         epi="residual", residuals=[xp, xs], tn=min(D, 1024),
                         tk=min(w_down.shape[1], 1024))

        outs["kp"].append(kp.reshape(B, S, N_KV_HEADS, HEAD_DIM))
        outs["vp"].append(vp.reshape(B, S, N_KV_HEADS, HEAD_DIM))
        outs["kip"].append(kip.reshape(B, S, IDX_DIM))
        outs["cp"].append(tail_p[:, R - (CONV_WIDTH - 1):, :])
        outs["ks"].append(ks.reshape(Bd, R, N_KV_HEADS, HEAD_DIM)[:, :T])
        outs["vs"].append(vs.reshape(Bd, R, N_KV_HEADS, HEAD_DIM)[:, :T])
        outs["kis"].append(kis.reshape(Bd, R, IDX_DIM)[:, :T])
        outs["cs"].append(tail_s[:, T - (CONV_WIDTH - 1):T, :])

    y_prompt = _rmsnorm(xp, g_final, _F32).reshape(B, S, D)
    y_sample = _rmsnorm(xs, g_final, _F32).reshape(Bd, R, D)[:, :T]
    st = lambda n: jnp.stack(outs[n])
    return (y_prompt, y_sample, st("kp"), st("vp"), st("kip"), st("cp"),
            st("ks"), st("vs"), st("kis"), st("cs"))
```

```python
import functools

import numpy as np
import jax
import jax.numpy as jnp
from jax import lax
from jax.experimental import pallas as pl
from jax.experimental.pallas import tpu as pltpu

HEAD_DIM = 128
N_HEADS = 32
N_KV_HEADS = 8
IDX_HEADS = 32
IDX_DIM = 64
TOPK_MAX = 256
CONV_WIDTH = 3
PAGE_SIZE = 128
ROPE_THETA = 10000.0
NORM_EPS = 1e-6
NEG = -1e30

LANES = 128
SUBLANES = 8
VMEM_LIMIT = 56 * 1024 * 1024
PAGES_PER_STEP = 8
INT_MIN = -2 ** 31
TN = 1024
SUB_TN = 512
MERGE_TC = 512

_F32 = jnp.float32
_BF16 = jnp.bfloat16
_NT = (((1,), (1,)), ((), ()))


def _cparams(sem):
    return pltpu.CompilerParams(dimension_semantics=sem, vmem_limit_bytes=VMEM_LIMIT)


def _rmsnorm_kernel(x_ref, g_ref, o_ref):
    x = x_ref[...]
    y = x * lax.rsqrt(jnp.mean(x * x, axis=-1, keepdims=True) + NORM_EPS)
    o_ref[...] = (y * g_ref[...]).astype(o_ref.dtype)


def _rmsnorm(x, g, out_dtype):
    m, d = x.shape
    tm = min(m, 256)
    return pl.pallas_call(
        _rmsnorm_kernel,
        grid=(m // tm,),
        in_specs=[pl.BlockSpec((tm, d), lambda i: (i, 0)),
                  pl.BlockSpec((1, d), lambda i: (0, 0))],
        out_specs=pl.BlockSpec((tm, d), lambda i: (i, 0)),
        out_shape=jax.ShapeDtypeStruct((m, d), out_dtype),
        compiler_params=_cparams(("arbitrary",)),
        name="rmsnorm",
    )(x, g.reshape(1, d))


def _rope_slab(z, cos, sin, half):
    if 2 * half == LANES:
        partner = pltpu.roll(z, half, 1)
    else:
        lane = lax.broadcasted_iota(jnp.int32, z.shape, 1)
        first = (lane % (2 * half)) < half
        partner = jnp.where(first, pltpu.roll(z, LANES - half, 1), pltpu.roll(z, half, 1))
    return z * cos + partner * sin


def _epilogue(g, z, c0, *, epi, rope_half, scale):
    o_ref = g["o"]
    width = z.shape[1]
    if rope_half:
        cos, sin = g["cos"][...], g["sin"][...]
        for s in range(width // LANES):
            slab = _rope_slab(z[:, s * LANES:(s + 1) * LANES], cos, sin, rope_half)
            if scale != 1.0:
                slab = slab * scale
            if g["pm"]:
                o_ref[c0 // LANES + s] = slab.astype(o_ref.dtype)
            else:
                o_ref[:, c0 + s * LANES:c0 + (s + 1) * LANES] = slab.astype(o_ref.dtype)
        return
    if epi == "relu2":
        r = jnp.maximum(z, 0.0)
        z = r * r
    elif epi == "residual":
        z = g["res"][:, c0:c0 + width] + z
    o_ref[:, c0:c0 + width] = z.astype(o_ref.dtype)


def _unpack_groups(it, ng, rope_half, epi, pair_major):
    groups = []
    for _ in range(ng):
        g = dict(x=next(it))
        if rope_half:
            g["cos"], g["sin"] = next(it), next(it)
        if epi == "residual":
            g["res"] = next(it)
        groups.append(g)
    for gi, g in enumerate(groups):
        g["o"] = next(it)
        g["pm"] = pair_major[gi]
    return groups


def _mm_ws_kernel(*refs, starts, ntiles, npieces, epi, rope_half, scale, pair_major, w_t):
    it = iter(refs)
    wp_ref = next(it)
    groups = _unpack_groups(it, len(starts), rope_half, epi, pair_major)
    wb_ref = next(it)
    jj = pl.program_id(0)
    i = pl.program_id(1)
    nj = pl.num_programs(0) - 1

    @pl.when((jj < nj) & (i < npieces))
    def _():
        piece = wp_ref[0].astype(_BF16)
        rows = piece.shape[0]
        wb_ref[jj % 2, pl.ds(pl.multiple_of(i * rows, rows), rows), :] = piece

    for gi, g in enumerate(groups):
        active = (jj > 0) & (i >= starts[gi]) & (i < starts[gi] + ntiles[gi])

        @pl.when(active)
        def _(g=g):
            slot = (jj + 1) % 2
            tn = wb_ref.shape[1] if w_t else wb_ref.shape[2]
            sub = min(tn, SUB_TN)
            for c0 in range(0, tn, sub):
                if w_t:
                    z = lax.dot_general(g["x"][...], wb_ref[slot, c0:c0 + sub, :], _NT,
                                        preferred_element_type=_F32)
                else:
                    z = jnp.dot(g["x"][...], wb_ref[slot, :, c0:c0 + sub],
                                preferred_element_type=_F32)
                _epilogue(g, z, c0, epi=epi, rope_half=rope_half, scale=scale)


def _mm_kt_kernel(*refs, starts, ntiles):
    it = iter(refs)
    w_ref = next(it)
    groups = _unpack_groups(it, len(starts), 0, "residual", [False] * len(starts))
    i = pl.program_id(1)
    k = pl.program_id(2)
    for gi, g in enumerate(groups):
        active = (i >= starts[gi]) & (i < starts[gi] + ntiles[gi])

        @pl.when(active)
        def _(g=g):
            @pl.when(k == 0)
            def _():
                g["o"][...] = g["res"][...]

            g["o"][...] += jnp.dot(g["x"][...], w_ref[0].astype(_BF16),
                                   preferred_element_type=_F32)


def _matmul(xs, w, layer, *, col_start, n_cols, out_dtype, tms, epi="none", rope=None,
            scale=1.0, residuals=None, pair_major=None, tn=TN, tk=None, w_t=False):
    kdim = w.shape[2] if w_t else w.shape[1]
    tn = min(tn, n_cols)
    nj = n_cols // tn
    cb0 = col_start // tn
    assert n_cols % tn == 0
    assert col_start % (SUBLANES if w_t else tn) == 0
    ng = len(xs)
    pair_major = [False] * ng if pair_major is None else pair_major
    ntiles = [x.shape[0] // tm for x, tm in zip(xs, tms)]
    starts = [int(s) for s in np.concatenate([[0], np.cumsum(ntiles)[:-1]])]
    ni = sum(ntiles)
    ws = tk is None
    if ws:
        tk = kdim
        npieces = max(p for p in (8, 4, 2, 1) if p <= ni)
    else:
        assert epi == "residual" and out_dtype == _F32 and not w_t and kdim % tk == 0
        assert rope is None and not any(pair_major)

    def tile(gi):
        return lambda i: jnp.clip(i - starts[gi], 0, ntiles[gi] - 1)

    if ws:
        col = lambda a: jnp.maximum(a[0] - 1, 0)
        krow = lambda a: 0
        rowsel = lambda a, t: jnp.where(a[0] > 0, t(a[1]), 0)
        nxt = lambda a: jnp.minimum(a[0], nj - 1)
        pc = lambda a: jnp.minimum(a[1], npieces - 1)
        if w_t:
            pr = tn // npieces
            w_spec = pl.BlockSpec(
                (pl.Element(1), pl.Element(pr), pl.Element(tk)),
                lambda *a: (layer, pl.multiple_of(col_start + nxt(a) * tn + pc(a) * pr, SUBLANES),
                            0))
        else:
            w_spec = pl.BlockSpec((1, tk // npieces, tn), lambda *a: (layer, pc(a), cb0 + nxt(a)))
    else:
        col = lambda a: a[0]
        krow = lambda a: a[2]
        rowsel = lambda a, t: t(a[1])
        w_spec = pl.BlockSpec((1, tk, tn), lambda *a: (layer, a[2], cb0 + a[0]))

    in_specs, args = [w_spec], [w]
    rope_half = 0
    for gi in range(ng):
        tm, t = tms[gi], tile(gi)
        in_specs.append(pl.BlockSpec((tm, tk), lambda *a, t=t: (t(a[1]), krow(a))))
        args.append(xs[gi])
        if rope is not None:
            rope_half = rope[0]
            cos, sin = rope[1][gi]
            nper = cos.shape[0] // tm
            tspec = pl.BlockSpec((tm, LANES), lambda *a, t=t, nper=nper: (t(a[1]) % nper, 0))
            in_specs += [tspec, tspec]
            args += [cos, sin]
        if epi == "residual":
            mode = {} if ws else dict(pipeline_mode=pl.Buffered(1))
            in_specs.append(pl.BlockSpec((tm, tn), lambda *a, t=t: (t(a[1]), col(a)), **mode))
            args.append(residuals[gi])
    out_shapes, out_specs = [], []
    for gi in range(ng):
        tm, t, m = tms[gi], tile(gi), xs[gi].shape[0]
        if pair_major[gi]:
            out_shapes.append(jax.ShapeDtypeStruct((n_cols // LANES, m, LANES), out_dtype))
            out_specs.append(pl.BlockSpec((tn // LANES, tm, LANES),
                                          lambda *a, t=t: (col(a), rowsel(a, t), 0)))
        else:
            out_shapes.append(jax.ShapeDtypeStruct((m, n_cols), out_dtype))
            out_specs.append(pl.BlockSpec((tm, tn), lambda *a, t=t: (rowsel(a, t), col(a))))
    if ws:
        body = functools.partial(_mm_ws_kernel, starts=starts, ntiles=ntiles, npieces=npieces,
                                 epi=epi, rope_half=rope_half, scale=scale,
                                 pair_major=tuple(pair_major), w_t=w_t)
        grid = (nj + 1, ni)
        scratch = [pltpu.VMEM((2, tn, tk) if w_t else (2, tk, tn), _BF16)]
    else:
        body = functools.partial(_mm_kt_kernel, starts=starts, ntiles=ntiles)
        grid = (nj, ni, kdim // tk)
        scratch = []
    return pl.pallas_call(
        body,
        grid=grid,
        in_specs=in_specs,
        out_specs=out_specs,
        out_shape=out_shapes,
        scratch_shapes=scratch,
        compiler_params=_cparams(("arbitrary",) * len(grid)),
        name="matmul_" + epi + ("_rope%d" % rope_half if rope_half else ""),
    )(*args)


def _sortable(x):
    bits = pltpu.bitcast(x, jnp.int32)
    return jnp.where(bits < 0, bits ^ jnp.int32(0x7FFFFFFF), bits)


def _kth_largest(keys, n_sel, count_fn):
    cnt = count_fn((keys >= 0).astype(jnp.int32))
    thr0 = jnp.where(cnt >= n_sel, jnp.int32(0), jnp.int32(INT_MIN))

    def body(i, thr):
        cand = thr | jnp.left_shift(jnp.int32(1), 30 - i)
        cnt = count_fn((keys >= cand).astype(jnp.int32))
        return jnp.where(cnt >= n_sel, cand, thr)

    return lax.fori_loop(0, 31, body, thr0)


def _pattn_kernel(qi_ref, wit_ref, ki_ref, q_ref, k_ref, v_ref, o_ref, sc_ref, bias_ref, *,
                  n_sel, kv_rep):
    jq = pl.program_id(1)
    g = pl.program_id(2)
    s_len, tq = sc_ref.shape
    n_pairs = qi_ref.shape[0]

    def select(klen):
        kb = ki_ref[0:klen, :].astype(_BF16)
        zeros = jnp.zeros_like(kb)
        k_even = jnp.concatenate([kb, zeros], axis=1)
        k_odd = jnp.concatenate([zeros, kb], axis=1)
        w_scale = (IDX_HEADS ** -0.5) * (IDX_DIM ** -0.5)
        sc_ref[0:klen, :] = jnp.zeros((klen, tq), _F32)

        def body(p, carry):
            x = qi_ref[p]
            d0 = lax.dot_general(k_even, x, _NT, preferred_element_type=_F32)
            d1 = lax.dot_general(k_odd, x, _NT, preferred_element_type=_F32)
            w0 = wit_ref[pl.ds(2 * p, 1), :] * w_scale
            w1 = wit_ref[pl.ds(2 * p + 1, 1), :] * w_scale
            sc_ref[0:klen, :] += jnp.maximum(d0, 0.0) * w0 + jnp.maximum(d1, 0.0) * w1
            return carry

        lax.fori_loop(0, n_pairs, body, 0)

        key_pos = lax.broadcasted_iota(jnp.int32, (klen, tq), 0)
        q_pos = jq * tq + lax.broadcasted_iota(jnp.int32, (klen, tq), 1)
        adm = key_pos <= q_pos
        keys = _sortable(jnp.where(adm, sc_ref[0:klen, :], NEG))
        thr = _kth_largest(keys, n_sel, lambda c: jnp.sum(c, axis=0, keepdims=True))
        sel = (keys >= thr) & adm
        bias_ref[:, 0:klen] = jnp.where(sel, 0.0, NEG).astype(_F32).T

    def attend(klen):
        kg = k_ref[0:klen, :].astype(_BF16)
        vg = v_ref[0:klen, :].astype(_BF16)
        bias = bias_ref[:, 0:klen]
        for r in range(kv_rep):
            qh = q_ref[:, r * HEAD_DIM:(r + 1) * HEAD_DIM]
            s = lax.dot_general(qh, kg, _NT, preferred_element_type=_F32) + bias
            m = jnp.max(s, axis=1, keepdims=True)
            p = jnp.exp(s - m)
            l = jnp.sum(p, axis=1, keepdims=True)
            o = jnp.dot(p.astype(_BF16), vg, preferred_element_type=_F32) / l
            o_ref[:, r * HEAD_DIM:(r + 1) * HEAD_DIM] = o.astype(o_ref.dtype)

    for n in range(1, s_len // tq + 1):
        @pl.when(jq == n - 1)
        def _(klen=n * tq):
            @pl.when(g == 0)
            def _():
                select(klen)

            attend(klen)


def _prompt_attention(qi_pm, wit, ki, q, k, v, *, batch, seq):
    m, d = q.shape
    n_kv = k.shape[1] // HEAD_DIM
    kv_rep = (d // HEAD_DIM) // n_kv
    tq = min(seq, 256)
    nq = seq // tq
    n_sel = min(TOPK_MAX, seq // 4)
    n_pairs = qi_pm.shape[0]
    gw = kv_rep * HEAD_DIM
    return pl.pallas_call(
        functools.partial(_pattn_kernel, n_sel=n_sel, kv_rep=kv_rep),
        grid=(batch, nq, n_kv),
        in_specs=[
            pl.BlockSpec((n_pairs, tq, LANES), lambda b, j, g: (0, b * nq + j, 0)),
            pl.BlockSpec((wit.shape[0], tq), lambda b, j, g: (0, b * nq + j)),
            pl.BlockSpec((seq, ki.shape[1]), lambda b, j, g: (b, 0)),
            pl.BlockSpec((tq, gw), lambda b, j, g: (b * nq + j, g)),
            pl.BlockSpec((seq, HEAD_DIM), lambda b, j, g: (b, g)),
            pl.BlockSpec((seq, HEAD_DIM), lambda b, j, g: (b, g)),
        ],
        out_specs=pl.BlockSpec((tq, gw), lambda b, j, g: (b * nq + j, g)),
        out_shape=jax.ShapeDtypeStruct((m, d), _BF16),
        scratch_shapes=[pltpu.VMEM((seq, tq), _F32), pltpu.VMEM((tq, seq), _F32)],
        compiler_params=_cparams(("arbitrary", "arbitrary", "arbitrary")),
        name="prompt_attention",
    )(qi_pm, wit, ki, q, k, v)


def _sidx_kernel(pt_ref, qi_ref, w_ref, kin_ref, *rest, n_sel, t_real, n_chunks):
    pages = rest[:PAGES_PER_STEP]
    bias_ref, sc_ref = rest[PAGES_PER_STEP], rest[PAGES_PER_STEP + 1]
    c = pl.program_id(1)
    rows = SUBLANES
    chunk = PAGES_PER_STEP * PAGE_SIZE
    w = w_ref[0] * ((IDX_HEADS ** -0.5) * (IDX_DIM ** -0.5))
    qi = qi_ref[0]

    def scores(d):
        d = jnp.maximum(d, 0.0) * w
        acc = d[0:rows]
        for h in range(1, IDX_HEADS):
            acc = acc + d[h * rows:(h + 1) * rows]
        return acc

    @pl.when(c < n_chunks)
    def _():
        keys_t = jnp.concatenate([p[0] for p in pages], axis=1).astype(_BF16)
        sc_ref[c] = scores(jnp.dot(qi, keys_t, preferred_element_type=_F32))

    @pl.when(c == n_chunks)
    def _():
        kn = kin_ref[...].astype(_BF16)
        kn = jnp.concatenate([kn, jnp.zeros((PAGE_SIZE - rows, kn.shape[1]), _BF16)], axis=0)
        s_new = scores(lax.dot_general(qi, kn, _NT, preferred_element_type=_F32))
        s_new = jnp.concatenate([s_new, jnp.zeros((rows, chunk - PAGE_SIZE), _F32)], axis=1)
        shape = (n_chunks + 1, rows, chunk)
        ci = lax.broadcasted_iota(jnp.int32, shape, 0)
        qrow = lax.broadcasted_iota(jnp.int32, shape, 1)
        lane = lax.broadcasted_iota(jnp.int32, shape, 2)
        adm = (ci < n_chunks) | ((lane <= qrow) & (lane < t_real))
        sc_ref[n_chunks] = s_new
        keys = _sortable(jnp.where(adm, sc_ref[...], NEG))
        count = lambda x: jnp.sum(jnp.sum(x, axis=0, keepdims=True), axis=2, keepdims=True)
        thr = _kth_largest(keys, n_sel, count)
        sel = (keys >= thr) & adm
        bias_ref[0] = jnp.where(sel, 0.0, NEG).astype(_F32)


def _page_spec(block, n_chunks, page0, i):
    zeros = (0,) * (len(block) - 1)
    return pl.BlockSpec(
        block,
        lambda b, c, pt: (page0 + pt[b, jnp.minimum(c, n_chunks - 1) * PAGES_PER_STEP + i],)
        + zeros)


def _sample_index(page_table, qi_hm, w_hm, ki_new, pool_ki, page0, row0, *, t_real):
    bd, n_pages = page_table.shape
    n_chunks = n_pages // PAGES_PER_STEP
    chunk = PAGES_PER_STEP * PAGE_SIZE
    past = n_pages * PAGE_SIZE
    n_sel = min(TOPK_MAX, (past + t_real) // 4)
    hr = qi_hm.shape[1]
    grid_spec = pltpu.PrefetchScalarGridSpec(
        num_scalar_prefetch=1,
        grid=(bd, n_chunks + 1),
        in_specs=[pl.BlockSpec((1, hr, IDX_DIM), lambda b, c, pt: (b, 0, 0)),
                  pl.BlockSpec((1, hr, 1), lambda b, c, pt: (b, 0, 0)),
                  pl.BlockSpec((SUBLANES, IDX_DIM), lambda b, c, pt: (row0 + b, 0))]
                 + [_page_spec((1, IDX_DIM, PAGE_SIZE), n_chunks, page0, i)
                    for i in range(PAGES_PER_STEP)],
        out_specs=pl.BlockSpec((1, n_chunks + 1, SUBLANES, chunk), lambda b, c, pt: (b, 0, 0, 0)),
        scratch_shapes=[pltpu.VMEM((n_chunks + 1, SUBLANES, chunk), _F32)],
    )
    return pl.pallas_call(
        functools.partial(_sidx_kernel, n_sel=n_sel, t_real=t_real, n_chunks=n_chunks),
        grid_spec=grid_spec,
        out_shape=jax.ShapeDtypeStruct((bd, n_chunks + 1, SUBLANES, chunk), _F32),
        compiler_params=_cparams(("arbitrary", "arbitrary")),
        name="sample_index",
    )(page_table, qi_hm, w_hm, ki_new, *([pool_ki] * PAGES_PER_STEP))


def _sattn_kernel(pt_ref, q_ref, bias_ref, kn_ref, vn_ref, *rest, n_chunks, n_kv, kv_rep):
    kpages = rest[:PAGES_PER_STEP]
    vpages = rest[PAGES_PER_STEP:2 * PAGES_PER_STEP]
    o_ref, m_ref, l_ref, acc_ref = rest[2 * PAGES_PER_STEP:]
    c = pl.program_id(1)
    rows = SUBLANES

    @pl.when(c == 0)
    def _():
        m_ref[...] = jnp.full_like(m_ref, -jnp.inf)
        l_ref[...] = jnp.zeros_like(l_ref)
        acc_ref[...] = jnp.zeros_like(acc_ref)

    def update(g, kg, vg, bias):
        qg = jnp.concatenate(
            [q_ref[:, (g * kv_rep + r) * HEAD_DIM:(g * kv_rep + r + 1) * HEAD_DIM]
             for r in range(kv_rep)], axis=0).astype(_BF16)
        bg = jnp.concatenate([bias] * kv_rep, axis=0)
        s = lax.dot_general(qg, kg, _NT, preferred_element_type=_F32) + bg
        m_old = m_ref[g]
        m_new = jnp.maximum(m_old, jnp.max(s, axis=1, keepdims=True))
        alpha = jnp.exp(m_old - m_new)
        p = jnp.exp(s - m_new)
        l_ref[g] = alpha * l_ref[g] + jnp.sum(p, axis=1, keepdims=True)
        acc_ref[g] = alpha * acc_ref[g] + jnp.dot(p.astype(_BF16), vg,
                                                  preferred_element_type=_F32)
        m_ref[g] = m_new

    @pl.when(c < n_chunks)
    def _():
        bias = bias_ref[0, 0]
        for g in range(n_kv):
            rows_g = pl.ds(g, PAGE_SIZE, stride=n_kv)
            kg = jnp.concatenate([p[rows_g, :] for p in kpages], axis=0).astype(_BF16)
            vg = jnp.concatenate([p[rows_g, :] for p in vpages], axis=0).astype(_BF16)
            update(g, kg, vg, bias)

    @pl.when(c == n_chunks)
    def _():
        bias = bias_ref[0, 0][:, :PAGE_SIZE]
        pad = jnp.zeros((PAGE_SIZE - rows, HEAD_DIM), _BF16)
        for g in range(n_kv):
            sl = slice(g * HEAD_DIM, (g + 1) * HEAD_DIM)
            kg = jnp.concatenate([kn_ref[:, sl].astype(_BF16), pad], axis=0)
            vg = jnp.concatenate([vn_ref[:, sl].astype(_BF16), pad], axis=0)
            update(g, kg, vg, bias)
            o = acc_ref[g] / l_ref[g]
            for r in range(kv_rep):
                h = g * kv_rep + r
                o_ref[:, h * HEAD_DIM:(h + 1) * HEAD_DIM] = (
                    o[r * rows:(r + 1) * rows].astype(o_ref.dtype))


def _sample_attention(page_table, q, bias, k_new, v_new, pool_k, pool_v, page0):
    bd, n_pages = page_table.shape
    n_chunks = n_pages // PAGES_PER_STEP
    chunk = PAGES_PER_STEP * PAGE_SIZE
    m, d = q.shape
    kvw = k_new.shape[1]
    n_kv = kvw // HEAD_DIM
    kv_rep = (d // HEAD_DIM) // n_kv
    grid_spec = pltpu.PrefetchScalarGridSpec(
        num_scalar_prefetch=1,
        grid=(bd, n_chunks + 1),
        in_specs=[pl.BlockSpec((SUBLANES, d), lambda b, c, pt: (b, 0)),
                  pl.BlockSpec((1, 1, SUBLANES, chunk), lambda b, c, pt: (b, c, 0, 0)),
                  pl.BlockSpec((SUBLANES, kvw), lambda b, c, pt: (b, 0)),
                  pl.BlockSpec((SUBLANES, kvw), lambda b, c, pt: (b, 0))]
                 + [_page_spec((PAGE_SIZE * n_kv, HEAD_DIM), n_chunks, page0, i)
                    for i in range(PAGES_PER_STEP)] * 2,
        out_specs=pl.BlockSpec((SUBLANES, d), lambda b, c, pt: (b, 0)),
        scratch_shapes=[pltpu.VMEM((n_kv, kv_rep * SUBLANES, 1), _F32),
                        pltpu.VMEM((n_kv, kv_rep * SUBLANES, 1), _F32),
                        pltpu.VMEM((n_kv, kv_rep * SUBLANES, HEAD_DIM), _F32)],
    )
    return pl.pallas_call(
        functools.partial(_sattn_kernel, n_chunks=n_chunks, n_kv=n_kv, kv_rep=kv_rep),
        grid_spec=grid_spec,
        out_shape=jax.ShapeDtypeStruct((m, d), _F32),
        compiler_params=_cparams(("arbitrary", "arbitrary")),
        name="sample_attention",
    )(page_table, q, bias, k_new, v_new,
      *([pool_k] * PAGES_PER_STEP), *([pool_v] * PAGES_PER_STEP))


def _merge_kernel(cb_ref, cc_ref, ch_ref, ga_ref, gc_ref, attn_ref, ccp_ref, chp_ref, st_ref,
                  taps_ref, o_ref, tail_ref, u_ref):
    i = pl.program_id(2)
    tm = cb_ref.shape[0]
    u = cc_ref[...] * ch_ref[...]
    u_ref[SUBLANES:, :] = u

    @pl.when(i == 0)
    def _():
        u_ref[0:SUBLANES, :] = st_ref[0]

    @pl.when(i > 0)
    def _():
        u_ref[0:SUBLANES, :] = ccp_ref[...] * chp_ref[...]

    y = u * taps_ref[CONV_WIDTH - 1:CONV_WIDTH, :]
    for t in range(CONV_WIDTH - 1):
        off = SUBLANES - (CONV_WIDTH - 1) + t
        y = y + u_ref[off:off + tm, :] * taps_ref[t:t + 1, :]
    merged = (jax.nn.sigmoid(ga_ref[...]) * attn_ref[...].astype(_F32)
              + jax.nn.sigmoid(gc_ref[...]) * (cb_ref[...] * y))
    o_ref[...] = merged.astype(o_ref.dtype)
    tail_ref[0] = u[tm - SUBLANES:, :]


def _merge(z, attn, state8, taps, *, batch, seq, out_dtype):
    m, c = attn.shape
    tm = min(seq, 256)
    nt = seq // tm
    tc = min(c, MERGE_TC)
    nseg = c // tc
    bpt = tm // SUBLANES
    row = lambda b, i: b * nt + i
    seg = lambda s: pl.BlockSpec((tm, tc), lambda b, j, i: (row(b, i), j + s * nseg))
    prev = lambda s: pl.BlockSpec(
        (SUBLANES, tc), lambda b, j, i: (jnp.maximum(row(b, i) * bpt - 1, 0), j + s * nseg))
    return pl.pallas_call(
        _merge_kernel,
        grid=(batch, nseg, nt),
        in_specs=[seg(0), seg(1), seg(2), seg(3), seg(4),
                  pl.BlockSpec((tm, tc), lambda b, j, i: (row(b, i), j)),
                  prev(1), prev(2),
                  pl.BlockSpec((1, SUBLANES, tc), lambda b, j, i: (b, 0, j)),
                  pl.BlockSpec((CONV_WIDTH, tc), lambda b, j, i: (0, j))],
        out_specs=[pl.BlockSpec((tm, tc), lambda b, j, i: (row(b, i), j)),
                   pl.BlockSpec((1, SUBLANES, tc), lambda b, j, i: (b, 0, j))],
        out_shape=[jax.ShapeDtypeStruct((m, c), out_dtype),
                   jax.ShapeDtypeStruct((batch, SUBLANES, c), _F32)],
        scratch_shapes=[pltpu.VMEM((tm + SUBLANES, tc), _F32)],
        compiler_params=_cparams(("arbitrary", "arbitrary", "arbitrary")),
        name="conv_merge",
    )(z, z, z, z, z, attn, z, z, state8, taps)


def _rope_tables(positions, half, n_rot_lanes=LANES):
    pos = np.asarray(positions, np.float64)[:, None]
    lane = np.arange(LANES)
    inv = ROPE_THETA ** (-(lane % half).astype(np.float64) / half)
    ang = pos * inv[None, :]
    sign = np.where((lane % (2 * half)) < half, -1.0, 1.0)
    rot = (lane < n_rot_lanes)[None, :]
    cos = np.where(rot, np.cos(ang), 1.0)
    sin = np.where(rot, np.sin(ang) * sign[None, :], 0.0)
    return jnp.asarray(cos, _F32), jnp.asarray(sin, _F32)


def kernel(x_prompt, x_sample, cache_k, cache_v, cache_kidx, state_conv, page_table,
           w_in, conv_w, w_o, g_mix, g_mlp, w_up, w_down, g_final):
    B, S, D = x_prompt.shape
    Bd, T, _ = x_sample.shape
    depth = w_in.shape[0]
    n_pool = cache_k.shape[1]
    past = page_table.shape[1] * PAGE_SIZE
    R = SUBLANES
    kvw = N_KV_HEADS * HEAD_DIM
    half, ihalf = HEAD_DIM // 2, IDX_DIM // 2
    Mp, Ms = B * S, Bd * R
    tms = [min(Mp, 1024), Ms]
    tms_down = [min(Mp, 2048), Ms]

    c_q, c_k = 0, N_HEADS * HEAD_DIM
    c_v = c_k + kvw
    c_qi = c_v + kvw
    c_kiwi = c_qi + IDX_HEADS * IDX_DIM
    c_gate = c_kiwi + IDX_DIM + IDX_HEADS

    pos_p = np.arange(max(S, tms[0])) % S
    pos_s = np.tile(past + np.arange(R), Bd)
    r128 = [_rope_tables(pos_p, half), _rope_tables(pos_s, half)]
    r64 = [_rope_tables(pos_p, ihalf), _rope_tables(pos_s, ihalf)]
    rkiwi = [_rope_tables(pos_p, ihalf, IDX_DIM), _rope_tables(pos_s, ihalf, IDX_DIM)]

    xp = x_prompt.reshape(Mp, D)
    xs = jnp.pad(x_sample, ((0, 0), (0, R - T), (0, 0))).reshape(Ms, D)
    w_in_t = jnp.swapaxes(w_in, 1, 2)
    pool_k = cache_k.reshape(depth * n_pool * PAGE_SIZE * N_KV_HEADS, HEAD_DIM)
    pool_v = cache_v.reshape(depth * n_pool * PAGE_SIZE * N_KV_HEADS, HEAD_DIM)
    pool_ki_t = jnp.swapaxes(cache_kidx, 2, 3).reshape(depth * n_pool, IDX_DIM, PAGE_SIZE)
    zero_state = jnp.zeros((B, R, D), _F32)

    outs = {n: [] for n in ("kp", "vp", "kip", "cp", "ks", "vs", "kis", "cs")}
    for l in range(depth):
        hs = [_rmsnorm(xp, g_mix[l], _BF16), _rmsnorm(xs, g_mix[l], _BF16)]
        mm = functools.partial(_matmul, hs, w_in_t, l, tms=tms, w_t=True)
        qp, qs = mm(col_start=c_q, n_cols=N_HEADS * HEAD_DIM, out_dtype=_BF16,
                    rope=(half, r128), scale=HEAD_DIM ** -0.5)
        kp, ks = mm(col_start=c_k, n_cols=kvw, out_dtype=_F32, rope=(half, r128))
        vp, vs = mm(col_start=c_v, n_cols=kvw, out_dtype=_F32)
        qip, qis = mm(col_start=c_qi, n_cols=IDX_HEADS * IDX_DIM, out_dtype=_BF16,
                      rope=(ihalf, r64), pair_major=[True, False])
        kwp, kws = mm(col_start=c_kiwi, n_cols=LANES, out_dtype=_F32, rope=(ihalf, rkiwi))
        zp, zs = mm(col_start=c_gate, n_cols=5 * D, out_dtype=_F32)
        page0 = l * n_pool

        kip = kwp[:, :IDX_DIM]
        wit = kwp[:, IDX_DIM:IDX_DIM + IDX_HEADS].T
        attn_p = _prompt_attention(qip, wit, kip, qp, kp, vp, batch=B, seq=S)

        qi_hm = qis.reshape(Bd, R, IDX_HEADS, IDX_DIM).transpose(0, 2, 1, 3).reshape(
            Bd, IDX_HEADS * R, IDX_DIM)
        kis = kws[:, :IDX_DIM]
        w_hm = kws[:, IDX_DIM:IDX_DIM + IDX_HEADS].reshape(Bd, R, IDX_HEADS).transpose(
            0, 2, 1).reshape(Bd, IDX_HEADS * R, 1)
        bias = _sample_index(page_table, qi_hm, w_hm, kis, pool_ki_t, page0, 0, t_real=T)
        attn_s = _sample_attention(page_table, qs.astype(_F32), bias, ks, vs, pool_k, pool_v,
                                   page0)

        state8 = jnp.pad(state_conv[l], ((0, 0), (R - (CONV_WIDTH - 1), 0), (0, 0)))
        mg_p, tail_p = _merge(zp, attn_p, zero_state, conv_w[l], batch=B, seq=S, out_dtype=_BF16)
        mg_s, tail_s = _merge(zs, attn_s, state8, conv_w[l], batch=Bd, seq=R, out_dtype=_F32)
        xp, xs = _matmul([mg_p, mg_s.astype(_BF16)], w_o, l, col_start=0, n_cols=D,
                         out_dtype=_F32, tms=tms, epi="residual", residuals=[xp, xs])
        h2 = [_rmsnorm(xp, g_mlp[l], _BF16), _rmsnorm(xs, g_mlp[l], _BF16)]
        hid = _matmul(h2, w_up, l, col_start=0, n_cols=w_up.shape[2], out_dtype=_BF16, tms=tms,
                      epi="relu2")
        xp, xs = _matmul(hid, w_down, l, col_start=0, n_cols=D, out_dtype=_F32, tms=tms_down,
                         epi="residual", residuals=[xp, xs], tn=min(D, 1024),
                         tk=min(w_down.shape[1], 1024))

        outs["kp"].append(kp.reshape(B, S, N_KV_HEADS, HEAD_DIM))
        outs["vp"].append(vp.reshape(B, S, N_KV_HEADS, HEAD_DIM))
        outs["kip"].append(kip.reshape(B, S, IDX_DIM))
        outs["cp"].append(tail_p[:, R - (CONV_WIDTH - 1):, :])
        outs["ks"].append(ks.reshape(Bd, R, N_KV_HEADS, HEAD_DIM)[:, :T])
        outs["vs"].append(vs.reshape(Bd, R, N_KV_HEADS, HEAD_DIM)[:, :T])
        outs["kis"].append(kis.reshape(Bd, R, IDX_DIM)[:, :T])
        outs["cs"].append(tail_s[:, T - (CONV_WIDTH - 1):T, :])

    y_prompt = _rmsnorm(xp, g_final, _F32).reshape(B, S, D)
    y_sample = _rmsnorm(xs, g_final, _F32).reshape(Bd, R, D)[:, :T]
    st = lambda n: jnp.stack(outs[n])
    return (y_prompt, y_sample, st("kp"), st("vp"), st("kip"), st("cp"),
            st("ks"), st("vs"), st("kis"), st("cs"))
```

```python
import functools

import numpy as np
import jax
import jax.numpy as jnp
from jax import lax
from jax.experimental import pallas as pl
from jax.experimental.pallas import tpu as pltpu

HEAD_DIM = 128
N_HEADS = 32
N_KV_HEADS = 8
IDX_HEADS = 32
IDX_DIM = 64
TOPK_MAX = 256
CONV_WIDTH = 3
PAGE_SIZE = 128
ROPE_THETA = 10000.0
NORM_EPS = 1e-6
NEG = -1e30

LANES = 128
SUBLANES = 8
VMEM_LIMIT = 60 * 1024 * 1024
PAGES_PER_STEP = 8
INT_MIN = -2 ** 31
TN = 1024
SUB_TN = 512
GATE_CW = 256

_F32 = jnp.float32
_BF16 = jnp.bfloat16
_NT = (((1,), (1,)), ((), ()))


def _cparams(sem):
    return pltpu.CompilerParams(dimension_semantics=sem, vmem_limit_bytes=VMEM_LIMIT)


def _rmsnorm_kernel(x_ref, g_ref, o_ref):
    x = x_ref[...]
    y = x * lax.rsqrt(jnp.mean(x * x, axis=-1, keepdims=True) + NORM_EPS)
    o_ref[...] = (y * g_ref[...]).astype(o_ref.dtype)


def _rmsnorm(x, g, out_dtype):
    m, d = x.shape
    tm = min(m, 256)
    return pl.pallas_call(
        _rmsnorm_kernel,
        grid=(m // tm,),
        in_specs=[pl.BlockSpec((tm, d), lambda i: (i, 0)),
                  pl.BlockSpec((1, d), lambda i: (0, 0))],
        out_specs=pl.BlockSpec((tm, d), lambda i: (i, 0)),
        out_shape=jax.ShapeDtypeStruct((m, d), out_dtype),
        compiler_params=_cparams(("arbitrary",)),
        name="rmsnorm",
    )(x, g.reshape(1, d))


def _rope_slab(z, cos, sin, half):
    if 2 * half == LANES:
        partner = pltpu.roll(z, half, 1)
    else:
        lane = lax.broadcasted_iota(jnp.int32, z.shape, 1)
        first = (lane % (2 * half)) < half
        partner = jnp.where(first, pltpu.roll(z, LANES - half, 1), pltpu.roll(z, half, 1))
    return z * cos + partner * sin


def _epilogue(g, z, c0, *, epi, rope_half, scale):
    o_ref = g["o"]
    width = z.shape[1]
    if rope_half:
        cos, sin = g["cos"][...], g["sin"][...]
        for s in range(width // LANES):
            slab = _rope_slab(z[:, s * LANES:(s + 1) * LANES], cos, sin, rope_half)
            if scale != 1.0:
                slab = slab * scale
            if g["pm"]:
                o_ref[c0 // LANES + s] = slab.astype(o_ref.dtype)
            else:
                o_ref[:, c0 + s * LANES:c0 + (s + 1) * LANES] = slab.astype(o_ref.dtype)
        return
    if epi == "relu2":
        r = jnp.maximum(z, 0.0)
        z = r * r
    elif epi == "residual":
        z = g["res"][:, c0:c0 + width] + z
    o_ref[:, c0:c0 + width] = z.astype(o_ref.dtype)


def _unpack_groups(it, ng, rope_half, epi, pair_major):
    groups = []
    for _ in range(ng):
        g = dict(x=next(it))
        if rope_half:
            g["cos"], g["sin"] = next(it), next(it)
        if epi == "residual":
            g["res"] = next(it)
        groups.append(g)
    for gi, g in enumerate(groups):
        g["o"] = next(it)
        g["pm"] = pair_major[gi]
    return groups


def _mm_ws_kernel(*refs, starts, ntiles, npieces, epi, rope_half, scale, pair_major, w_t):
    it = iter(refs)
    wp_ref = next(it)
    groups = _unpack_groups(it, len(starts), rope_half, epi, pair_major)
    wb_ref = next(it)
    jj = pl.program_id(0)
    i = pl.program_id(1)
    nj = pl.num_programs(0) - 1

    @pl.when((jj < nj) & (i < npieces))
    def _():
        piece = wp_ref[0].astype(_BF16)
        rows = piece.shape[0]
        wb_ref[jj % 2, pl.ds(pl.multiple_of(i * rows, rows), rows), :] = piece

    for gi, g in enumerate(groups):
        active = (jj > 0) & (i >= starts[gi]) & (i < starts[gi] + ntiles[gi])

        @pl.when(active)
        def _(g=g):
            slot = (jj + 1) % 2
            tn = wb_ref.shape[1] if w_t else wb_ref.shape[2]
            sub = min(tn, SUB_TN)
            for c0 in range(0, tn, sub):
                if w_t:
                    z = lax.dot_general(g["x"][...], wb_ref[slot, c0:c0 + sub, :], _NT,
                                        preferred_element_type=_F32)
                else:
                    z = jnp.dot(g["x"][...], wb_ref[slot, :, c0:c0 + sub],
                                preferred_element_type=_F32)
                _epilogue(g, z, c0, epi=epi, rope_half=rope_half, scale=scale)


def _mm_kt_kernel(*refs, starts, ntiles):
    it = iter(refs)
    w_ref = next(it)
    groups = _unpack_groups(it, len(starts), 0, "residual", [False] * len(starts))
    i = pl.program_id(1)
    k = pl.program_id(2)
    for gi, g in enumerate(groups):
        active = (i >= starts[gi]) & (i < starts[gi] + ntiles[gi])

        @pl.when(active)
        def _(g=g):
            @pl.when(k == 0)
            def _():
                g["o"][...] = g["res"][...]

            g["o"][...] += jnp.dot(g["x"][...], w_ref[0].astype(_BF16),
                                   preferred_element_type=_F32)


def _matmul(xs, w, layer, *, col_start, n_cols, out_dtype, tms, epi="none", rope=None,
            scale=1.0, residuals=None, pair_major=None, tn=TN, tk=None, w_t=False):
    kdim = w.shape[2] if w_t else w.shape[1]
    tn = min(tn, n_cols)
    nj = n_cols // tn
    cb0 = col_start // tn
    assert n_cols % tn == 0
    assert col_start % (SUBLANES if w_t else tn) == 0
    ng = len(xs)
    pair_major = [False] * ng if pair_major is None else pair_major
    ntiles = [x.shape[0] // tm for x, tm in zip(xs, tms)]
    starts = [int(s) for s in np.concatenate([[0], np.cumsum(ntiles)[:-1]])]
    ni = sum(ntiles)
    ws = tk is None
    if ws:
        tk = kdim
        npieces = max(p for p in (8, 4, 2, 1) if p <= ni)
    else:
        assert epi == "residual" and out_dtype == _F32 and not w_t and kdim % tk == 0
        assert rope is None and not any(pair_major)

    def tile(gi):
        return lambda i: jnp.clip(i - starts[gi], 0, ntiles[gi] - 1)

    if ws:
        col = lambda a: jnp.maximum(a[0] - 1, 0)
        krow = lambda a: 0
        rowsel = lambda a, t: jnp.where(a[0] > 0, t(a[1]), 0)
        nxt = lambda a: jnp.minimum(a[0], nj - 1)
        pc = lambda a: jnp.minimum(a[1], npieces - 1)
        if w_t:
            pr = tn // npieces
            w_spec = pl.BlockSpec(
                (pl.Element(1), pl.Element(pr), pl.Element(tk)),
                lambda *a: (layer, pl.multiple_of(col_start + nxt(a) * tn + pc(a) * pr, SUBLANES),
                            0))
        else:
            w_spec = pl.BlockSpec((1, tk // npieces, tn), lambda *a: (layer, pc(a), cb0 + nxt(a)))
    else:
        col = lambda a: a[0]
        krow = lambda a: a[2]
        rowsel = lambda a, t: t(a[1])
        w_spec = pl.BlockSpec((1, tk, tn), lambda *a: (layer, a[2], cb0 + a[0]))

    in_specs, args = [w_spec], [w]
    rope_half = 0
    for gi in range(ng):
        tm, t = tms[gi], tile(gi)
        in_specs.append(pl.BlockSpec((tm, tk), lambda *a, t=t: (t(a[1]), krow(a))))
        args.append(xs[gi])
        if rope is not None:
            rope_half = rope[0]
            cos, sin = rope[1][gi]
            nper = cos.shape[0] // tm
            tspec = pl.BlockSpec((tm, LANES), lambda *a, t=t, nper=nper: (t(a[1]) % nper, 0))
            in_specs += [tspec, tspec]
            args += [cos, sin]
        if epi == "residual":
            mode = {} if ws else dict(pipeline_mode=pl.Buffered(1))
            in_specs.append(pl.BlockSpec((tm, tn), lambda *a, t=t: (t(a[1]), col(a)), **mode))
            args.append(residuals[gi])
    out_shapes, out_specs = [], []
    for gi in range(ng):
        tm, t, m = tms[gi], tile(gi), xs[gi].shape[0]
        if pair_major[gi]:
            out_shapes.append(jax.ShapeDtypeStruct((n_cols // LANES, m, LANES), out_dtype))
            out_specs.append(pl.BlockSpec((tn // LANES, tm, LANES),
                                          lambda *a, t=t: (col(a), rowsel(a, t), 0)))
        else:
            out_shapes.append(jax.ShapeDtypeStruct((m, n_cols), out_dtype))
            out_specs.append(pl.BlockSpec((tm, tn), lambda *a, t=t: (rowsel(a, t), col(a))))
    if ws:
        body = functools.partial(_mm_ws_kernel, starts=starts, ntiles=ntiles, npieces=npieces,
                                 epi=epi, rope_half=rope_half, scale=scale,
                                 pair_major=tuple(pair_major), w_t=w_t)
        grid = (nj + 1, ni)
        scratch = [pltpu.VMEM((2, tn, tk) if w_t else (2, tk, tn), _BF16)]
    else:
        body = functools.partial(_mm_kt_kernel, starts=starts, ntiles=ntiles)
        grid = (nj, ni, kdim // tk)
        scratch = []
    return pl.pallas_call(
        body,
        grid=grid,
        in_specs=in_specs,
        out_specs=out_specs,
        out_shape=out_shapes,
        scratch_shapes=scratch,
        compiler_params=_cparams(("arbitrary",) * len(grid)),
        name="matmul_" + epi + ("_rope%d" % rope_half if rope_half else ""),
    )(*args)


def _sortable(x):
    bits = pltpu.bitcast(x, jnp.int32)
    return jnp.where(bits < 0, bits ^ jnp.int32(0x7FFFFFFF), bits)


def _kth_largest(keys, n_sel, count_fn):
    cnt = count_fn((keys >= 0).astype(jnp.int32))
    thr0 = jnp.where(cnt >= n_sel, jnp.int32(0), jnp.int32(INT_MIN))

    def body(i, thr):
        cand = thr | jnp.left_shift(jnp.int32(1), 30 - i)
        cnt = count_fn((keys >= cand).astype(jnp.int32))
        return jnp.where(cnt >= n_sel, cand, thr)

    return lax.fori_loop(0, 31, body, thr0)


def _select_topk(keys, idx, n_sel, count_fn, idx_bits):
    thr = _kth_largest(keys, n_sel, count_fn)
    gt = keys > thr
    eq = keys == thr
    need = n_sel - count_fn(gt.astype(jnp.int32))
    surplus = count_fn(eq.astype(jnp.int32)) - need

    def tie_cut():
        def body(b, lo):
            cand = lo + jnp.left_shift(jnp.int32(1), idx_bits - 1 - b)
            below = count_fn((eq & (idx < cand)).astype(jnp.int32))
            return jnp.where(below < need, cand, lo)

        return lax.fori_loop(0, idx_bits, body, jnp.zeros_like(thr))

    cut = lax.cond(jnp.max(surplus) > 0, tie_cut,
                   lambda: jnp.full_like(thr, jnp.iinfo(jnp.int32).max))
    return gt | (eq & (idx <= cut))


def _pattn_kernel(qi_ref, wit_ref, ki_ref, q_ref, k_ref, v_ref, o_ref, sc_ref, bias_ref, *,
                  n_sel, kv_rep):
    jq = pl.program_id(1)
    g = pl.program_id(2)
    s_len, tq = sc_ref.shape
    n_pairs = qi_ref.shape[0]

    def select(klen):
        kb = ki_ref[0:klen, :].astype(_BF16)
        zeros = jnp.zeros_like(kb)
        k_even = jnp.concatenate([kb, zeros], axis=1)
        k_odd = jnp.concatenate([zeros, kb], axis=1)
        w_scale = (IDX_HEADS ** -0.5) * (IDX_DIM ** -0.5)
        sc_ref[0:klen, :] = jnp.zeros((klen, tq), _F32)

        def body(p, carry):
            x = qi_ref[p]
            d0 = lax.dot_general(k_even, x, _NT, preferred_element_type=_F32)
            d1 = lax.dot_general(k_odd, x, _NT, preferred_element_type=_F32)
            w0 = wit_ref[pl.ds(2 * p, 1), :] * w_scale
            w1 = wit_ref[pl.ds(2 * p + 1, 1), :] * w_scale
            sc_ref[0:klen, :] += jnp.maximum(d0, 0.0) * w0 + jnp.maximum(d1, 0.0) * w1
            return carry

        lax.fori_loop(0, n_pairs, body, 0)

        key_pos = lax.broadcasted_iota(jnp.int32, (klen, tq), 0)
        q_pos = jq * tq + lax.broadcasted_iota(jnp.int32, (klen, tq), 1)
        adm = key_pos <= q_pos
        keys = _sortable(jnp.where(adm, sc_ref[0:klen, :], NEG))
        sel = _select_topk(keys, key_pos, n_sel, lambda c: jnp.sum(c, axis=0, keepdims=True),
                           (s_len - 1).bit_length()) & adm
        bias_ref[:, 0:klen] = jnp.where(sel, 0.0, NEG).astype(_F32).T

    def attend(klen):
        kg = k_ref[0:klen, :].astype(_BF16)
        vg = v_ref[0:klen, :].astype(_BF16)
        bias = bias_ref[:, 0:klen]
        for r in range(kv_rep):
            qh = q_ref[:, r * HEAD_DIM:(r + 1) * HEAD_DIM]
            s = lax.dot_general(qh, kg, _NT, preferred_element_type=_F32) + bias
            m = jnp.max(s, axis=1, keepdims=True)
            p = jnp.exp(s - m)
            l = jnp.sum(p, axis=1, keepdims=True)
            o = jnp.dot(p.astype(_BF16), vg, preferred_element_type=_F32) / l
            o_ref[:, r * HEAD_DIM:(r + 1) * HEAD_DIM] = o.astype(o_ref.dtype)

    for n in range(1, s_len // tq + 1):
        @pl.when(jq == n - 1)
        def _(klen=n * tq):
            @pl.when(g == 0)
            def _():
                select(klen)

            attend(klen)


def _prompt_attention(qi_pm, wit, ki, q, k, v, *, batch, seq):
    m, d = q.shape
    n_kv = k.shape[1] // HEAD_DIM
    kv_rep = (d // HEAD_DIM) // n_kv
    tq = min(seq, 256)
    nq = seq // tq
    n_sel = min(TOPK_MAX, seq // 4)
    n_pairs = qi_pm.shape[0]
    gw = kv_rep * HEAD_DIM
    return pl.pallas_call(
        functools.partial(_pattn_kernel, n_sel=n_sel, kv_rep=kv_rep),
        grid=(batch, nq, n_kv),
        in_specs=[
            pl.BlockSpec((n_pairs, tq, LANES), lambda b, j, g: (0, b * nq + j, 0)),
            pl.BlockSpec((wit.shape[0], tq), lambda b, j, g: (0, b * nq + j)),
            pl.BlockSpec((seq, ki.shape[1]), lambda b, j, g: (b, 0)),
            pl.BlockSpec((tq, gw), lambda b, j, g: (b * nq + j, g)),
            pl.BlockSpec((seq, HEAD_DIM), lambda b, j, g: (b, g)),
            pl.BlockSpec((seq, HEAD_DIM), lambda b, j, g: (b, g)),
        ],
        out_specs=pl.BlockSpec((tq, gw), lambda b, j, g: (b * nq + j, g)),
        out_shape=jax.ShapeDtypeStruct((m, d), _BF16),
        scratch_shapes=[pltpu.VMEM((seq, tq), _F32), pltpu.VMEM((tq, seq), _F32)],
        compiler_params=_cparams(("arbitrary", "arbitrary", "arbitrary")),
        name="prompt_attention",
    )(qi_pm, wit, ki, q, k, v)


def _sidx_kernel(pt_ref, qi_ref, w_ref, kin_ref, *rest, n_sel, t_real, n_chunks):
    pages = rest[:PAGES_PER_STEP]
    bias_ref, sc_ref = rest[PAGES_PER_STEP], rest[PAGES_PER_STEP + 1]
    c = pl.program_id(1)
    rows = SUBLANES
    chunk = PAGES_PER_STEP * PAGE_SIZE
    w = w_ref[0] * ((IDX_HEADS ** -0.5) * (IDX_DIM ** -0.5))
    qi = qi_ref[0]

    def scores(d):
        d = jnp.maximum(d, 0.0) * w
        acc = d[0:rows]
        for h in range(1, IDX_HEADS):
            acc = acc + d[h * rows:(h + 1) * rows]
        return acc

    @pl.when(c < n_chunks)
    def _():
        keys_t = jnp.concatenate([p[0] for p in pages], axis=1).astype(_BF16)
        sc_ref[c] = scores(jnp.dot(qi, keys_t, preferred_element_type=_F32))

    @pl.when(c == n_chunks)
    def _():
        kn = kin_ref[...].astype(_BF16)
        kn = jnp.concatenate([kn, jnp.zeros((PAGE_SIZE - rows, kn.shape[1]), _BF16)], axis=0)
        s_new = scores(lax.dot_general(qi, kn, _NT, preferred_element_type=_F32))
        s_new = jnp.concatenate([s_new, jnp.zeros((rows, chunk - PAGE_SIZE), _F32)], axis=1)
        shape = (n_chunks + 1, rows, chunk)
        ci = lax.broadcasted_iota(jnp.int32, shape, 0)
        qrow = lax.broadcasted_iota(jnp.int32, shape, 1)
        lane = lax.broadcasted_iota(jnp.int32, shape, 2)
        adm = (ci < n_chunks) | ((lane <= qrow) & (lane < t_real))
        sc_ref[n_chunks] = s_new
        keys = _sortable(jnp.where(adm, sc_ref[...], NEG))
        count = lambda x: jnp.sum(jnp.sum(x, axis=0, keepdims=True), axis=2, keepdims=True)
        key_idx = ci * chunk + lane
        n_keys = (n_chunks + 1) * chunk
        sel = _select_topk(keys, key_idx, n_sel, count, (n_keys - 1).bit_length()) & adm
        bias_ref[0] = jnp.where(sel, 0.0, NEG).astype(_F32)


def _page_spec(block, n_chunks, page0, i):
    zeros = (0,) * (len(block) - 1)
    return pl.BlockSpec(
        block,
        lambda b, c, pt: (page0 + pt[b, jnp.minimum(c, n_chunks - 1) * PAGES_PER_STEP + i],)
        + zeros)


def _sample_index(page_table, qi_hm, w_hm, ki_new, pool_ki, page0, row0, *, t_real):
    bd, n_pages = page_table.shape
    n_chunks = n_pages // PAGES_PER_STEP
    chunk = PAGES_PER_STEP * PAGE_SIZE
    past = n_pages * PAGE_SIZE
    n_sel = min(TOPK_MAX, (past + t_real) // 4)
    hr = qi_hm.shape[1]
    grid_spec = pltpu.PrefetchScalarGridSpec(
        num_scalar_prefetch=1,
        grid=(bd, n_chunks + 1),
        in_specs=[pl.BlockSpec((1, hr, IDX_DIM), lambda b, c, pt: (b, 0, 0)),
                  pl.BlockSpec((1, hr, 1), lambda b, c, pt: (b, 0, 0)),
                  pl.BlockSpec((SUBLANES, IDX_DIM), lambda b, c, pt: (row0 + b, 0))]
                 + [_page_spec((1, IDX_DIM, PAGE_SIZE), n_chunks, page0, i)
                    for i in range(PAGES_PER_STEP)],
        out_specs=pl.BlockSpec((1, n_chunks + 1, SUBLANES, chunk), lambda b, c, pt: (b, 0, 0, 0)),
        scratch_shapes=[pltpu.VMEM((n_chunks + 1, SUBLANES, chunk), _F32)],
    )
    return pl.pallas_call(
        functools.partial(_sidx_kernel, n_sel=n_sel, t_real=t_real, n_chunks=n_chunks),
        grid_spec=grid_spec,
        out_shape=jax.ShapeDtypeStruct((bd, n_chunks + 1, SUBLANES, chunk), _F32),
        compiler_params=_cparams(("arbitrary", "arbitrary")),
        name="sample_index",
    )(page_table, qi_hm, w_hm, ki_new, *([pool_ki] * PAGES_PER_STEP))


def _sattn_kernel(pt_ref, q_ref, bias_ref, kn_ref, vn_ref, *rest, n_chunks, n_kv, kv_rep):
    kpages = rest[:PAGES_PER_STEP]
    vpages = rest[PAGES_PER_STEP:2 * PAGES_PER_STEP]
    o_ref, m_ref, l_ref, acc_ref = rest[2 * PAGES_PER_STEP:]
    c = pl.program_id(1)
    rows = SUBLANES

    @pl.when(c == 0)
    def _():
        m_ref[...] = jnp.full_like(m_ref, -jnp.inf)
        l_ref[...] = jnp.zeros_like(l_ref)
        acc_ref[...] = jnp.zeros_like(acc_ref)

    def update(g, kg, vg, bias):
        qg = jnp.concatenate(
            [q_ref[:, (g * kv_rep + r) * HEAD_DIM:(g * kv_rep + r + 1) * HEAD_DIM]
             for r in range(kv_rep)], axis=0).astype(_BF16)
        bg = jnp.concatenate([bias] * kv_rep, axis=0)
        s = lax.dot_general(qg, kg, _NT, preferred_element_type=_F32) + bg
        m_old = m_ref[g]
        m_new = jnp.maximum(m_old, jnp.max(s, axis=1, keepdims=True))
        alpha = jnp.exp(m_old - m_new)
        p = jnp.exp(s - m_new)
        l_ref[g] = alpha * l_ref[g] + jnp.sum(p, axis=1, keepdims=True)
        acc_ref[g] = alpha * acc_ref[g] + jnp.dot(p.astype(_BF16), vg,
                                                  preferred_element_type=_F32)
        m_ref[g] = m_new

    @pl.when(c < n_chunks)
    def _():
        bias = bias_ref[0, 0]
        for g in range(n_kv):
            rows_g = pl.ds(g, PAGE_SIZE, stride=n_kv)
            kg = jnp.concatenate([p[rows_g, :] for p in kpages], axis=0).astype(_BF16)
            vg = jnp.concatenate([p[rows_g, :] for p in vpages], axis=0).astype(_BF16)
            update(g, kg, vg, bias)

    @pl.when(c == n_chunks)
    def _():
        bias = bias_ref[0, 0][:, :PAGE_SIZE]
        pad = jnp.zeros((PAGE_SIZE - rows, HEAD_DIM), _BF16)
        for g in range(n_kv):
            sl = slice(g * HEAD_DIM, (g + 1) * HEAD_DIM)
            kg = jnp.concatenate([kn_ref[:, sl].astype(_BF16), pad], axis=0)
            vg = jnp.concatenate([vn_ref[:, sl].astype(_BF16), pad], axis=0)
            update(g, kg, vg, bias)
            o = acc_ref[g] / l_ref[g]
            for r in range(kv_rep):
                h = g * kv_rep + r
                o_ref[:, h * HEAD_DIM:(h + 1) * HEAD_DIM] = (
                    o[r * rows:(r + 1) * rows].astype(o_ref.dtype))


def _sample_attention(page_table, q, bias, k_new, v_new, pool_k, pool_v, page0):
    bd, n_pages = page_table.shape
    n_chunks = n_pages // PAGES_PER_STEP
    chunk = PAGES_PER_STEP * PAGE_SIZE
    m, d = q.shape
    kvw = k_new.shape[1]
    n_kv = kvw // HEAD_DIM
    kv_rep = (d // HEAD_DIM) // n_kv
    grid_spec = pltpu.PrefetchScalarGridSpec(
        num_scalar_prefetch=1,
        grid=(bd, n_chunks + 1),
        in_specs=[pl.BlockSpec((SUBLANES, d), lambda b, c, pt: (b, 0)),
                  pl.BlockSpec((1, 1, SUBLANES, chunk), lambda b, c, pt: (b, c, 0, 0)),
                  pl.BlockSpec((SUBLANES, kvw), lambda b, c, pt: (b, 0)),
                  pl.BlockSpec((SUBLANES, kvw), lambda b, c, pt: (b, 0))]
                 + [_page_spec((PAGE_SIZE * n_kv, HEAD_DIM), n_chunks, page0, i)
                    for i in range(PAGES_PER_STEP)] * 2,
        out_specs=pl.BlockSpec((SUBLANES, d), lambda b, c, pt: (b, 0)),
        scratch_shapes=[pltpu.VMEM((n_kv, kv_rep * SUBLANES, 1), _F32),
                        pltpu.VMEM((n_kv, kv_rep * SUBLANES, 1), _F32),
                        pltpu.VMEM((n_kv, kv_rep * SUBLANES, HEAD_DIM), _F32)],
    )
    return pl.pallas_call(
        functools.partial(_sattn_kernel, n_chunks=n_chunks, n_kv=n_kv, kv_rep=kv_rep),
        grid_spec=grid_spec,
        out_shape=jax.ShapeDtypeStruct((m, d), _F32),
        compiler_params=_cparams(("arbitrary", "arbitrary")),
        name="sample_attention",
    )(page_table, q, bias, k_new, v_new,
      *([pool_k] * PAGES_PER_STEP), *([pool_v] * PAGES_PER_STEP))


N_GATE = 5


def _gm_kernel(*refs, starts, ntiles, tms, seq_lens, has_state, cw):
    ng = len(starts)
    it = iter(refs)
    wp_ref, taps_ref = next(it), next(it)
    groups = []
    for gi in range(ng):
        g = dict(x=next(it), attn=next(it))
        if has_state[gi]:
            g["s0"], g["s1"] = next(it), next(it)
        groups.append(g)
    for g in groups:
        g["o"], g["tail"] = next(it), next(it)
    wb_ref = next(it)
    for g in groups:
        g["u"] = next(it)
    jj = pl.program_id(0)
    i = pl.program_id(1)
    nj = pl.num_programs(0) - 1

    @pl.when((jj < nj) & (i < N_GATE))
    def _():
        wb_ref[jj % 2, pl.ds(pl.multiple_of(i * cw, cw), cw), :] = wp_ref[0].astype(_BF16)

    for gi, g in enumerate(groups):
        tm, seq = tms[gi], seq_lens[gi]
        active = (jj > 0) & (i >= starts[gi]) & (i < starts[gi] + ntiles[gi])

        @pl.when(active)
        def _(g=g, gi=gi, tm=tm, seq=seq):
            u_ref = g["u"]
            z = lax.dot_general(g["x"][...], wb_ref[(jj + 1) % 2], _NT,
                                preferred_element_type=_F32)
            cb, cc, ch, ga, gc = [z[:, s * cw:(s + 1) * cw] for s in range(N_GATE)]
            u = cc * ch
            first = i == starts[gi]

            @pl.when(first)
            def _():
                u_ref[0:SUBLANES, :] = jnp.zeros((SUBLANES, cw), _F32)

            @pl.when(jnp.logical_not(first))
            def _():
                u_ref[0:SUBLANES, :] = u_ref[tm:tm + SUBLANES, :]

            u_ref[SUBLANES:, :] = u
            t = ((i - starts[gi]) * tm + lax.broadcasted_iota(jnp.int32, (tm, 1), 0)) % seq
            if "s0" in g:
                s0, s1 = g["s0"][...], g["s1"][...]
            else:
                s0 = s1 = jnp.zeros((tm, cw), _F32)
            u1 = jnp.where(t >= 1, u_ref[SUBLANES - 1:SUBLANES - 1 + tm, :], s1)
            u2 = jnp.where(t >= 2, u_ref[SUBLANES - 2:SUBLANES - 2 + tm, :],
                           jnp.where(t == 1, s1, s0))
            y = u * taps_ref[2:3, :] + u1 * taps_ref[1:2, :] + u2 * taps_ref[0:1, :]
            merged = (jax.nn.sigmoid(ga) * g["attn"][...].astype(_F32)
                      + jax.nn.sigmoid(gc) * (cb * y))
            g["o"][...] = merged.astype(g["o"].dtype)
            if tm >= seq:
                for s in range(tm // seq):
                    g["tail"][s] = u[(s + 1) * seq - SUBLANES:(s + 1) * seq, :]
            else:
                g["tail"][0] = u[tm - SUBLANES:, :]


def _gate_merge(xs, w_t, layer, attns, states, taps, *, col_start, tms, seq_lens, out_dtypes):
    assert CONV_WIDTH == 3
    kdim = w_t.shape[2]
    d = attns[0].shape[1]
    cw = min(d, GATE_CW)
    nj = d // cw
    ng = len(xs)
    ntiles = [x.shape[0] // tm for x, tm in zip(xs, tms)]
    starts = [int(s) for s in np.concatenate([[0], np.cumsum(ntiles)[:-1]])]
    ni = sum(ntiles)
    assert ni >= N_GATE and col_start % SUBLANES == 0 and d % cw == 0

    def tile(gi):
        return lambda i: jnp.clip(i - starts[gi], 0, ntiles[gi] - 1)

    col = lambda jj: jnp.maximum(jj - 1, 0)
    rowsel = lambda jj, i, t: jnp.where(jj > 0, t(i), 0)
    nxt = lambda jj: jnp.minimum(jj, nj - 1)
    seg = lambda i: jnp.minimum(i, N_GATE - 1)
    in_specs = [
        pl.BlockSpec((pl.Element(1), pl.Element(cw), pl.Element(kdim)),
                     lambda jj, i: (layer, pl.multiple_of(col_start + seg(i) * d + nxt(jj) * cw,
                                                          SUBLANES), 0)),
        pl.BlockSpec((CONV_WIDTH, cw), lambda jj, i: (0, col(jj)))]
    args = [w_t, taps]
    for gi in range(ng):
        tm, t = tms[gi], tile(gi)
        in_specs.append(pl.BlockSpec((tm, kdim), lambda jj, i, t=t: (t(i), 0)))
        act = pl.BlockSpec((tm, cw), lambda jj, i, t=t: (t(i), col(jj)))
        in_specs.append(act)
        args += [xs[gi], attns[gi]]
        if states[gi] is not None:
            in_specs += [act, act]
            args += list(states[gi])
    out_shapes, out_specs = [], []
    for gi in range(ng):
        tm, t, m, seq = tms[gi], tile(gi), xs[gi].shape[0], seq_lens[gi]
        spt, tps = max(tm // seq, 1), max(seq // tm, 1)
        out_shapes += [jax.ShapeDtypeStruct((m, d), out_dtypes[gi]),
                       jax.ShapeDtypeStruct((m // seq, SUBLANES, d), _F32)]
        out_specs += [
            pl.BlockSpec((tm, cw), lambda jj, i, t=t: (rowsel(jj, i, t), col(jj))),
            pl.BlockSpec((spt, SUBLANES, cw),
                         lambda jj, i, t=t, tps=tps: (rowsel(jj, i, t) // tps, 0, col(jj)))]
    scratch = [pltpu.VMEM((2, N_GATE * cw, kdim), _BF16)]
    scratch += [pltpu.VMEM((tm + SUBLANES, cw), _F32) for tm in tms]
    res = pl.pallas_call(
        functools.partial(_gm_kernel, starts=starts, ntiles=ntiles, tms=tuple(tms),
                          seq_lens=tuple(seq_lens),
                          has_state=tuple(s is not None for s in states), cw=cw),
        grid=(nj + 1, ni),
        in_specs=in_specs,
        out_specs=out_specs,
        out_shape=out_shapes,
        scratch_shapes=scratch,
        compiler_params=_cparams(("arbitrary", "arbitrary")),
        name="gate_merge",
    )(*args)
    return [(res[2 * gi], res[2 * gi + 1]) for gi in range(ng)]


def _rope_tables(positions, half, n_rot_lanes=LANES):
    pos = np.asarray(positions, np.float64)[:, None]
    lane = np.arange(LANES)
    inv = ROPE_THETA ** (-(lane % half).astype(np.float64) / half)
    ang = pos * inv[None, :]
    sign = np.where((lane % (2 * half)) < half, -1.0, 1.0)
    rot = (lane < n_rot_lanes)[None, :]
    cos = np.where(rot, np.cos(ang), 1.0)
    sin = np.where(rot, np.sin(ang) * sign[None, :], 0.0)
    return jnp.asarray(cos, _F32), jnp.asarray(sin, _F32)


def kernel(x_prompt, x_sample, cache_k, cache_v, cache_kidx, state_conv, page_table,
           w_in, conv_w, w_o, g_mix, g_mlp, w_up, w_down, g_final):
    B, S, D = x_prompt.shape
    Bd, T, _ = x_sample.shape
    depth = w_in.shape[0]
    n_pool = cache_k.shape[1]
    past = page_table.shape[1] * PAGE_SIZE
    R = SUBLANES
    kvw = N_KV_HEADS * HEAD_DIM
    half, ihalf = HEAD_DIM // 2, IDX_DIM // 2
    Mp, Ms = B * S, Bd * R
    tms = [min(Mp, 1024, S // 2), Ms]
    tms_down = [min(Mp, 2048), Ms]

    c_q, c_k = 0, N_HEADS * HEAD_DIM
    c_v = c_k + kvw
    c_qi = c_v + kvw
    c_kiwi = c_qi + IDX_HEADS * IDX_DIM
    c_gate = c_kiwi + IDX_DIM + IDX_HEADS

    pos_p = np.arange(max(S, tms[0])) % S
    pos_s = np.tile(past + np.arange(R), Bd)
    r128 = [_rope_tables(pos_p, half), _rope_tables(pos_s, half)]
    r64 = [_rope_tables(pos_p, ihalf), _rope_tables(pos_s, ihalf)]
    rkiwi = [_rope_tables(pos_p, ihalf, IDX_DIM), _rope_tables(pos_s, ihalf, IDX_DIM)]

    xp = x_prompt.reshape(Mp, D)
    xs = jnp.pad(x_sample, ((0, 0), (0, R - T), (0, 0))).reshape(Ms, D)
    w_in_t = jnp.swapaxes(w_in, 1, 2)
    pool_k = cache_k.reshape(depth * n_pool * PAGE_SIZE * N_KV_HEADS, HEAD_DIM)
    pool_v = cache_v.reshape(depth * n_pool * PAGE_SIZE * N_KV_HEADS, HEAD_DIM)
    pool_ki_t = jnp.swapaxes(cache_kidx, 2, 3).reshape(depth * n_pool, IDX_DIM, PAGE_SIZE)

    outs = {n: [] for n in ("kp", "vp", "kip", "cp", "ks", "vs", "kis", "cs")}
    for l in range(depth):
        hs = [_rmsnorm(xp, g_mix[l], _BF16), _rmsnorm(xs, g_mix[l], _BF16)]
        mm = functools.partial(_matmul, hs, w_in_t, l, tms=tms, w_t=True)
        qp, qs = mm(col_start=c_q, n_cols=N_HEADS * HEAD_DIM, out_dtype=_BF16,
                    rope=(half, r128), scale=HEAD_DIM ** -0.5)
        kp, ks = mm(col_start=c_k, n_cols=kvw, out_dtype=_F32, rope=(half, r128))
        vp, vs = mm(col_start=c_v, n_cols=kvw, out_dtype=_F32)
        qip, qis = mm(col_start=c_qi, n_cols=IDX_HEADS * IDX_DIM, out_dtype=_BF16,
                      rope=(ihalf, r64), pair_major=[True, False])
        kwp, kws = mm(col_start=c_kiwi, n_cols=LANES, out_dtype=_F32, rope=(ihalf, rkiwi))
        page0 = l * n_pool

        kip = kwp[:, :IDX_DIM]
        wit = kwp[:, IDX_DIM:IDX_DIM + IDX_HEADS].T
        attn_p = _prompt_attention(qip, wit, kip, qp, kp, vp, batch=B, seq=S)

        qi_hm = qis.reshape(Bd, R, IDX_HEADS, IDX_DIM).transpose(0, 2, 1, 3).reshape(
            Bd, IDX_HEADS * R, IDX_DIM)
        kis = kws[:, :IDX_DIM]
        w_hm = kws[:, IDX_DIM:IDX_DIM + IDX_HEADS].reshape(Bd, R, IDX_HEADS).transpose(
            0, 2, 1).reshape(Bd, IDX_HEADS * R, 1)
        bias = _sample_index(page_table, qi_hm, w_hm, kis, pool_ki_t, page0, 0, t_real=T)
        attn_s = _sample_attention(page_table, qs.astype(_F32), bias, ks, vs, pool_k, pool_v,
                                   page0)

        s0, s1 = [jnp.repeat(state_conv[l][:, r:r + 1, :], R, axis=1).reshape(Ms, D)
                  for r in range(CONV_WIDTH - 1)]
        (mg_p, tail_p), (mg_s, tail_s) = _gate_merge(
            hs, w_in_t, l, [attn_p, attn_s], [None, (s0, s1)], conv_w[l], col_start=c_gate,
            tms=tms, seq_lens=[S, R], out_dtypes=[_BF16, _F32])
        xp, xs = _matmul([mg_p, mg_s.astype(_BF16)], w_o, l, col_start=0, n_cols=D,
                         out_dtype=_F32, tms=tms, epi="residual", residuals=[xp, xs])
        h2 = [_rmsnorm(xp, g_mlp[l], _BF16), _rmsnorm(xs, g_mlp[l], _BF16)]
        hid = _matmul(h2, w_up, l, col_start=0, n_cols=w_up.shape[2], out_dtype=_BF16, tms=tms,
                      epi="relu2")
        xp, xs = _matmul(hid, w_down, l, col_start=0, n_cols=D, out_dtype=_F32, tms=tms_down,
                         epi="residual", residuals=[xp, xs], tn=min(D, 1024),
                         tk=min(w_down.shape[1], 1024))

        outs["kp"].append(kp.reshape(B, S, N_KV_HEADS, HEAD_DIM))
        outs["vp"].append(vp.reshape(B, S, N_KV_HEADS, HEAD_DIM))
        outs["kip"].append(kip.reshape(B, S, IDX_DIM))
        outs["cp"].append(tail_p[:, R - (CONV_WIDTH - 1):, :])
        outs["ks"].append(ks.reshape(Bd, R, N_KV_HEADS, HEAD_DIM)[:, :T])
        outs["vs"].append(vs.reshape(Bd, R, N_KV_HEADS, HEAD_DIM)[:, :T])
        outs["kis"].append(kis.reshape(Bd, R, IDX_DIM)[:, :T])
        outs["cs"].append(tail_s[:, T - (CONV_WIDTH - 1):T, :])

    y_prompt = _rmsnorm(xp, g_final, _F32).reshape(B, S, D)
    y_sample = _rmsnorm(xs, g_final, _F32).reshape(Bd, R, D)[:, :T]
    st = lambda n: jnp.stack(outs[n])
    return (y_prompt, y_sample, st("kp"), st("vp"), st("kip"), st("cp"),
            st("ks"), st("vs"), st("kis"), st("cs"))
```

```python
import functools

import numpy as np
import jax
import jax.numpy as jnp
from jax import lax
from jax.experimental import pallas as pl
from jax.experimental.pallas import tpu as pltpu

HEAD_DIM = 128
N_HEADS = 32
N_KV_HEADS = 8
IDX_HEADS = 32
IDX_DIM = 64
TOPK_MAX = 256
CONV_WIDTH = 3
PAGE_SIZE = 128
ROPE_THETA = 10000.0
NORM_EPS = 1e-6
NEG = -1e30

LANES = 128
SUBLANES = 8
VMEM_LIMIT = 60 * 1024 * 1024
PAGES_PER_STEP = 8
INT_MIN = -2 ** 31
TN = 1024
SUB_TN = 512
SUB_TM = 512
GATE_CW = 256

_F32 = jnp.float32
_BF16 = jnp.bfloat16
_NT = (((1,), (1,)), ((), ()))


def _cparams(sem):
    return pltpu.CompilerParams(dimension_semantics=sem, vmem_limit_bytes=VMEM_LIMIT)


def _rmsnorm_kernel(x_ref, g_ref, o_ref):
    x = x_ref[...]
    y = x * lax.rsqrt(jnp.mean(x * x, axis=-1, keepdims=True) + NORM_EPS)
    o_ref[...] = (y * g_ref[...]).astype(o_ref.dtype)


def _rmsnorm(x, g, out_dtype):
    m, d = x.shape
    tm = min(m, 256)
    return pl.pallas_call(
        _rmsnorm_kernel,
        grid=(m // tm,),
        in_specs=[pl.BlockSpec((tm, d), lambda i: (i, 0)),
                  pl.BlockSpec((1, d), lambda i: (0, 0))],
        out_specs=pl.BlockSpec((tm, d), lambda i: (i, 0)),
        out_shape=jax.ShapeDtypeStruct((m, d), out_dtype),
        compiler_params=_cparams(("arbitrary",)),
        name="rmsnorm",
    )(x, g.reshape(1, d))


def _cast_kernel(w_ref, o_ref):
    o_ref[...] = w_ref[...].astype(o_ref.dtype)


def _cast_bf16(w, layer):
    _, kdim, n = w.shape
    tr = min(kdim, 512)
    return pl.pallas_call(
        _cast_kernel,
        grid=(kdim // tr,),
        in_specs=[pl.BlockSpec((1, tr, n), lambda i: (layer, i, 0))],
        out_specs=pl.BlockSpec((1, tr, n), lambda i: (0, i, 0)),
        out_shape=jax.ShapeDtypeStruct((1, kdim, n), _BF16),
        compiler_params=_cparams(("arbitrary",)),
        name="cast_bf16",
    )(w)


def _rope_slab(z, cos, sin, half):
    if 2 * half == LANES:
        partner = pltpu.roll(z, half, 1)
    else:
        lane = lax.broadcasted_iota(jnp.int32, z.shape, 1)
        first = (lane % (2 * half)) < half
        partner = jnp.where(first, pltpu.roll(z, LANES - half, 1), pltpu.roll(z, half, 1))
    return z * cos + partner * sin


def _epilogue(g, z, c0, *, epi, rope_half, scale):
    o_ref = g["o"]
    width = z.shape[1]
    if rope_half:
        cos, sin = g["cos"][...], g["sin"][...]
        for s in range(width // LANES):
            slab = _rope_slab(z[:, s * LANES:(s + 1) * LANES], cos, sin, rope_half)
            if scale != 1.0:
                slab = slab * scale
            if g["pm"]:
                o_ref[c0 // LANES + s] = slab.astype(o_ref.dtype)
            else:
                o_ref[:, c0 + s * LANES:c0 + (s + 1) * LANES] = slab.astype(o_ref.dtype)
        return
    if epi == "relu2":
        r = jnp.maximum(z, 0.0)
        z = r * r
    elif epi == "residual":
        z = g["res"][:, c0:c0 + width] + z
    o_ref[:, c0:c0 + width] = z.astype(o_ref.dtype)


def _unpack_groups(it, ng, rope_half, epi, pair_major):
    groups = []
    for _ in range(ng):
        g = dict(x=next(it))
        if rope_half:
            g["cos"], g["sin"] = next(it), next(it)
        if epi == "residual":
            g["res"] = next(it)
        groups.append(g)
    for gi, g in enumerate(groups):
        g["o"] = next(it)
        g["pm"] = pair_major[gi]
    return groups


def _mm_ws_kernel(*refs, starts, ntiles, npieces, epi, rope_half, scale, pair_major, w_t):
    it = iter(refs)
    wp_ref = next(it)
    groups = _unpack_groups(it, len(starts), rope_half, epi, pair_major)
    wb_ref = next(it)
    jj = pl.program_id(0)
    i = pl.program_id(1)
    nj = pl.num_programs(0) - 1

    @pl.when((jj < nj) & (i < npieces))
    def _():
        piece = wp_ref[0].astype(_BF16)
        rows = piece.shape[0]
        wb_ref[jj % 2, pl.ds(pl.multiple_of(i * rows, rows), rows), :] = piece

    for gi, g in enumerate(groups):
        active = (jj > 0) & (i >= starts[gi]) & (i < starts[gi] + ntiles[gi])

        @pl.when(active)
        def _(g=g):
            slot = (jj + 1) % 2
            tn = wb_ref.shape[1] if w_t else wb_ref.shape[2]
            sub = min(tn, SUB_TN)
            for c0 in range(0, tn, sub):
                if w_t:
                    z = lax.dot_general(g["x"][...], wb_ref[slot, c0:c0 + sub, :], _NT,
                                        preferred_element_type=_F32)
                else:
                    z = jnp.dot(g["x"][...], wb_ref[slot, :, c0:c0 + sub],
                                preferred_element_type=_F32)
                _epilogue(g, z, c0, epi=epi, rope_half=rope_half, scale=scale)


def _mm_kt_kernel(*refs, starts, ntiles):
    it = iter(refs)
    w_ref = next(it)
    groups = _unpack_groups(it, len(starts), 0, "residual", [False] * len(starts))
    i = pl.program_id(1)
    k = pl.program_id(2)
    for gi, g in enumerate(groups):
        active = (i >= starts[gi]) & (i < starts[gi] + ntiles[gi])

        @pl.when(active)
        def _(g=g):
            @pl.when(k == 0)
            def _():
                g["o"][...] = g["res"][...]

            tm = g["x"].shape[0]
            sub = min(tm, SUB_TM)
            for r0 in range(0, tm, sub):
                g["o"][r0:r0 + sub, :] += jnp.dot(g["x"][r0:r0 + sub, :], w_ref[0].astype(_BF16),
                                                   preferred_element_type=_F32)


def _matmul(xs, w, layer, *, col_start, n_cols, out_dtype, tms, epi="none", rope=None,
            scale=1.0, residuals=None, pair_major=None, tn=TN, tk=None, w_t=False):
    kdim = w.shape[2] if w_t else w.shape[1]
    tn = min(tn, n_cols)
    nj = n_cols // tn
    cb0 = col_start // tn
    assert n_cols % tn == 0
    assert col_start % (SUBLANES if w_t else tn) == 0
    ng = len(xs)
    pair_major = [False] * ng if pair_major is None else pair_major
    ntiles = [x.shape[0] // tm for x, tm in zip(xs, tms)]
    starts = [int(s) for s in np.concatenate([[0], np.cumsum(ntiles)[:-1]])]
    ni = sum(ntiles)
    ws = tk is None
    if ws:
        tk = kdim
        npieces = max(p for p in (8, 4, 2, 1) if p <= ni)
    else:
        assert epi == "residual" and out_dtype == _F32 and not w_t and kdim % tk == 0
        assert rope is None and not any(pair_major)

    def tile(gi):
        return lambda i: jnp.clip(i - starts[gi], 0, ntiles[gi] - 1)

    if ws:
        col = lambda a: jnp.maximum(a[0] - 1, 0)
        krow = lambda a: 0
        rowsel = lambda a, t: jnp.where(a[0] > 0, t(a[1]), 0)
        nxt = lambda a: jnp.minimum(a[0], nj - 1)
        pc = lambda a: jnp.minimum(a[1], npieces - 1)
        if w_t:
            pr = tn // npieces
            w_spec = pl.BlockSpec(
                (pl.Element(1), pl.Element(pr), pl.Element(tk)),
                lambda *a: (layer, pl.multiple_of(col_start + nxt(a) * tn + pc(a) * pr, SUBLANES),
                            0))
        else:
            w_spec = pl.BlockSpec((1, tk // npieces, tn), lambda *a: (layer, pc(a), cb0 + nxt(a)))
    else:
        col = lambda a: a[0]
        krow = lambda a: a[2]
        rowsel = lambda a, t: t(a[1])
        w_spec = pl.BlockSpec((1, tk, tn), lambda *a: (layer, a[2], cb0 + a[0]))

    in_specs, args = [w_spec], [w]
    rope_half = 0
    for gi in range(ng):
        tm, t = tms[gi], tile(gi)
        in_specs.append(pl.BlockSpec((tm, tk), lambda *a, t=t: (t(a[1]), krow(a))))
        args.append(xs[gi])
        if rope is not None:
            rope_half = rope[0]
            cos, sin = rope[1][gi]
            nper = cos.shape[0] // tm
            tspec = pl.BlockSpec((tm, LANES), lambda *a, t=t, nper=nper: (t(a[1]) % nper, 0))
            in_specs += [tspec, tspec]
            args += [cos, sin]
        if epi == "residual":
            mode = {} if ws else dict(pipeline_mode=pl.Buffered(1))
            in_specs.append(pl.BlockSpec((tm, tn), lambda *a, t=t: (t(a[1]), col(a)), **mode))
            args.append(residuals[gi])
    out_shapes, out_specs = [], []
    for gi in range(ng):
        tm, t, m = tms[gi], tile(gi), xs[gi].shape[0]
        if pair_major[gi]:
            out_shapes.append(jax.ShapeDtypeStruct((n_cols // LANES, m, LANES), out_dtype))
            out_specs.append(pl.BlockSpec((tn // LANES, tm, LANES),
                                          lambda *a, t=t: (col(a), rowsel(a, t), 0)))
        else:
            out_shapes.append(jax.ShapeDtypeStruct((m, n_cols), out_dtype))
            out_specs.append(pl.BlockSpec((tm, tn), lambda *a, t=t: (rowsel(a, t), col(a))))
    if ws:
        body = functools.partial(_mm_ws_kernel, starts=starts, ntiles=ntiles, npieces=npieces,
                                 epi=epi, rope_half=rope_half, scale=scale,
                                 pair_major=tuple(pair_major), w_t=w_t)
        grid = (nj + 1, ni)
        scratch = [pltpu.VMEM((2, tn, tk) if w_t else (2, tk, tn), _BF16)]
    else:
        body = functools.partial(_mm_kt_kernel, starts=starts, ntiles=ntiles)
        grid = (nj, ni, kdim // tk)
        scratch = []
    return pl.pallas_call(
        body,
        grid=grid,
        in_specs=in_specs,
        out_specs=out_specs,
        out_shape=out_shapes,
        scratch_shapes=scratch,
        compiler_params=_cparams(("arbitrary",) * len(grid)),
        name="matmul_" + epi + ("_rope%d" % rope_half if rope_half else ""),
    )(*args)


def _sortable(x):
    bits = pltpu.bitcast(x, jnp.int32)
    return jnp.where(bits < 0, bits ^ jnp.int32(0x7FFFFFFF), bits)


def _kth_largest(keys, n_sel, count_fn):
    cnt = count_fn((keys >= 0).astype(jnp.int32))
    thr0 = jnp.where(cnt >= n_sel, jnp.int32(0), jnp.int32(INT_MIN))

    def body(i, thr):
        cand = thr | jnp.left_shift(jnp.int32(1), 30 - i)
        cnt = count_fn((keys >= cand).astype(jnp.int32))
        return jnp.where(cnt >= n_sel, cand, thr)

    return lax.fori_loop(0, 31, body, thr0)


def _select_topk(keys, idx, n_sel, count_fn, idx_bits):
    thr = _kth_largest(keys, n_sel, count_fn)
    gt = keys > thr
    eq = keys == thr
    need = n_sel - count_fn(gt.astype(jnp.int32))
    surplus = count_fn(eq.astype(jnp.int32)) - need

    def tie_cut():
        def body(b, lo):
            cand = lo + jnp.left_shift(jnp.int32(1), idx_bits - 1 - b)
            below = count_fn((eq & (idx < cand)).astype(jnp.int32))
            return jnp.where(below < need, cand, lo)

        return lax.fori_loop(0, idx_bits, body, jnp.zeros_like(thr))

    cut = lax.cond(jnp.max(surplus) > 0, tie_cut,
                   lambda: jnp.full_like(thr, jnp.iinfo(jnp.int32).max))
    return gt | (eq & (idx <= cut))


def _pattn_kernel(qi_ref, wit_ref, ki_ref, q_ref, k_ref, v_ref, o_ref, sc_ref, bias_ref, *,
                  n_sel, kv_rep):
    jq = pl.program_id(1)
    g = pl.program_id(2)
    s_len, tq = sc_ref.shape
    n_pairs = qi_ref.shape[0]

    def select(klen):
        kb = ki_ref[0:klen, :].astype(_BF16)
        zeros = jnp.zeros_like(kb)
        k_even = jnp.concatenate([kb, zeros], axis=1)
        k_odd = jnp.concatenate([zeros, kb], axis=1)
        w_scale = (IDX_HEADS ** -0.5) * (IDX_DIM ** -0.5)
        sc_ref[0:klen, :] = jnp.zeros((klen, tq), _F32)

        def body(p, carry):
            x = qi_ref[p]
            d0 = lax.dot_general(k_even, x, _NT, preferred_element_type=_F32)
            d1 = lax.dot_general(k_odd, x, _NT, preferred_element_type=_F32)
            w0 = wit_ref[pl.ds(2 * p, 1), :] * w_scale
            w1 = wit_ref[pl.ds(2 * p + 1, 1), :] * w_scale
            sc_ref[0:klen, :] += jnp.maximum(d0, 0.0) * w0 + jnp.maximum(d1, 0.0) * w1
            return carry

        lax.fori_loop(0, n_pairs, body, 0)

        key_pos = lax.broadcasted_iota(jnp.int32, (klen, tq), 0)
        q_pos = jq * tq + lax.broadcasted_iota(jnp.int32, (klen, tq), 1)
        adm = key_pos <= q_pos
        keys = _sortable(jnp.where(adm, sc_ref[0:klen, :], NEG))
        sel = _select_topk(keys, key_pos, n_sel, lambda c: jnp.sum(c, axis=0, keepdims=True),
                           (s_len - 1).bit_length()) & adm
        bias_ref[:, 0:klen] = jnp.where(sel, 0.0, NEG).astype(_F32).T

    def attend(klen):
        kg = k_ref[0:klen, :].astype(_BF16)
        vg = v_ref[0:klen, :].astype(_BF16)
        bias = bias_ref[:, 0:klen]
        for r in range(kv_rep):
            qh = q_ref[:, r * HEAD_DIM:(r + 1) * HEAD_DIM]
            s = lax.dot_general(qh, kg, _NT, preferred_element_type=_F32) + bias
            m = jnp.max(s, axis=1, keepdims=True)
            p = jnp.exp(s - m)
            l = jnp.sum(p, axis=1, keepdims=True)
            o = jnp.dot(p.astype(_BF16), vg, preferred_element_type=_F32) / l
            o_ref[:, r * HEAD_DIM:(r + 1) * HEAD_DIM] = o.astype(o_ref.dtype)

    for n in range(1, s_len // tq + 1):
        @pl.when(jq == n - 1)
        def _(klen=n * tq):
            @pl.when(g == 0)
            def _():
                select(klen)

            attend(klen)


def _prompt_attention(qi_pm, wit, ki, q, k, v, *, batch, seq):
    m, d = q.shape
    n_kv = k.shape[1] // HEAD_DIM
    kv_rep = (d // HEAD_DIM) // n_kv
    tq = min(seq, 256)
    nq = seq // tq
    n_sel = min(TOPK_MAX, seq // 4)
    n_pairs = qi_pm.shape[0]
    gw = kv_rep * HEAD_DIM
    return pl.pallas_call(
        functools.partial(_pattn_kernel, n_sel=n_sel, kv_rep=kv_rep),
        grid=(batch, nq, n_kv),
        in_specs=[
            pl.BlockSpec((n_pairs, tq, LANES), lambda b, j, g: (0, b * nq + j, 0)),
            pl.BlockSpec((wit.shape[0], tq), lambda b, j, g: (0, b * nq + j)),
            pl.BlockSpec((seq, ki.shape[1]), lambda b, j, g: (b, 0)),
            pl.BlockSpec((tq, gw), lambda b, j, g: (b * nq + j, g)),
            pl.BlockSpec((seq, HEAD_DIM), lambda b, j, g: (b, g)),
            pl.BlockSpec((seq, HEAD_DIM), lambda b, j, g: (b, g)),
        ],
        out_specs=pl.BlockSpec((tq, gw), lambda b, j, g: (b * nq + j, g)),
        out_shape=jax.ShapeDtypeStruct((m, d), _BF16),
        scratch_shapes=[pltpu.VMEM((seq, tq), _F32), pltpu.VMEM((tq, seq), _F32)],
        compiler_params=_cparams(("arbitrary", "arbitrary", "arbitrary")),
        name="prompt_attention",
    )(qi_pm, wit, ki, q, k, v)


def _sidx_kernel(pt_ref, qi_ref, w_ref, kin_ref, *rest, n_sel, t_real, n_chunks):
    pages = rest[:PAGES_PER_STEP]
    bias_ref, sc_ref = rest[PAGES_PER_STEP], rest[PAGES_PER_STEP + 1]
    c = pl.program_id(1)
    rows = SUBLANES
    chunk = PAGES_PER_STEP * PAGE_SIZE
    w = w_ref[0] * ((IDX_HEADS ** -0.5) * (IDX_DIM ** -0.5))
    qi = qi_ref[0]

    def scores(d):
        d = jnp.maximum(d, 0.0) * w
        acc = d[0:rows]
        for h in range(1, IDX_HEADS):
            acc = acc + d[h * rows:(h + 1) * rows]
        return acc

    @pl.when(c < n_chunks)
    def _():
        keys_t = jnp.concatenate([p[0] for p in pages], axis=1).astype(_BF16)
        sc_ref[c] = scores(jnp.dot(qi, keys_t, preferred_element_type=_F32))

    @pl.when(c == n_chunks)
    def _():
        kn = kin_ref[...].astype(_BF16)
        kn = jnp.concatenate([kn, jnp.zeros((PAGE_SIZE - rows, kn.shape[1]), _BF16)], axis=0)
        s_new = scores(lax.dot_general(qi, kn, _NT, preferred_element_type=_F32))
        s_new = jnp.concatenate([s_new, jnp.zeros((rows, chunk - PAGE_SIZE), _F32)], axis=1)
        shape = (n_chunks + 1, rows, chunk)
        ci = lax.broadcasted_iota(jnp.int32, shape, 0)
        qrow = lax.broadcasted_iota(jnp.int32, shape, 1)
        lane = lax.broadcasted_iota(jnp.int32, shape, 2)
        adm = (ci < n_chunks) | ((lane <= qrow) & (lane < t_real))
        sc_ref[n_chunks] = s_new
        keys = _sortable(jnp.where(adm, sc_ref[...], NEG))
        count = lambda x: jnp.sum(jnp.sum(x, axis=0, keepdims=True), axis=2, keepdims=True)
        key_idx = ci * chunk + lane
        n_keys = (n_chunks + 1) * chunk
        sel = _select_topk(keys, key_idx, n_sel, count, (n_keys - 1).bit_length()) & adm
        bias_ref[0] = jnp.where(sel, 0.0, NEG).astype(_F32)


def _page_spec(block, n_chunks, page0, i):
    zeros = (0,) * (len(block) - 1)
    return pl.BlockSpec(
        block,
        lambda b, c, pt: (page0 + pt[b, jnp.minimum(c, n_chunks - 1) * PAGES_PER_STEP + i],)
        + zeros)


def _sample_index(page_table, qi_hm, w_hm, ki_new, pool_ki, page0, row0, *, t_real):
    bd, n_pages = page_table.shape
    n_chunks = n_pages // PAGES_PER_STEP
    chunk = PAGES_PER_STEP * PAGE_SIZE
    past = n_pages * PAGE_SIZE
    n_sel = min(TOPK_MAX, (past + t_real) // 4)
    hr = qi_hm.shape[1]
    grid_spec = pltpu.PrefetchScalarGridSpec(
        num_scalar_prefetch=1,
        grid=(bd, n_chunks + 1),
        in_specs=[pl.BlockSpec((1, hr, IDX_DIM), lambda b, c, pt: (b, 0, 0)),
                  pl.BlockSpec((1, hr, 1), lambda b, c, pt: (b, 0, 0)),
                  pl.BlockSpec((SUBLANES, IDX_DIM), lambda b, c, pt: (row0 + b, 0))]
                 + [_page_spec((1, IDX_DIM, PAGE_SIZE), n_chunks, page0, i)
                    for i in range(PAGES_PER_STEP)],
        out_specs=pl.BlockSpec((1, n_chunks + 1, SUBLANES, chunk), lambda b, c, pt: (b, 0, 0, 0)),
        scratch_shapes=[pltpu.VMEM((n_chunks + 1, SUBLANES, chunk), _F32)],
    )
    return pl.pallas_call(
        functools.partial(_sidx_kernel, n_sel=n_sel, t_real=t_real, n_chunks=n_chunks),
        grid_spec=grid_spec,
        out_shape=jax.ShapeDtypeStruct((bd, n_chunks + 1, SUBLANES, chunk), _F32),
        compiler_params=_cparams(("arbitrary", "arbitrary")),
        name="sample_index",
    )(page_table, qi_hm, w_hm, ki_new, *([pool_ki] * PAGES_PER_STEP))


def _sattn_kernel(pt_ref, q_ref, bias_ref, kn_ref, vn_ref, *rest, n_chunks, n_kv, kv_rep):
    kpages = rest[:PAGES_PER_STEP]
    vpages = rest[PAGES_PER_STEP:2 * PAGES_PER_STEP]
    o_ref, m_ref, l_ref, acc_ref = rest[2 * PAGES_PER_STEP:]
    c = pl.program_id(1)
    rows = SUBLANES

    @pl.when(c == 0)
    def _():
        m_ref[...] = jnp.full_like(m_ref, -jnp.inf)
        l_ref[...] = jnp.zeros_like(l_ref)
        acc_ref[...] = jnp.zeros_like(acc_ref)

    gr = kv_rep * rows

    def update(kgs, vgs, bias):
        parts = []
        for g in range(n_kv):
            qg = jnp.concatenate(
                [q_ref[:, (g * kv_rep + r) * HEAD_DIM:(g * kv_rep + r + 1) * HEAD_DIM]
                 for r in range(kv_rep)], axis=0).astype(_BF16)
            parts.append(lax.dot_general(qg, kgs[g], _NT, preferred_element_type=_F32))
        s = jnp.concatenate(parts, axis=0) + jnp.concatenate([bias] * (n_kv * kv_rep), axis=0)
        m_old = m_ref[...]
        m_new = jnp.maximum(m_old, jnp.max(s, axis=1, keepdims=True))
        alpha = jnp.exp(m_old - m_new)
        p = jnp.exp(s - m_new)
        l_ref[...] = alpha * l_ref[...] + jnp.sum(p, axis=1, keepdims=True)
        pb = p.astype(_BF16)
        pv = jnp.concatenate(
            [jnp.dot(pb[g * gr:(g + 1) * gr], vgs[g], preferred_element_type=_F32)
             for g in range(n_kv)], axis=0)
        acc_ref[...] = alpha * acc_ref[...] + pv
        m_ref[...] = m_new

    @pl.when(c < n_chunks)
    def _():
        def head_rows(pages, g):
            rows_g = pl.ds(g, PAGE_SIZE, stride=n_kv)
            return jnp.concatenate([p[rows_g, :] for p in pages], axis=0).astype(_BF16)

        update([head_rows(kpages, g) for g in range(n_kv)],
               [head_rows(vpages, g) for g in range(n_kv)], bias_ref[0, 0])

    @pl.when(c == n_chunks)
    def _():
        pad = jnp.zeros((PAGE_SIZE - rows, HEAD_DIM), _BF16)

        def new_rows(ref, g):
            return jnp.concatenate(
                [ref[:, g * HEAD_DIM:(g + 1) * HEAD_DIM].astype(_BF16), pad], axis=0)

        update([new_rows(kn_ref, g) for g in range(n_kv)],
               [new_rows(vn_ref, g) for g in range(n_kv)], bias_ref[0, 0][:, :PAGE_SIZE])
        o = acc_ref[...] / l_ref[...]
        for h in range(n_kv * kv_rep):
            o_ref[:, h * HEAD_DIM:(h + 1) * HEAD_DIM] = (
                o[h * rows:(h + 1) * rows].astype(o_ref.dtype))


def _sample_attention(page_table, q, bias, k_new, v_new, pool_k, pool_v, page0):
    bd, n_pages = page_table.shape
    n_chunks = n_pages // PAGES_PER_STEP
    chunk = PAGES_PER_STEP * PAGE_SIZE
    m, d = q.shape
    kvw = k_new.shape[1]
    n_kv = kvw // HEAD_DIM
    kv_rep = (d // HEAD_DIM) // n_kv
    grid_spec = pltpu.PrefetchScalarGridSpec(
        num_scalar_prefetch=1,
        grid=(bd, n_chunks + 1),
        in_specs=[pl.BlockSpec((SUBLANES, d), lambda b, c, pt: (b, 0)),
                  pl.BlockSpec((1, 1, SUBLANES, chunk), lambda b, c, pt: (b, c, 0, 0)),
                  pl.BlockSpec((SUBLANES, kvw), lambda b, c, pt: (b, 0)),
                  pl.BlockSpec((SUBLANES, kvw), lambda b, c, pt: (b, 0))]
                 + [_page_spec((PAGE_SIZE * n_kv, HEAD_DIM), n_chunks, page0, i)
                    for i in range(PAGES_PER_STEP)] * 2,
        out_specs=pl.BlockSpec((SUBLANES, d), lambda b, c, pt: (b, 0)),
        scratch_shapes=[pltpu.VMEM((n_kv * kv_rep * SUBLANES, 1), _F32),
                        pltpu.VMEM((n_kv * kv_rep * SUBLANES, 1), _F32),
                        pltpu.VMEM((n_kv * kv_rep * SUBLANES, HEAD_DIM), _F32)],
    )
    return pl.pallas_call(
        functools.partial(_sattn_kernel, n_chunks=n_chunks, n_kv=n_kv, kv_rep=kv_rep),
        grid_spec=grid_spec,
        out_shape=jax.ShapeDtypeStruct((m, d), _F32),
        compiler_params=_cparams(("arbitrary", "arbitrary")),
        name="sample_attention",
    )(page_table, q, bias, k_new, v_new,
      *([pool_k] * PAGES_PER_STEP), *([pool_v] * PAGES_PER_STEP))


N_GATE = 5


def _gm_kernel(*refs, starts, ntiles, tms, seq_lens, has_state, cw):
    ng = len(starts)
    it = iter(refs)
    wp_ref, taps_ref = next(it), next(it)
    groups = []
    for gi in range(ng):
        g = dict(x=next(it), attn=next(it))
        if has_state[gi]:
            g["s0"], g["s1"] = next(it), next(it)
        groups.append(g)
    for g in groups:
        g["o"], g["tail"] = next(it), next(it)
    wb_ref = next(it)
    for g in groups:
        g["u"] = next(it)
    jj = pl.program_id(0)
    i = pl.program_id(1)
    nj = pl.num_programs(0) - 1

    @pl.when((jj < nj) & (i < N_GATE))
    def _():
        wb_ref[jj % 2, pl.ds(pl.multiple_of(i * cw, cw), cw), :] = wp_ref[0].astype(_BF16)

    for gi, g in enumerate(groups):
        tm, seq = tms[gi], seq_lens[gi]
        active = (jj > 0) & (i >= starts[gi]) & (i < starts[gi] + ntiles[gi])

        @pl.when(active)
        def _(g=g, gi=gi, tm=tm, seq=seq):
            u_ref = g["u"]
            z = lax.dot_general(g["x"][...], wb_ref[(jj + 1) % 2], _NT,
                                preferred_element_type=_F32)
            cb, cc, ch, ga, gc = [z[:, s * cw:(s + 1) * cw] for s in range(N_GATE)]
            u = cc * ch
            first = i == starts[gi]

            @pl.when(first)
            def _():
                u_ref[0:SUBLANES, :] = jnp.zeros((SUBLANES, cw), _F32)

            @pl.when(jnp.logical_not(first))
            def _():
                u_ref[0:SUBLANES, :] = u_ref[tm:tm + SUBLANES, :]

            u_ref[SUBLANES:, :] = u
            t = ((i - starts[gi]) * tm + lax.broadcasted_iota(jnp.int32, (tm, 1), 0)) % seq
            if "s0" in g:
                s0, s1 = g["s0"][...], g["s1"][...]
            else:
                s0 = s1 = jnp.zeros((tm, cw), _F32)
            u1 = jnp.where(t >= 1, u_ref[SUBLANES - 1:SUBLANES - 1 + tm, :], s1)
            u2 = jnp.where(t >= 2, u_ref[SUBLANES - 2:SUBLANES - 2 + tm, :],
                           jnp.where(t == 1, s1, s0))
            y = u * taps_ref[2:3, :] + u1 * taps_ref[1:2, :] + u2 * taps_ref[0:1, :]
            merged = (jax.nn.sigmoid(ga) * g["attn"][...].astype(_F32)
                      + jax.nn.sigmoid(gc) * (cb * y))
            g["o"][...] = merged.astype(g["o"].dtype)
            if tm >= seq:
                for s in range(tm // seq):
                    g["tail"][s] = u[(s + 1) * seq - SUBLANES:(s + 1) * seq, :]
            else:
                g["tail"][0] = u[tm - SUBLANES:, :]


def _gate_merge(xs, w_t, layer, attns, states, taps, *, col_start, tms, seq_lens, out_dtypes):
    assert CONV_WIDTH == 3
    kdim = w_t.shape[2]
    d = attns[0].shape[1]
    cw = min(d, GATE_CW)
    nj = d // cw
    ng = len(xs)
    ntiles = [x.shape[0] // tm for x, tm in zip(xs, tms)]
    starts = [int(s) for s in np.concatenate([[0], np.cumsum(ntiles)[:-1]])]
    ni = sum(ntiles)
    assert ni >= N_GATE and col_start % SUBLANES == 0 and d % cw == 0

    def tile(gi):
        return lambda i: jnp.clip(i - starts[gi], 0, ntiles[gi] - 1)

    col = lambda jj: jnp.maximum(jj - 1, 0)
    rowsel = lambda jj, i, t: jnp.where(jj > 0, t(i), 0)
    nxt = lambda jj: jnp.minimum(jj, nj - 1)
    seg = lambda i: jnp.minimum(i, N_GATE - 1)
    in_specs = [
        pl.BlockSpec((pl.Element(1), pl.Element(cw), pl.Element(kdim)),
                     lambda jj, i: (layer, pl.multiple_of(col_start + seg(i) * d + nxt(jj) * cw,
                                                          SUBLANES), 0)),
        pl.BlockSpec((CONV_WIDTH, cw), lambda jj, i: (0, col(jj)))]
    args = [w_t, taps]
    for gi in range(ng):
        tm, t = tms[gi], tile(gi)
        in_specs.append(pl.BlockSpec((tm, kdim), lambda jj, i, t=t: (t(i), 0)))
        act = pl.BlockSpec((tm, cw), lambda jj, i, t=t: (t(i), col(jj)))
        in_specs.append(act)
        args += [xs[gi], attns[gi]]
        if states[gi] is not None:
            in_specs += [act, act]
            args += list(states[gi])
    out_shapes, out_specs = [], []
    for gi in range(ng):
        tm, t, m, seq = tms[gi], tile(gi), xs[gi].shape[0], seq_lens[gi]
        spt, tps = max(tm // seq, 1), max(seq // tm, 1)
        out_shapes += [jax.ShapeDtypeStruct((m, d), out_dtypes[gi]),
                       jax.ShapeDtypeStruct((m // seq, SUBLANES, d), _F32)]
        out_specs += [
            pl.BlockSpec((tm, cw), lambda jj, i, t=t: (rowsel(jj, i, t), col(jj))),
            pl.BlockSpec((spt, SUBLANES, cw),
                         lambda jj, i, t=t, tps=tps: (rowsel(jj, i, t) // tps, 0, col(jj)))]
    scratch = [pltpu.VMEM((2, N_GATE * cw, kdim), _BF16)]
    scratch += [pltpu.VMEM((tm + SUBLANES, cw), _F32) for tm in tms]
    res = pl.pallas_call(
        functools.partial(_gm_kernel, starts=starts, ntiles=ntiles, tms=tuple(tms),
                          seq_lens=tuple(seq_lens),
                          has_state=tuple(s is not None for s in states), cw=cw),
        grid=(nj + 1, ni),
        in_specs=in_specs,
        out_specs=out_specs,
        out_shape=out_shapes,
        scratch_shapes=scratch,
        compiler_params=_cparams(("arbitrary", "arbitrary")),
        name="gate_merge",
    )(*args)
    return [(res[2 * gi], res[2 * gi + 1]) for gi in range(ng)]


def _rope_tables(positions, half, n_rot_lanes=LANES):
    pos = np.asarray(positions, np.float64)[:, None]
    lane = np.arange(LANES)
    inv = ROPE_THETA ** (-(lane % half).astype(np.float64) / half)
    ang = pos * inv[None, :]
    sign = np.where((lane % (2 * half)) < half, -1.0, 1.0)
    rot = (lane < n_rot_lanes)[None, :]
    cos = np.where(rot, np.cos(ang), 1.0)
    sin = np.where(rot, np.sin(ang) * sign[None, :], 0.0)
    return jnp.asarray(cos, _F32), jnp.asarray(sin, _F32)


def kernel(x_prompt, x_sample, cache_k, cache_v, cache_kidx, state_conv, page_table,
           w_in, conv_w, w_o, g_mix, g_mlp, w_up, w_down, g_final):
    B, S, D = x_prompt.shape
    Bd, T, _ = x_sample.shape
    depth = w_in.shape[0]
    n_pool = cache_k.shape[1]
    past = page_table.shape[1] * PAGE_SIZE
    R = SUBLANES
    kvw = N_KV_HEADS * HEAD_DIM
    half, ihalf = HEAD_DIM // 2, IDX_DIM // 2
    Mp, Ms = B * S, Bd * R
    tms = [min(Mp, 1024, S // 2), Ms]
    tms_down = [min(Mp, 2048), Ms]

    c_q, c_k = 0, N_HEADS * HEAD_DIM
    c_v = c_k + kvw
    c_qi = c_v + kvw
    c_kiwi = c_qi + IDX_HEADS * IDX_DIM
    c_gate = c_kiwi + IDX_DIM + IDX_HEADS

    pos_p = np.arange(max(S, tms[0])) % S
    pos_s = np.tile(past + np.arange(R), Bd)
    r128 = [_rope_tables(pos_p, half), _rope_tables(pos_s, half)]
    r64 = [_rope_tables(pos_p, ihalf), _rope_tables(pos_s, ihalf)]
    rkiwi = [_rope_tables(pos_p, ihalf, IDX_DIM), _rope_tables(pos_s, ihalf, IDX_DIM)]

    xp = x_prompt.reshape(Mp, D)
    xs = jnp.pad(x_sample, ((0, 0), (0, R - T), (0, 0))).reshape(Ms, D)
    w_in_t = jnp.swapaxes(w_in, 1, 2)
    pool_k = cache_k.reshape(depth * n_pool * PAGE_SIZE * N_KV_HEADS, HEAD_DIM)
    pool_v = cache_v.reshape(depth * n_pool * PAGE_SIZE * N_KV_HEADS, HEAD_DIM)
    pool_ki_t = jnp.swapaxes(cache_kidx, 2, 3).reshape(depth * n_pool, IDX_DIM, PAGE_SIZE)

    outs = {n: [] for n in ("kp", "vp", "kip", "cp", "ks", "vs", "kis", "cs")}
    for l in range(depth):
        hs = [_rmsnorm(xp, g_mix[l], _BF16), _rmsnorm(xs, g_mix[l], _BF16)]
        mm = functools.partial(_matmul, hs, w_in_t, l, tms=tms, w_t=True)
        qp, qs = mm(col_start=c_q, n_cols=N_HEADS * HEAD_DIM, out_dtype=_BF16,
                    rope=(half, r128), scale=HEAD_DIM ** -0.5)
        kp, ks = mm(col_start=c_k, n_cols=kvw, out_dtype=_F32, rope=(half, r128))
        vp, vs = mm(col_start=c_v, n_cols=kvw, out_dtype=_F32)
        qip, qis = mm(col_start=c_qi, n_cols=IDX_HEADS * IDX_DIM, out_dtype=_BF16,
                      rope=(ihalf, r64), pair_major=[True, False])
        kwp, kws = mm(col_start=c_kiwi, n_cols=LANES, out_dtype=_F32, rope=(ihalf, rkiwi))
        page0 = l * n_pool

        kip = kwp[:, :IDX_DIM]
        wit = kwp[:, IDX_DIM:IDX_DIM + IDX_HEADS].T
        attn_p = _prompt_attention(qip, wit, kip, qp, kp, vp, batch=B, seq=S)

        qi_hm = qis.reshape(Bd, R, IDX_HEADS, IDX_DIM).transpose(0, 2, 1, 3).reshape(
            Bd, IDX_HEADS * R, IDX_DIM)
        kis = kws[:, :IDX_DIM]
        w_hm = kws[:, IDX_DIM:IDX_DIM + IDX_HEADS].reshape(Bd, R, IDX_HEADS).transpose(
            0, 2, 1).reshape(Bd, IDX_HEADS * R, 1)
        bias = _sample_index(page_table, qi_hm, w_hm, kis, pool_ki_t, page0, 0, t_real=T)
        attn_s = _sample_attention(page_table, qs.astype(_F32), bias, ks, vs, pool_k, pool_v,
                                   page0)

        s0, s1 = [jnp.repeat(state_conv[l][:, r:r + 1, :], R, axis=1).reshape(Ms, D)
                  for r in range(CONV_WIDTH - 1)]
        (mg_p, tail_p), (mg_s, tail_s) = _gate_merge(
            hs, w_in_t, l, [attn_p, attn_s], [None, (s0, s1)], conv_w[l], col_start=c_gate,
            tms=tms, seq_lens=[S, R], out_dtypes=[_BF16, _F32])
        xp, xs = _matmul([mg_p, mg_s.astype(_BF16)], w_o, l, col_start=0, n_cols=D,
                         out_dtype=_F32, tms=tms, epi="residual", residuals=[xp, xs])
        h2 = [_rmsnorm(xp, g_mlp[l], _BF16), _rmsnorm(xs, g_mlp[l], _BF16)]
        hid = _matmul(h2, w_up, l, col_start=0, n_cols=w_up.shape[2], out_dtype=_BF16, tms=tms,
                      epi="relu2")
        xp, xs = _matmul(hid, _cast_bf16(w_down, l), 0, col_start=0, n_cols=D, out_dtype=_F32,
                         tms=tms_down, epi="residual", residuals=[xp, xs], tn=min(D, 1024),
                         tk=min(w_down.shape[1], 2048))

        outs["kp"].append(kp.reshape(B, S, N_KV_HEADS, HEAD_DIM))
        outs["vp"].append(vp.reshape(B, S, N_KV_HEADS, HEAD_DIM))
        outs["kip"].append(kip.reshape(B, S, IDX_DIM))
        outs["cp"].append(tail_p[:, R - (CONV_WIDTH - 1):, :])
        outs["ks"].append(ks.reshape(Bd, R, N_KV_HEADS, HEAD_DIM)[:, :T])
        outs["vs"].append(vs.reshape(Bd, R, N_KV_HEADS, HEAD_DIM)[:, :T])
        outs["kis"].append(kis.reshape(Bd, R, IDX_DIM)[:, :T])
        outs["cs"].append(tail_s[:, T - (CONV_WIDTH - 1):T, :])

    y_prompt = _rmsnorm(xp, g_final, _F32).reshape(B, S, D)
    y_sample = _rmsnorm(xs, g_final, _F32).reshape(Bd, R, D)[:, :T]
    st = lambda n: jnp.stack(outs[n])
    return (y_prompt, y_sample, st("kp"), st("vp"), st("kip"), st("cp"),
            st("ks"), st("vs"), st("kis"), st("cs"))
```

```python
import functools

import numpy as np
import jax
import jax.numpy as jnp
from jax import lax
from jax.experimental import pallas as pl
from jax.experimental.pallas import tpu as pltpu

HEAD_DIM = 128
N_HEADS = 32
N_KV_HEADS = 8
IDX_HEADS = 32
IDX_DIM = 64
TOPK_MAX = 256
CONV_WIDTH = 3
PAGE_SIZE = 128
ROPE_THETA = 10000.0
NORM_EPS = 1e-6
NEG = -1e30

LANES = 128
SUBLANES = 8
VMEM_LIMIT = 60 * 1024 * 1024
PAGES_PER_STEP = 16
INT_MIN = -2 ** 31
TN = 1024
SUB_TN = 512
SUB_TM = 512
GATE_CW = 256

_F32 = jnp.float32
_BF16 = jnp.bfloat16
_NT = (((1,), (1,)), ((), ()))


def _cparams(sem):
    return pltpu.CompilerParams(dimension_semantics=sem, vmem_limit_bytes=VMEM_LIMIT)


def _rmsnorm_kernel(x_ref, g_ref, o_ref):
    x = x_ref[...]
    y = x * lax.rsqrt(jnp.mean(x * x, axis=-1, keepdims=True) + NORM_EPS)
    o_ref[...] = (y * g_ref[...]).astype(o_ref.dtype)


def _rmsnorm(x, g, out_dtype):
    m, d = x.shape
    tm = min(m, 256)
    return pl.pallas_call(
        _rmsnorm_kernel,
        grid=(m // tm,),
        in_specs=[pl.BlockSpec((tm, d), lambda i: (i, 0)),
                  pl.BlockSpec((1, d), lambda i: (0, 0))],
        out_specs=pl.BlockSpec((tm, d), lambda i: (i, 0)),
        out_shape=jax.ShapeDtypeStruct((m, d), out_dtype),
        compiler_params=_cparams(("arbitrary",)),
        name="rmsnorm",
    )(x, g.reshape(1, d))


def _cast_kernel(w_ref, o_ref):
    o_ref[...] = w_ref[...].astype(o_ref.dtype)


def _cast_bf16(w, layer):
    _, kdim, n = w.shape
    tr = min(kdim, 512)
    return pl.pallas_call(
        _cast_kernel,
        grid=(kdim // tr,),
        in_specs=[pl.BlockSpec((1, tr, n), lambda i: (layer, i, 0))],
        out_specs=pl.BlockSpec((1, tr, n), lambda i: (0, i, 0)),
        out_shape=jax.ShapeDtypeStruct((1, kdim, n), _BF16),
        compiler_params=_cparams(("arbitrary",)),
        name="cast_bf16",
    )(w)


def _rope_slab(z, cos, sin, half):
    if 2 * half == LANES:
        partner = pltpu.roll(z, half, 1)
    else:
        lane = lax.broadcasted_iota(jnp.int32, z.shape, 1)
        first = (lane % (2 * half)) < half
        partner = jnp.where(first, pltpu.roll(z, LANES - half, 1), pltpu.roll(z, half, 1))
    return z * cos + partner * sin


def _epilogue(g, z, c0, *, epi, rope_half, scale):
    o_ref = g["o"]
    width = z.shape[1]
    if rope_half:
        cos, sin = g["cos"][...], g["sin"][...]
        for s in range(width // LANES):
            slab = _rope_slab(z[:, s * LANES:(s + 1) * LANES], cos, sin, rope_half)
            if scale != 1.0:
                slab = slab * scale
            if g["pm"]:
                o_ref[c0 // LANES + s] = slab.astype(o_ref.dtype)
            else:
                o_ref[:, c0 + s * LANES:c0 + (s + 1) * LANES] = slab.astype(o_ref.dtype)
        return
    if epi == "relu2":
        r = jnp.maximum(z, 0.0)
        z = r * r
    elif epi == "residual":
        z = g["res"][:, c0:c0 + width] + z
    o_ref[:, c0:c0 + width] = z.astype(o_ref.dtype)


def _unpack_groups(it, ng, rope_half, epi, pair_major):
    groups = []
    for _ in range(ng):
        g = dict(x=next(it))
        if rope_half:
            g["cos"], g["sin"] = next(it), next(it)
        if epi == "residual":
            g["res"] = next(it)
        groups.append(g)
    for gi, g in enumerate(groups):
        g["o"] = next(it)
        g["pm"] = pair_major[gi]
    return groups


def _mm_ws_kernel(*refs, starts, ntiles, npieces, epi, rope_half, scale, pair_major, w_t):
    it = iter(refs)
    wp_ref = next(it)
    groups = _unpack_groups(it, len(starts), rope_half, epi, pair_major)
    wb_ref = next(it)
    jj = pl.program_id(0)
    i = pl.program_id(1)
    nj = pl.num_programs(0) - 1

    @pl.when((jj < nj) & (i < npieces))
    def _():
        piece = wp_ref[0].astype(_BF16)
        rows = piece.shape[0]
        wb_ref[jj % 2, pl.ds(pl.multiple_of(i * rows, rows), rows), :] = piece

    for gi, g in enumerate(groups):
        active = (jj > 0) & (i >= starts[gi]) & (i < starts[gi] + ntiles[gi])

        @pl.when(active)
        def _(g=g):
            slot = (jj + 1) % 2
            tn = wb_ref.shape[1] if w_t else wb_ref.shape[2]
            sub = min(tn, SUB_TN)
            for c0 in range(0, tn, sub):
                if w_t:
                    z = lax.dot_general(g["x"][...], wb_ref[slot, c0:c0 + sub, :], _NT,
                                        preferred_element_type=_F32)
                else:
                    z = jnp.dot(g["x"][...], wb_ref[slot, :, c0:c0 + sub],
                                preferred_element_type=_F32)
                _epilogue(g, z, c0, epi=epi, rope_half=rope_half, scale=scale)


def _mm_kt_kernel(*refs, starts, ntiles):
    it = iter(refs)
    w_ref = next(it)
    groups = _unpack_groups(it, len(starts), 0, "residual", [False] * len(starts))
    i = pl.program_id(1)
    k = pl.program_id(2)
    for gi, g in enumerate(groups):
        active = (i >= starts[gi]) & (i < starts[gi] + ntiles[gi])

        @pl.when(active)
        def _(g=g):
            @pl.when(k == 0)
            def _():
                g["o"][...] = g["res"][...]

            tm = g["x"].shape[0]
            sub = min(tm, SUB_TM)
            for r0 in range(0, tm, sub):
                g["o"][r0:r0 + sub, :] += jnp.dot(g["x"][r0:r0 + sub, :], w_ref[0].astype(_BF16),
                                                   preferred_element_type=_F32)


def _matmul(xs, w, layer, *, col_start, n_cols, out_dtype, tms, epi="none", rope=None,
            scale=1.0, residuals=None, pair_major=None, tn=TN, tk=None, w_t=False):
    kdim = w.shape[2] if w_t else w.shape[1]
    tn = min(tn, n_cols)
    nj = n_cols // tn
    cb0 = col_start // tn
    assert n_cols % tn == 0
    assert col_start % (SUBLANES if w_t else tn) == 0
    ng = len(xs)
    pair_major = [False] * ng if pair_major is None else pair_major
    ntiles = [x.shape[0] // tm for x, tm in zip(xs, tms)]
    starts = [int(s) for s in np.concatenate([[0], np.cumsum(ntiles)[:-1]])]
    ni = sum(ntiles)
    ws = tk is None
    if ws:
        tk = kdim
        npieces = max(p for p in (8, 4, 2, 1) if p <= ni)
    else:
        assert epi == "residual" and out_dtype == _F32 and not w_t and kdim % tk == 0
        assert rope is None and not any(pair_major)

    def tile(gi):
        return lambda i: jnp.clip(i - starts[gi], 0, ntiles[gi] - 1)

    if ws:
        col = lambda a: jnp.maximum(a[0] - 1, 0)
        krow = lambda a: 0
        rowsel = lambda a, t: jnp.where(a[0] > 0, t(a[1]), 0)
        nxt = lambda a: jnp.minimum(a[0], nj - 1)
        pc = lambda a: jnp.minimum(a[1], npieces - 1)
        if w_t:
            pr = tn // npieces
            w_spec = pl.BlockSpec(
                (pl.Element(1), pl.Element(pr), pl.Element(tk)),
                lambda *a: (layer, pl.multiple_of(col_start + nxt(a) * tn + pc(a) * pr, SUBLANES),
                            0))
        else:
            w_spec = pl.BlockSpec((1, tk // npieces, tn), lambda *a: (layer, pc(a), cb0 + nxt(a)))
    else:
        col = lambda a: a[0]
        krow = lambda a: a[2]
        rowsel = lambda a, t: t(a[1])
        w_spec = pl.BlockSpec((1, tk, tn), lambda *a: (layer, a[2], cb0 + a[0]))

    in_specs, args = [w_spec], [w]
    rope_half = 0
    for gi in range(ng):
        tm, t = tms[gi], tile(gi)
        in_specs.append(pl.BlockSpec((tm, tk), lambda *a, t=t: (t(a[1]), krow(a))))
        args.append(xs[gi])
        if rope is not None:
            rope_half = rope[0]
            cos, sin = rope[1][gi]
            nper = cos.shape[0] // tm
            tspec = pl.BlockSpec((tm, LANES), lambda *a, t=t, nper=nper: (t(a[1]) % nper, 0))
            in_specs += [tspec, tspec]
            args += [cos, sin]
        if epi == "residual":
            mode = {} if ws else dict(pipeline_mode=pl.Buffered(1))
            in_specs.append(pl.BlockSpec((tm, tn), lambda *a, t=t: (t(a[1]), col(a)), **mode))
            args.append(residuals[gi])
    out_shapes, out_specs = [], []
    for gi in range(ng):
        tm, t, m = tms[gi], tile(gi), xs[gi].shape[0]
        if pair_major[gi]:
            out_shapes.append(jax.ShapeDtypeStruct((n_cols // LANES, m, LANES), out_dtype))
            out_specs.append(pl.BlockSpec((tn // LANES, tm, LANES),
                                          lambda *a, t=t: (col(a), rowsel(a, t), 0)))
        else:
            out_shapes.append(jax.ShapeDtypeStruct((m, n_cols), out_dtype))
            out_specs.append(pl.BlockSpec((tm, tn), lambda *a, t=t: (rowsel(a, t), col(a))))
    if ws:
        body = functools.partial(_mm_ws_kernel, starts=starts, ntiles=ntiles, npieces=npieces,
                                 epi=epi, rope_half=rope_half, scale=scale,
                                 pair_major=tuple(pair_major), w_t=w_t)
        grid = (nj + 1, ni)
        scratch = [pltpu.VMEM((2, tn, tk) if w_t else (2, tk, tn), _BF16)]
    else:
        body = functools.partial(_mm_kt_kernel, starts=starts, ntiles=ntiles)
        grid = (nj, ni, kdim // tk)
        scratch = []
    return pl.pallas_call(
        body,
        grid=grid,
        in_specs=in_specs,
        out_specs=out_specs,
        out_shape=out_shapes,
        scratch_shapes=scratch,
        compiler_params=_cparams(("arbitrary",) * len(grid)),
        name="matmul_" + epi + ("_rope%d" % rope_half if rope_half else ""),
    )(*args)


def _sortable(x):
    bits = pltpu.bitcast(x, jnp.int32)
    return jnp.where(bits < 0, bits ^ jnp.int32(0x7FFFFFFF), bits)


def _kth_largest(keys, n_sel, count_fn):
    cnt = count_fn((keys >= 0).astype(jnp.int32))
    thr0 = jnp.where(cnt >= n_sel, jnp.int32(0), jnp.int32(INT_MIN))

    def body(i, thr):
        cand = thr | jnp.left_shift(jnp.int32(1), 30 - i)
        cnt = count_fn((keys >= cand).astype(jnp.int32))
        return jnp.where(cnt >= n_sel, cand, thr)

    return lax.fori_loop(0, 31, body, thr0)


def _select_topk(keys, idx, n_sel, count_fn, idx_bits):
    thr = _kth_largest(keys, n_sel, count_fn)
    gt = keys > thr
    eq = keys == thr
    need = n_sel - count_fn(gt.astype(jnp.int32))
    surplus = count_fn(eq.astype(jnp.int32)) - need

    def tie_cut():
        def body(b, lo):
            cand = lo + jnp.left_shift(jnp.int32(1), idx_bits - 1 - b)
            below = count_fn((eq & (idx < cand)).astype(jnp.int32))
            return jnp.where(below < need, cand, lo)

        return lax.fori_loop(0, idx_bits, body, jnp.zeros_like(thr))

    cut = lax.cond(jnp.max(surplus) > 0, tie_cut,
                   lambda: jnp.full_like(thr, jnp.iinfo(jnp.int32).max))
    return gt | (eq & (idx <= cut))


def _pattn_kernel(qi_ref, wit_ref, ki_ref, q_ref, k_ref, v_ref, o_ref, sc_ref, bias_ref, *,
                  n_sel, kv_rep):
    jq = pl.program_id(1)
    g = pl.program_id(2)
    s_len, tq = sc_ref.shape
    n_pairs = qi_ref.shape[0]

    def select(klen):
        kb = ki_ref[0:klen, :].astype(_BF16)
        zeros = jnp.zeros_like(kb)
        k_even = jnp.concatenate([kb, zeros], axis=1)
        k_odd = jnp.concatenate([zeros, kb], axis=1)
        w_scale = (IDX_HEADS ** -0.5) * (IDX_DIM ** -0.5)
        sc_ref[0:klen, :] = jnp.zeros((klen, tq), _F32)

        def body(p, carry):
            x = qi_ref[p]
            d0 = lax.dot_general(k_even, x, _NT, preferred_element_type=_F32)
            d1 = lax.dot_general(k_odd, x, _NT, preferred_element_type=_F32)
            w0 = wit_ref[pl.ds(2 * p, 1), :] * w_scale
            w1 = wit_ref[pl.ds(2 * p + 1, 1), :] * w_scale
            sc_ref[0:klen, :] += jnp.maximum(d0, 0.0) * w0 + jnp.maximum(d1, 0.0) * w1
            return carry

        lax.fori_loop(0, n_pairs, body, 0)

        key_pos = lax.broadcasted_iota(jnp.int32, (klen, tq), 0)
        q_pos = jq * tq + lax.broadcasted_iota(jnp.int32, (klen, tq), 1)
        adm = key_pos <= q_pos
        keys = _sortable(jnp.where(adm, sc_ref[0:klen, :], NEG))
        sel = _select_topk(keys, key_pos, n_sel, lambda c: jnp.sum(c, axis=0, keepdims=True),
                           (s_len - 1).bit_length()) & adm
        bias_ref[:, 0:klen] = jnp.where(sel, 0.0, NEG).astype(_F32).T

    def attend(klen):
        kg = k_ref[0:klen, :].astype(_BF16)
        vg = v_ref[0:klen, :].astype(_BF16)
        bias = bias_ref[:, 0:klen]
        for r in range(kv_rep):
            qh = q_ref[:, r * HEAD_DIM:(r + 1) * HEAD_DIM]
            s = lax.dot_general(qh, kg, _NT, preferred_element_type=_F32) + bias
            m = jnp.max(s, axis=1, keepdims=True)
            p = jnp.exp(s - m)
            l = jnp.sum(p, axis=1, keepdims=True)
            o = jnp.dot(p.astype(_BF16), vg, preferred_element_type=_F32) / l
            o_ref[:, r * HEAD_DIM:(r + 1) * HEAD_DIM] = o.astype(o_ref.dtype)

    for n in range(1, s_len // tq + 1):
        @pl.when(jq == n - 1)
        def _(klen=n * tq):
            @pl.when(g == 0)
            def _():
                select(klen)

            attend(klen)


def _prompt_attention(qi_pm, wit, ki, q, k, v, *, batch, seq):
    m, d = q.shape
    n_kv = k.shape[1] // HEAD_DIM
    kv_rep = (d // HEAD_DIM) // n_kv
    tq = min(seq, 256)
    nq = seq // tq
    n_sel = min(TOPK_MAX, seq // 4)
    n_pairs = qi_pm.shape[0]
    gw = kv_rep * HEAD_DIM
    return pl.pallas_call(
        functools.partial(_pattn_kernel, n_sel=n_sel, kv_rep=kv_rep),
        grid=(batch, nq, n_kv),
        in_specs=[
            pl.BlockSpec((n_pairs, tq, LANES), lambda b, j, g: (0, b * nq + j, 0)),
            pl.BlockSpec((wit.shape[0], tq), lambda b, j, g: (0, b * nq + j)),
            pl.BlockSpec((seq, ki.shape[1]), lambda b, j, g: (b, 0)),
            pl.BlockSpec((tq, gw), lambda b, j, g: (b * nq + j, g)),
            pl.BlockSpec((seq, HEAD_DIM), lambda b, j, g: (b, g)),
            pl.BlockSpec((seq, HEAD_DIM), lambda b, j, g: (b, g)),
        ],
        out_specs=pl.BlockSpec((tq, gw), lambda b, j, g: (b * nq + j, g)),
        out_shape=jax.ShapeDtypeStruct((m, d), _BF16),
        scratch_shapes=[pltpu.VMEM((seq, tq), _F32), pltpu.VMEM((tq, seq), _F32)],
        compiler_params=_cparams(("arbitrary", "arbitrary", "arbitrary")),
        name="prompt_attention",
    )(qi_pm, wit, ki, q, k, v)


def _sidx_kernel(pt_ref, qi_ref, w_ref, kin_ref, *rest, n_sel, t_real, n_chunks):
    pages = rest[:PAGES_PER_STEP]
    bias_ref, sc_ref = rest[PAGES_PER_STEP], rest[PAGES_PER_STEP + 1]
    c = pl.program_id(1)
    rows = SUBLANES
    chunk = PAGES_PER_STEP * PAGE_SIZE
    w = w_ref[0] * ((IDX_HEADS ** -0.5) * (IDX_DIM ** -0.5))
    qi = qi_ref[0]

    def scores(d):
        d = jnp.maximum(d, 0.0) * w
        acc = d[0:rows]
        for h in range(1, IDX_HEADS):
            acc = acc + d[h * rows:(h + 1) * rows]
        return acc

    @pl.when(c < n_chunks)
    def _():
        keys_t = jnp.concatenate([p[0] for p in pages], axis=1).astype(_BF16)
        sc_ref[c] = scores(jnp.dot(qi, keys_t, preferred_element_type=_F32))

    @pl.when(c == n_chunks)
    def _():
        kn = kin_ref[...].astype(_BF16)
        kn = jnp.concatenate([kn, jnp.zeros((PAGE_SIZE - rows, kn.shape[1]), _BF16)], axis=0)
        s_new = scores(lax.dot_general(qi, kn, _NT, preferred_element_type=_F32))
        s_new = jnp.concatenate([s_new, jnp.zeros((rows, chunk - PAGE_SIZE), _F32)], axis=1)
        shape = (n_chunks + 1, rows, chunk)
        ci = lax.broadcasted_iota(jnp.int32, shape, 0)
        qrow = lax.broadcasted_iota(jnp.int32, shape, 1)
        lane = lax.broadcasted_iota(jnp.int32, shape, 2)
        adm = (ci < n_chunks) | ((lane <= qrow) & (lane < t_real))
        sc_ref[n_chunks] = s_new
        keys = _sortable(jnp.where(adm, sc_ref[...], NEG))
        count = lambda x: jnp.sum(jnp.sum(x, axis=0, keepdims=True), axis=2, keepdims=True)
        key_idx = ci * chunk + lane
        n_keys = (n_chunks + 1) * chunk
        sel = _select_topk(keys, key_idx, n_sel, count, (n_keys - 1).bit_length()) & adm
        bias_ref[0] = jnp.where(sel, 0.0, NEG).astype(_F32)


def _page_spec(block, n_chunks, page0, i):
    zeros = (0,) * (len(block) - 1)
    return pl.BlockSpec(
        block,
        lambda b, c, pt: (page0 + pt[b, jnp.minimum(c, n_chunks - 1) * PAGES_PER_STEP + i],)
        + zeros)


def _sample_index(page_table, qi_hm, w_hm, ki_new, pool_ki, page0, row0, *, t_real):
    bd, n_pages = page_table.shape
    n_chunks = n_pages // PAGES_PER_STEP
    chunk = PAGES_PER_STEP * PAGE_SIZE
    past = n_pages * PAGE_SIZE
    n_sel = min(TOPK_MAX, (past + t_real) // 4)
    hr = qi_hm.shape[1]
    grid_spec = pltpu.PrefetchScalarGridSpec(
        num_scalar_prefetch=1,
        grid=(bd, n_chunks + 1),
        in_specs=[pl.BlockSpec((1, hr, IDX_DIM), lambda b, c, pt: (b, 0, 0)),
                  pl.BlockSpec((1, hr, 1), lambda b, c, pt: (b, 0, 0)),
                  pl.BlockSpec((SUBLANES, IDX_DIM), lambda b, c, pt: (row0 + b, 0))]
                 + [_page_spec((1, IDX_DIM, PAGE_SIZE), n_chunks, page0, i)
                    for i in range(PAGES_PER_STEP)],
        out_specs=pl.BlockSpec((1, n_chunks + 1, SUBLANES, chunk), lambda b, c, pt: (b, 0, 0, 0)),
        scratch_shapes=[pltpu.VMEM((n_chunks + 1, SUBLANES, chunk), _F32)],
    )
    return pl.pallas_call(
        functools.partial(_sidx_kernel, n_sel=n_sel, t_real=t_real, n_chunks=n_chunks),
        grid_spec=grid_spec,
        out_shape=jax.ShapeDtypeStruct((bd, n_chunks + 1, SUBLANES, chunk), _F32),
        compiler_params=_cparams(("arbitrary", "arbitrary")),
        name="sample_index",
    )(page_table, qi_hm, w_hm, ki_new, *([pool_ki] * PAGES_PER_STEP))


def _sattn_kernel(pt_ref, q_ref, bias_ref, kn_ref, vn_ref, *rest, n_chunks, n_kv, kv_rep):
    kpages = rest[:PAGES_PER_STEP]
    vpages = rest[PAGES_PER_STEP:2 * PAGES_PER_STEP]
    o_ref, m_ref, l_ref, acc_ref = rest[2 * PAGES_PER_STEP:]
    c = pl.program_id(1)
    rows = SUBLANES

    @pl.when(c == 0)
    def _():
        m_ref[...] = jnp.full_like(m_ref, -jnp.inf)
        l_ref[...] = jnp.zeros_like(l_ref)
        acc_ref[...] = jnp.zeros_like(acc_ref)

    gr = kv_rep * rows

    def update(kgs, vgs, bias):
        parts = []
        for g in range(n_kv):
            qg = jnp.concatenate(
                [q_ref[:, (g * kv_rep + r) * HEAD_DIM:(g * kv_rep + r + 1) * HEAD_DIM]
                 for r in range(kv_rep)], axis=0).astype(_BF16)
            parts.append(lax.dot_general(qg, kgs[g], _NT, preferred_element_type=_F32))
        s = jnp.concatenate(parts, axis=0) + jnp.concatenate([bias] * (n_kv * kv_rep), axis=0)
        m_old = m_ref[...]
        m_new = jnp.maximum(m_old, jnp.max(s, axis=1, keepdims=True))
        alpha = jnp.exp(m_old - m_new)
        p = jnp.exp(s - m_new)
        l_ref[...] = alpha * l_ref[...] + jnp.sum(p, axis=1, keepdims=True)
        pb = p.astype(_BF16)
        pv = jnp.concatenate(
            [jnp.dot(pb[g * gr:(g + 1) * gr], vgs[g], preferred_element_type=_F32)
             for g in range(n_kv)], axis=0)
        acc_ref[...] = alpha * acc_ref[...] + pv
        m_ref[...] = m_new

    @pl.when(c < n_chunks)
    def _():
        def head_rows(pages, g):
            rows_g = pl.ds(g, PAGE_SIZE, stride=n_kv)
            return jnp.concatenate([p[rows_g, :] for p in pages], axis=0).astype(_BF16)

        update([head_rows(kpages, g) for g in range(n_kv)],
               [head_rows(vpages, g) for g in range(n_kv)], bias_ref[0, 0])

    @pl.when(c == n_chunks)
    def _():
        pad = jnp.zeros((PAGE_SIZE - rows, HEAD_DIM), _BF16)

        def new_rows(ref, g):
            return jnp.concatenate(
                [ref[:, g * HEAD_DIM:(g + 1) * HEAD_DIM].astype(_BF16), pad], axis=0)

        update([new_rows(kn_ref, g) for g in range(n_kv)],
               [new_rows(vn_ref, g) for g in range(n_kv)], bias_ref[0, 0][:, :PAGE_SIZE])
        o = acc_ref[...] / l_ref[...]
        for h in range(n_kv * kv_rep):
            o_ref[:, h * HEAD_DIM:(h + 1) * HEAD_DIM] = (
                o[h * rows:(h + 1) * rows].astype(o_ref.dtype))


def _sample_attention(page_table, q, bias, k_new, v_new, pool_k, pool_v, page0):
    bd, n_pages = page_table.shape
    n_chunks = n_pages // PAGES_PER_STEP
    chunk = PAGES_PER_STEP * PAGE_SIZE
    m, d = q.shape
    kvw = k_new.shape[1]
    n_kv = kvw // HEAD_DIM
    kv_rep = (d // HEAD_DIM) // n_kv
    grid_spec = pltpu.PrefetchScalarGridSpec(
        num_scalar_prefetch=1,
        grid=(bd, n_chunks + 1),
        in_specs=[pl.BlockSpec((SUBLANES, d), lambda b, c, pt: (b, 0)),
                  pl.BlockSpec((1, 1, SUBLANES, chunk), lambda b, c, pt: (b, c, 0, 0)),
                  pl.BlockSpec((SUBLANES, kvw), lambda b, c, pt: (b, 0)),
                  pl.BlockSpec((SUBLANES, kvw), lambda b, c, pt: (b, 0))]
                 + [_page_spec((PAGE_SIZE * n_kv, HEAD_DIM), n_chunks, page0, i)
                    for i in range(PAGES_PER_STEP)] * 2,
        out_specs=pl.BlockSpec((SUBLANES, d), lambda b, c, pt: (b, 0)),
        scratch_shapes=[pltpu.VMEM((n_kv * kv_rep * SUBLANES, 1), _F32),
                        pltpu.VMEM((n_kv * kv_rep * SUBLANES, 1), _F32),
                        pltpu.VMEM((n_kv * kv_rep * SUBLANES, HEAD_DIM), _F32)],
    )
    return pl.pallas_call(
        functools.partial(_sattn_kernel, n_chunks=n_chunks, n_kv=n_kv, kv_rep=kv_rep),
        grid_spec=grid_spec,
        out_shape=jax.ShapeDtypeStruct((m, d), _F32),
        compiler_params=_cparams(("arbitrary", "arbitrary")),
        name="sample_attention",
    )(page_table, q, bias, k_new, v_new,
      *([pool_k] * PAGES_PER_STEP), *([pool_v] * PAGES_PER_STEP))


N_GATE = 5


def _gm_kernel(*refs, starts, ntiles, tms, seq_lens, has_state, cw):
    ng = len(starts)
    it = iter(refs)
    wp_ref, taps_ref = next(it), next(it)
    groups = []
    for gi in range(ng):
        g = dict(x=next(it), attn=next(it))
        if has_state[gi]:
            g["s0"], g["s1"] = next(it), next(it)
        groups.append(g)
    for g in groups:
        g["o"], g["tail"] = next(it), next(it)
    wb_ref = next(it)
    for g in groups:
        g["u"] = next(it)
    jj = pl.program_id(0)
    i = pl.program_id(1)
    nj = pl.num_programs(0) - 1

    @pl.when((jj < nj) & (i < N_GATE))
    def _():
        wb_ref[jj % 2, pl.ds(pl.multiple_of(i * cw, cw), cw), :] = wp_ref[0].astype(_BF16)

    for gi, g in enumerate(groups):
        tm, seq = tms[gi], seq_lens[gi]
        active = (jj > 0) & (i >= starts[gi]) & (i < starts[gi] + ntiles[gi])

        @pl.when(active)
        def _(g=g, gi=gi, tm=tm, seq=seq):
            u_ref = g["u"]
            first = i == starts[gi]

            @pl.when(first)
            def _():
                u_ref[0:SUBLANES, :] = jnp.zeros((SUBLANES, cw), _F32)

            @pl.when(jnp.logical_not(first))
            def _():
                u_ref[0:SUBLANES, :] = u_ref[tm:tm + SUBLANES, :]

            sub = min(tm, SUB_TM)
            for r0 in range(0, tm, sub):
                rs = slice(r0, r0 + sub)
                z = lax.dot_general(g["x"][rs, :], wb_ref[(jj + 1) % 2], _NT,
                                    preferred_element_type=_F32)
                cb, cc, ch, ga, gc = [z[:, s * cw:(s + 1) * cw] for s in range(N_GATE)]
                u = cc * ch
                u_ref[SUBLANES + r0:SUBLANES + r0 + sub, :] = u
                t = ((i - starts[gi]) * tm + r0
                     + lax.broadcasted_iota(jnp.int32, (sub, 1), 0)) % seq
                if "s0" in g:
                    s0, s1 = g["s0"][rs, :], g["s1"][rs, :]
                else:
                    s0 = s1 = jnp.zeros((sub, cw), _F32)
                h1 = SUBLANES - 1 + r0
                u1 = jnp.where(t >= 1, u_ref[h1:h1 + sub, :], s1)
                u2 = jnp.where(t >= 2, u_ref[h1 - 1:h1 - 1 + sub, :], jnp.where(t == 1, s1, s0))
                y = u * taps_ref[2:3, :] + u1 * taps_ref[1:2, :] + u2 * taps_ref[0:1, :]
                merged = (jax.nn.sigmoid(ga) * g["attn"][rs, :].astype(_F32)
                          + jax.nn.sigmoid(gc) * (cb * y))
                g["o"][rs, :] = merged.astype(g["o"].dtype)
            if tm >= seq:
                for s in range(tm // seq):
                    g["tail"][s] = u_ref[(s + 1) * seq:(s + 1) * seq + SUBLANES, :]
            else:
                g["tail"][0] = u_ref[tm:tm + SUBLANES, :]


def _gate_merge(xs, w_t, layer, attns, states, taps, *, col_start, tms, seq_lens, out_dtypes):
    assert CONV_WIDTH == 3
    kdim = w_t.shape[2]
    d = attns[0].shape[1]
    cw = min(d, GATE_CW)
    nj = d // cw
    ng = len(xs)
    ntiles = [x.shape[0] // tm for x, tm in zip(xs, tms)]
    starts = [int(s) for s in np.concatenate([[0], np.cumsum(ntiles)[:-1]])]
    ni = sum(ntiles)
    assert ni >= N_GATE and col_start % SUBLANES == 0 and d % cw == 0

    def tile(gi):
        return lambda i: jnp.clip(i - starts[gi], 0, ntiles[gi] - 1)

    col = lambda jj: jnp.maximum(jj - 1, 0)
    rowsel = lambda jj, i, t: jnp.where(jj > 0, t(i), 0)
    nxt = lambda jj: jnp.minimum(jj, nj - 1)
    seg = lambda i: jnp.minimum(i, N_GATE - 1)
    in_specs = [
        pl.BlockSpec((pl.Element(1), pl.Element(cw), pl.Element(kdim)),
                     lambda jj, i: (layer, pl.multiple_of(col_start + seg(i) * d + nxt(jj) * cw,
                                                          SUBLANES), 0)),
        pl.BlockSpec((CONV_WIDTH, cw), lambda jj, i: (0, col(jj)))]
    args = [w_t, taps]
    for gi in range(ng):
        tm, t = tms[gi], tile(gi)
        in_specs.append(pl.BlockSpec((tm, kdim), lambda jj, i, t=t: (t(i), 0)))
        act = pl.BlockSpec((tm, cw), lambda jj, i, t=t: (t(i), col(jj)))
        in_specs.append(act)
        args += [xs[gi], attns[gi]]
        if states[gi] is not None:
            in_specs += [act, act]
            args += list(states[gi])
    out_shapes, out_specs = [], []
    for gi in range(ng):
        tm, t, m, seq = tms[gi], tile(gi), xs[gi].shape[0], seq_lens[gi]
        spt, tps = max(tm // seq, 1), max(seq // tm, 1)
        out_shapes += [jax.ShapeDtypeStruct((m, d), out_dtypes[gi]),
                       jax.ShapeDtypeStruct((m // seq, SUBLANES, d), _F32)]
        out_specs += [
            pl.BlockSpec((tm, cw), lambda jj, i, t=t: (rowsel(jj, i, t), col(jj))),
            pl.BlockSpec((spt, SUBLANES, cw),
                         lambda jj, i, t=t, tps=tps: (rowsel(jj, i, t) // tps, 0, col(jj)))]
    scratch = [pltpu.VMEM((2, N_GATE * cw, kdim), _BF16)]
    scratch += [pltpu.VMEM((tm + SUBLANES, cw), _F32) for tm in tms]
    res = pl.pallas_call(
        functools.partial(_gm_kernel, starts=starts, ntiles=ntiles, tms=tuple(tms),
                          seq_lens=tuple(seq_lens),
                          has_state=tuple(s is not None for s in states), cw=cw),
        grid=(nj + 1, ni),
        in_specs=in_specs,
        out_specs=out_specs,
        out_shape=out_shapes,
        scratch_shapes=scratch,
        compiler_params=_cparams(("arbitrary", "arbitrary")),
        name="gate_merge",
    )(*args)
    return [(res[2 * gi], res[2 * gi + 1]) for gi in range(ng)]


def _rope_tables(positions, half, n_rot_lanes=LANES):
    pos = np.asarray(positions, np.float64)[:, None]
    lane = np.arange(LANES)
    inv = ROPE_THETA ** (-(lane % half).astype(np.float64) / half)
    ang = pos * inv[None, :]
    sign = np.where((lane % (2 * half)) < half, -1.0, 1.0)
    rot = (lane < n_rot_lanes)[None, :]
    cos = np.where(rot, np.cos(ang), 1.0)
    sin = np.where(rot, np.sin(ang) * sign[None, :], 0.0)
    return jnp.asarray(cos, _F32), jnp.asarray(sin, _F32)


def kernel(x_prompt, x_sample, cache_k, cache_v, cache_kidx, state_conv, page_table,
           w_in, conv_w, w_o, g_mix, g_mlp, w_up, w_down, g_final):
    B, S, D = x_prompt.shape
    Bd, T, _ = x_sample.shape
    depth = w_in.shape[0]
    n_pool = cache_k.shape[1]
    past = page_table.shape[1] * PAGE_SIZE
    R = SUBLANES
    kvw = N_KV_HEADS * HEAD_DIM
    half, ihalf = HEAD_DIM // 2, IDX_DIM // 2
    Mp, Ms = B * S, Bd * R
    tms = [min(Mp, 1024, S // 2), Ms]
    tms_down = [min(Mp, 2048), Ms]

    c_q, c_k = 0, N_HEADS * HEAD_DIM
    c_v = c_k + kvw
    c_qi = c_v + kvw
    c_kiwi = c_qi + IDX_HEADS * IDX_DIM
    c_gate = c_kiwi + IDX_DIM + IDX_HEADS

    pos_p = np.arange(max(S, tms[0])) % S
    pos_s = np.tile(past + np.arange(R), Bd)
    r128 = [_rope_tables(pos_p, half), _rope_tables(pos_s, half)]
    r64 = [_rope_tables(pos_p, ihalf), _rope_tables(pos_s, ihalf)]
    rkiwi = [_rope_tables(pos_p, ihalf, IDX_DIM), _rope_tables(pos_s, ihalf, IDX_DIM)]

    xp = x_prompt.reshape(Mp, D)
    xs = jnp.pad(x_sample, ((0, 0), (0, R - T), (0, 0))).reshape(Ms, D)
    w_in_t = jnp.swapaxes(w_in, 1, 2)
    pool_k = cache_k.reshape(depth * n_pool * PAGE_SIZE * N_KV_HEADS, HEAD_DIM)
    pool_v = cache_v.reshape(depth * n_pool * PAGE_SIZE * N_KV_HEADS, HEAD_DIM)
    pool_ki_t = jnp.swapaxes(cache_kidx, 2, 3).reshape(depth * n_pool, IDX_DIM, PAGE_SIZE)

    outs = {n: [] for n in ("kp", "vp", "kip", "cp", "ks", "vs", "kis", "cs")}
    for l in range(depth):
        hs = [_rmsnorm(xp, g_mix[l], _BF16), _rmsnorm(xs, g_mix[l], _BF16)]
        mm = functools.partial(_matmul, hs, w_in_t, l, tms=tms, w_t=True)
        qp, qs = mm(col_start=c_q, n_cols=N_HEADS * HEAD_DIM, out_dtype=_BF16,
                    rope=(half, r128), scale=HEAD_DIM ** -0.5)
        kp, ks = mm(col_start=c_k, n_cols=kvw, out_dtype=_F32, rope=(half, r128))
        vp, vs = mm(col_start=c_v, n_cols=kvw, out_dtype=_F32)
        qip, qis = mm(col_start=c_qi, n_cols=IDX_HEADS * IDX_DIM, out_dtype=_BF16,
                      rope=(ihalf, r64), pair_major=[True, False])
        kwp, kws = mm(col_start=c_kiwi, n_cols=LANES, out_dtype=_F32, rope=(ihalf, rkiwi))
        page0 = l * n_pool

        kip = kwp[:, :IDX_DIM]
        wit = kwp[:, IDX_DIM:IDX_DIM + IDX_HEADS].T
        attn_p = _prompt_attention(qip, wit, kip, qp, kp, vp, batch=B, seq=S)

        qi_hm = qis.reshape(Bd, R, IDX_HEADS, IDX_DIM).transpose(0, 2, 1, 3).reshape(
            Bd, IDX_HEADS * R, IDX_DIM)
        kis = kws[:, :IDX_DIM]
        w_hm = kws[:, IDX_DIM:IDX_DIM + IDX_HEADS].reshape(Bd, R, IDX_HEADS).transpose(
            0, 2, 1).reshape(Bd, IDX_HEADS * R, 1)
        bias = _sample_index(page_table, qi_hm, w_hm, kis, pool_ki_t, page0, 0, t_real=T)
        attn_s = _sample_attention(page_table, qs.astype(_F32), bias, ks, vs, pool_k, pool_v,
                                   page0)

        s0, s1 = [jnp.repeat(state_conv[l][:, r:r + 1, :], R, axis=1).reshape(Ms, D)
                  for r in range(CONV_WIDTH - 1)]
        (mg_p, tail_p), (mg_s, tail_s) = _gate_merge(
            hs, w_in_t, l, [attn_p, attn_s], [None, (s0, s1)], conv_w[l], col_start=c_gate,
            tms=tms, seq_lens=[S, R], out_dtypes=[_BF16, _F32])
        xp, xs = _matmul([mg_p, mg_s.astype(_BF16)], w_o, l, col_start=0, n_cols=D,
                         out_dtype=_F32, tms=tms, epi="residual", residuals=[xp, xs])
        h2 = [_rmsnorm(xp, g_mlp[l], _BF16), _rmsnorm(xs, g_mlp[l], _BF16)]
        hid = _matmul(h2, w_up, l, col_start=0, n_cols=w_up.shape[2], out_dtype=_BF16, tms=tms,
                      epi="relu2")
        xp, xs = _matmul(hid, _cast_bf16(w_down, l), 0, col_start=0, n_cols=D, out_dtype=_F32,
                         tms=tms_down, epi="residual", residuals=[xp, xs], tn=min(D, 1024),
                         tk=min(w_down.shape[1], 2048))

        outs["kp"].append(kp.reshape(B, S, N_KV_HEADS, HEAD_DIM))
        outs["vp"].append(vp.reshape(B, S, N_KV_HEADS, HEAD_DIM))
        outs["kip"].append(kip.reshape(B, S, IDX_DIM))
        outs["cp"].append(tail_p[:, R - (CONV_WIDTH - 1):, :])
        outs["ks"].append(ks.reshape(Bd, R, N_KV_HEADS, HEAD_DIM)[:, :T])
        outs["vs"].append(vs.reshape(Bd, R, N_KV_HEADS, HEAD_DIM)[:, :T])
        outs["kis"].append(kis.reshape(Bd, R, IDX_DIM)[:, :T])
        outs["cs"].append(tail_s[:, T - (CONV_WIDTH - 1):T, :])

    y_prompt = _rmsnorm(xp, g_final, _F32).reshape(B, S, D)
    y_sample = _rmsnorm(xs, g_final, _F32).reshape(Bd, R, D)[:, :T]
    st = lambda n: jnp.stack(outs[n])
    return (y_prompt, y_sample, st("kp"), st("vp"), st("kip"), st("cp"),
            st("ks"), st("vs"), st("kis"), st("cs"))
```

```python
import functools

import numpy as np
import jax
import jax.numpy as jnp
from jax import lax
from jax.experimental import pallas as pl
from jax.experimental.pallas import tpu as pltpu

HEAD_DIM = 128
N_HEADS = 32
N_KV_HEADS = 8
IDX_HEADS = 32
IDX_DIM = 64
TOPK_MAX = 256
CONV_WIDTH = 3
PAGE_SIZE = 128
ROPE_THETA = 10000.0
NORM_EPS = 1e-6
NEG = -1e30

LANES = 128
SUBLANES = 8
VMEM_LIMIT = 60 * 1024 * 1024
PAGES_PER_STEP = 16
INT_MIN = -2 ** 31
TN = 1024
SUB_TN = 512
SUB_TM = 512
GATE_CW = 256

_F32 = jnp.float32
_BF16 = jnp.bfloat16
_NT = (((1,), (1,)), ((), ()))


def _cparams(sem):
    return pltpu.CompilerParams(dimension_semantics=sem, vmem_limit_bytes=VMEM_LIMIT)


def _rmsnorm_kernel(x_ref, g_ref, o_ref):
    x = x_ref[...]
    y = x * lax.rsqrt(jnp.mean(x * x, axis=-1, keepdims=True) + NORM_EPS)
    o_ref[...] = (y * g_ref[...]).astype(o_ref.dtype)


def _rmsnorm(x, g, out_dtype):
    m, d = x.shape
    tm = min(m, 256)
    return pl.pallas_call(
        _rmsnorm_kernel,
        grid=(m // tm,),
        in_specs=[pl.BlockSpec((tm, d), lambda i: (i, 0)),
                  pl.BlockSpec((1, d), lambda i: (0, 0))],
        out_specs=pl.BlockSpec((tm, d), lambda i: (i, 0)),
        out_shape=jax.ShapeDtypeStruct((m, d), out_dtype),
        compiler_params=_cparams(("arbitrary",)),
        name="rmsnorm",
    )(x, g.reshape(1, d))


def _cast_kernel(w_ref, o_ref):
    o_ref[...] = w_ref[...].astype(o_ref.dtype)


def _cast_bf16(w, layer):
    _, kdim, n = w.shape
    tr = min(kdim, 512)
    return pl.pallas_call(
        _cast_kernel,
        grid=(kdim // tr,),
        in_specs=[pl.BlockSpec((1, tr, n), lambda i: (layer, i, 0))],
        out_specs=pl.BlockSpec((1, tr, n), lambda i: (0, i, 0)),
        out_shape=jax.ShapeDtypeStruct((1, kdim, n), _BF16),
        compiler_params=_cparams(("arbitrary",)),
        name="cast_bf16",
    )(w)


def _rope_slab(z, cos, sin, half):
    if 2 * half == LANES:
        partner = pltpu.roll(z, half, 1)
    else:
        lane = lax.broadcasted_iota(jnp.int32, z.shape, 1)
        first = (lane % (2 * half)) < half
        partner = jnp.where(first, pltpu.roll(z, LANES - half, 1), pltpu.roll(z, half, 1))
    return z * cos + partner * sin


def _epilogue(g, z, c0, *, epi, rope_half, scale):
    o_ref = g["o"]
    width = z.shape[1]
    if rope_half:
        cos, sin = g["cos"][...], g["sin"][...]
        for s in range(width // LANES):
            slab = _rope_slab(z[:, s * LANES:(s + 1) * LANES], cos, sin, rope_half)
            if scale != 1.0:
                slab = slab * scale
            if g["pm"]:
                o_ref[c0 // LANES + s] = slab.astype(o_ref.dtype)
            else:
                o_ref[:, c0 + s * LANES:c0 + (s + 1) * LANES] = slab.astype(o_ref.dtype)
        return
    if epi == "relu2":
        r = jnp.maximum(z, 0.0)
        z = r * r
    elif epi == "residual":
        z = g["res"][:, c0:c0 + width] + z
    o_ref[:, c0:c0 + width] = z.astype(o_ref.dtype)


def _unpack_groups(it, ng, rope_half, epi, pair_major):
    groups = []
    for _ in range(ng):
        g = dict(x=next(it))
        if rope_half:
            g["cos"], g["sin"] = next(it), next(it)
        if epi == "residual":
            g["res"] = next(it)
        groups.append(g)
    for gi, g in enumerate(groups):
        g["o"] = next(it)
        g["pm"] = pair_major[gi]
    return groups


def _mm_ws_kernel(*refs, starts, ntiles, npieces, epi, rope_half, rope_blocks, scale, pair_major,
                  w_t):
    it = iter(refs)
    wp_ref = next(it)
    groups = _unpack_groups(it, len(starts), rope_half, epi, pair_major)
    wb_ref = next(it)
    jj = pl.program_id(0)
    i = pl.program_id(1)
    nj = pl.num_programs(0) - 1

    @pl.when((jj < nj) & (i < npieces))
    def _():
        piece = wp_ref[0].astype(_BF16)
        rows = piece.shape[0]
        wb_ref[jj % 2, pl.ds(pl.multiple_of(i * rows, rows), rows), :] = piece

    for gi, g in enumerate(groups):
        active = (jj > 0) & (i >= starts[gi]) & (i < starts[gi] + ntiles[gi])

        @pl.when(active)
        def _(g=g):
            slot = (jj + 1) % 2
            tn = wb_ref.shape[1] if w_t else wb_ref.shape[2]
            sub = min(tn, SUB_TN)
            for c0 in range(0, tn, sub):
                if w_t:
                    z = lax.dot_general(g["x"][...], wb_ref[slot, c0:c0 + sub, :], _NT,
                                        preferred_element_type=_F32)
                else:
                    z = jnp.dot(g["x"][...], wb_ref[slot, :, c0:c0 + sub],
                                preferred_element_type=_F32)
                if rope_blocks is None:
                    _epilogue(g, z, c0, epi=epi, rope_half=rope_half, scale=scale)
                else:
                    @pl.when(jj <= rope_blocks)
                    def _():
                        _epilogue(g, z, c0, epi=epi, rope_half=rope_half, scale=scale)

                    @pl.when(jj > rope_blocks)
                    def _():
                        _epilogue(g, z, c0, epi=epi, rope_half=0, scale=1.0)


def _mm_kt_kernel(*refs, starts, ntiles):
    it = iter(refs)
    w_ref = next(it)
    groups = _unpack_groups(it, len(starts), 0, "residual", [False] * len(starts))
    i = pl.program_id(1)
    k = pl.program_id(2)
    for gi, g in enumerate(groups):
        active = (i >= starts[gi]) & (i < starts[gi] + ntiles[gi])

        @pl.when(active)
        def _(g=g):
            @pl.when(k == 0)
            def _():
                g["o"][...] = g["res"][...]

            tm = g["x"].shape[0]
            sub = min(tm, SUB_TM)
            for r0 in range(0, tm, sub):
                g["o"][r0:r0 + sub, :] += jnp.dot(g["x"][r0:r0 + sub, :], w_ref[0].astype(_BF16),
                                                   preferred_element_type=_F32)


def _matmul(xs, w, layer, *, col_start, n_cols, out_dtype, tms, epi="none", rope=None,
            scale=1.0, residuals=None, pair_major=None, tn=TN, tk=None, w_t=False,
            rope_blocks=None):
    kdim = w.shape[2] if w_t else w.shape[1]
    tn = min(tn, n_cols)
    nj = n_cols // tn
    cb0 = col_start // tn
    assert n_cols % tn == 0
    assert col_start % (SUBLANES if w_t else tn) == 0
    ng = len(xs)
    pair_major = [False] * ng if pair_major is None else pair_major
    ntiles = [x.shape[0] // tm for x, tm in zip(xs, tms)]
    starts = [int(s) for s in np.concatenate([[0], np.cumsum(ntiles)[:-1]])]
    ni = sum(ntiles)
    ws = tk is None
    if ws:
        tk = kdim
        npieces = max(p for p in (8, 4, 2, 1) if p <= ni)
    else:
        assert epi == "residual" and out_dtype == _F32 and not w_t and kdim % tk == 0
        assert rope is None and not any(pair_major)

    def tile(gi):
        return lambda i: jnp.clip(i - starts[gi], 0, ntiles[gi] - 1)

    if ws:
        col = lambda a: jnp.maximum(a[0] - 1, 0)
        krow = lambda a: 0
        rowsel = lambda a, t: jnp.where(a[0] > 0, t(a[1]), 0)
        nxt = lambda a: jnp.minimum(a[0], nj - 1)
        pc = lambda a: jnp.minimum(a[1], npieces - 1)
        if w_t:
            pr = tn // npieces
            w_spec = pl.BlockSpec(
                (pl.Element(1), pl.Element(pr), pl.Element(tk)),
                lambda *a: (layer, pl.multiple_of(col_start + nxt(a) * tn + pc(a) * pr, SUBLANES),
                            0))
        else:
            w_spec = pl.BlockSpec((1, tk // npieces, tn), lambda *a: (layer, pc(a), cb0 + nxt(a)))
    else:
        col = lambda a: a[0]
        krow = lambda a: a[2]
        rowsel = lambda a, t: t(a[1])
        w_spec = pl.BlockSpec((1, tk, tn), lambda *a: (layer, a[2], cb0 + a[0]))

    in_specs, args = [w_spec], [w]
    rope_half = 0
    for gi in range(ng):
        tm, t = tms[gi], tile(gi)
        in_specs.append(pl.BlockSpec((tm, tk), lambda *a, t=t: (rowsel(a, t), krow(a))))
        args.append(xs[gi])
        if rope is not None:
            rope_half = rope[0]
            cos, sin = rope[1][gi]
            nper = cos.shape[0] // tm
            tspec = pl.BlockSpec((tm, LANES),
                                 lambda *a, t=t, nper=nper: (rowsel(a, t) % nper, 0))
            in_specs += [tspec, tspec]
            args += [cos, sin]
        if epi == "residual":
            mode = {} if ws else dict(pipeline_mode=pl.Buffered(1))
            in_specs.append(pl.BlockSpec((tm, tn), lambda *a, t=t: (rowsel(a, t), col(a)),
                                         **mode))
            args.append(residuals[gi])
    out_shapes, out_specs = [], []
    for gi in range(ng):
        tm, t, m = tms[gi], tile(gi), xs[gi].shape[0]
        if pair_major[gi]:
            out_shapes.append(jax.ShapeDtypeStruct((n_cols // LANES, m, LANES), out_dtype))
            out_specs.append(pl.BlockSpec((tn // LANES, tm, LANES),
                                          lambda *a, t=t: (col(a), rowsel(a, t), 0)))
        else:
            out_shapes.append(jax.ShapeDtypeStruct((m, n_cols), out_dtype))
            out_specs.append(pl.BlockSpec((tm, tn), lambda *a, t=t: (rowsel(a, t), col(a))))
    if ws:
        body = functools.partial(_mm_ws_kernel, starts=starts, ntiles=ntiles, npieces=npieces,
                                 epi=epi, rope_half=rope_half, rope_blocks=rope_blocks,
                                 scale=scale, pair_major=tuple(pair_major), w_t=w_t)
        grid = (nj + 1, ni)
        scratch = [pltpu.VMEM((2, tn, tk) if w_t else (2, tk, tn), _BF16)]
    else:
        body = functools.partial(_mm_kt_kernel, starts=starts, ntiles=ntiles)
        grid = (nj, ni, kdim // tk)
        scratch = []
    return pl.pallas_call(
        body,
        grid=grid,
        in_specs=in_specs,
        out_specs=out_specs,
        out_shape=out_shapes,
        scratch_shapes=scratch,
        compiler_params=_cparams(("arbitrary",) * len(grid)),
        name="matmul_" + epi + ("_rope%d" % rope_half if rope_half else ""),
    )(*args)


def _sortable(x):
    bits = pltpu.bitcast(x, jnp.int32)
    return jnp.where(bits < 0, bits ^ jnp.int32(0x7FFFFFFF), bits)


def _kth_largest(keys, n_sel, count_fn):
    cnt = count_fn((keys >= 0).astype(jnp.int32))
    thr0 = jnp.where(cnt >= n_sel, jnp.int32(0), jnp.int32(INT_MIN))

    def body(i, thr):
        cand = thr | jnp.left_shift(jnp.int32(1), 30 - i)
        cnt = count_fn((keys >= cand).astype(jnp.int32))
        return jnp.where(cnt >= n_sel, cand, thr)

    return lax.fori_loop(0, 31, body, thr0)


def _select_topk(keys, idx, n_sel, count_fn, idx_bits):
    thr = _kth_largest(keys, n_sel, count_fn)
    gt = keys > thr
    eq = keys == thr
    need = n_sel - count_fn(gt.astype(jnp.int32))
    surplus = count_fn(eq.astype(jnp.int32)) - need

    def tie_cut():
        def body(b, lo):
            cand = lo + jnp.left_shift(jnp.int32(1), idx_bits - 1 - b)
            below = count_fn((eq & (idx < cand)).astype(jnp.int32))
            return jnp.where(below < need, cand, lo)

        return lax.fori_loop(0, idx_bits, body, jnp.zeros_like(thr))

    cut = lax.cond(jnp.max(surplus) > 0, tie_cut,
                   lambda: jnp.full_like(thr, jnp.iinfo(jnp.int32).max))
    return gt | (eq & (idx <= cut))


def _pattn_kernel(qi_ref, wit_ref, ki_ref, q_ref, k_ref, v_ref, o_ref, sc_ref, bias_ref, *,
                  n_sel, kv_rep):
    jq = pl.program_id(1)
    g = pl.program_id(2)
    s_len, tq = sc_ref.shape
    n_pairs = qi_ref.shape[0]

    def select(klen):
        kb = ki_ref[0:klen, :].astype(_BF16)
        zeros = jnp.zeros_like(kb)
        k_even = jnp.concatenate([kb, zeros], axis=1)
        k_odd = jnp.concatenate([zeros, kb], axis=1)
        w_scale = (IDX_HEADS ** -0.5) * (IDX_DIM ** -0.5)
        sc_ref[0:klen, :] = jnp.zeros((klen, tq), _F32)

        def body(p, carry):
            x = qi_ref[p]
            d0 = lax.dot_general(k_even, x, _NT, preferred_element_type=_F32)
            d1 = lax.dot_general(k_odd, x, _NT, preferred_element_type=_F32)
            w0 = wit_ref[pl.ds(2 * p, 1), :] * w_scale
            w1 = wit_ref[pl.ds(2 * p + 1, 1), :] * w_scale
            sc_ref[0:klen, :] += jnp.maximum(d0, 0.0) * w0 + jnp.maximum(d1, 0.0) * w1
            return carry

        lax.fori_loop(0, n_pairs, body, 0)

        key_pos = lax.broadcasted_iota(jnp.int32, (klen, tq), 0)
        q_pos = jq * tq + lax.broadcasted_iota(jnp.int32, (klen, tq), 1)
        adm = key_pos <= q_pos
        keys = _sortable(jnp.where(adm, sc_ref[0:klen, :], NEG))
        sel = _select_topk(keys, key_pos, n_sel, lambda c: jnp.sum(c, axis=0, keepdims=True),
                           (s_len - 1).bit_length()) & adm
        bias_ref[:, 0:klen] = jnp.where(sel, 0.0, NEG).astype(_F32).T

    def attend(klen):
        kg = k_ref[0:klen, :].astype(_BF16)
        vg = v_ref[0:klen, :].astype(_BF16)
        bias = bias_ref[:, 0:klen]
        for r in range(kv_rep):
            qh = q_ref[:, r * HEAD_DIM:(r + 1) * HEAD_DIM]
            s = lax.dot_general(qh, kg, _NT, preferred_element_type=_F32) + bias
            m = jnp.max(s, axis=1, keepdims=True)
            p = jnp.exp(s - m)
            l = jnp.sum(p, axis=1, keepdims=True)
            o = jnp.dot(p.astype(_BF16), vg, preferred_element_type=_F32) / l
            o_ref[:, r * HEAD_DIM:(r + 1) * HEAD_DIM] = o.astype(o_ref.dtype)

    for n in range(1, s_len // tq + 1):
        @pl.when(jq == n - 1)
        def _(klen=n * tq):
            @pl.when(g == 0)
            def _():
                select(klen)

            attend(klen)


def _prompt_attention(qi_pm, wit, ki, q, kv, *, batch, seq):
    m, d = q.shape
    n_kv = kv.shape[1] // (2 * HEAD_DIM)
    kv_rep = (d // HEAD_DIM) // n_kv
    tq = min(seq, 256)
    nq = seq // tq
    n_sel = min(TOPK_MAX, seq // 4)
    n_pairs = qi_pm.shape[0]
    gw = kv_rep * HEAD_DIM
    return pl.pallas_call(
        functools.partial(_pattn_kernel, n_sel=n_sel, kv_rep=kv_rep),
        grid=(batch, nq, n_kv),
        in_specs=[
            pl.BlockSpec((n_pairs, tq, LANES), lambda b, j, g: (0, b * nq + j, 0)),
            pl.BlockSpec((wit.shape[0], tq), lambda b, j, g: (0, b * nq + j)),
            pl.BlockSpec((seq, ki.shape[1]), lambda b, j, g: (b, 0)),
            pl.BlockSpec((tq, gw), lambda b, j, g: (b * nq + j, g)),
            pl.BlockSpec((seq, HEAD_DIM), lambda b, j, g: (b, g)),
            pl.BlockSpec((seq, HEAD_DIM), lambda b, j, g: (b, n_kv + g)),
        ],
        out_specs=pl.BlockSpec((tq, gw), lambda b, j, g: (b * nq + j, g)),
        out_shape=jax.ShapeDtypeStruct((m, d), _BF16),
        scratch_shapes=[pltpu.VMEM((seq, tq), _F32), pltpu.VMEM((tq, seq), _F32)],
        compiler_params=_cparams(("arbitrary", "arbitrary", "arbitrary")),
        name="prompt_attention",
    )(qi_pm, wit, ki, q, kv, kv)


def _sidx_kernel(pt_ref, qi_ref, w_ref, kin_ref, *rest, n_sel, t_real, n_chunks):
    pages = rest[:PAGES_PER_STEP]
    bias_ref, sc_ref = rest[PAGES_PER_STEP], rest[PAGES_PER_STEP + 1]
    c = pl.program_id(1)
    rows = SUBLANES
    chunk = PAGES_PER_STEP * PAGE_SIZE
    w = w_ref[0] * ((IDX_HEADS ** -0.5) * (IDX_DIM ** -0.5))
    qi = qi_ref[0]

    def scores(d):
        d = jnp.maximum(d, 0.0) * w
        acc = d[0:rows]
        for h in range(1, IDX_HEADS):
            acc = acc + d[h * rows:(h + 1) * rows]
        return acc

    @pl.when(c < n_chunks)
    def _():
        keys_t = jnp.concatenate([p[0] for p in pages], axis=1).astype(_BF16)
        sc_ref[c] = scores(jnp.dot(qi, keys_t, preferred_element_type=_F32))

    @pl.when(c == n_chunks)
    def _():
        kn = kin_ref[...].astype(_BF16)
        kn = jnp.concatenate([kn, jnp.zeros((PAGE_SIZE - rows, kn.shape[1]), _BF16)], axis=0)
        s_new = scores(lax.dot_general(qi, kn, _NT, preferred_element_type=_F32))
        s_new = jnp.concatenate([s_new, jnp.zeros((rows, chunk - PAGE_SIZE), _F32)], axis=1)
        shape = (n_chunks + 1, rows, chunk)
        ci = lax.broadcasted_iota(jnp.int32, shape, 0)
        qrow = lax.broadcasted_iota(jnp.int32, shape, 1)
        lane = lax.broadcasted_iota(jnp.int32, shape, 2)
        adm = (ci < n_chunks) | ((lane <= qrow) & (lane < t_real))
        sc_ref[n_chunks] = s_new
        keys = _sortable(jnp.where(adm, sc_ref[...], NEG))
        count = lambda x: jnp.sum(jnp.sum(x, axis=0, keepdims=True), axis=2, keepdims=True)
        key_idx = ci * chunk + lane
        n_keys = (n_chunks + 1) * chunk
        sel = _select_topk(keys, key_idx, n_sel, count, (n_keys - 1).bit_length()) & adm
        bias_ref[0] = jnp.where(sel, 0.0, NEG).astype(_F32)


def _page_spec(block, n_chunks, page0, i):
    zeros = (0,) * (len(block) - 1)
    return pl.BlockSpec(
        block,
        lambda b, c, pt: (page0 + pt[b, jnp.minimum(c, n_chunks - 1) * PAGES_PER_STEP + i],)
        + zeros)


def _sample_index(page_table, qi_hm, w_hm, ki_new, pool_ki, page0, row0, *, t_real):
    bd, n_pages = page_table.shape
    n_chunks = n_pages // PAGES_PER_STEP
    chunk = PAGES_PER_STEP * PAGE_SIZE
    past = n_pages * PAGE_SIZE
    n_sel = min(TOPK_MAX, (past + t_real) // 4)
    hr = qi_hm.shape[1]
    grid_spec = pltpu.PrefetchScalarGridSpec(
        num_scalar_prefetch=1,
        grid=(bd, n_chunks + 1),
        in_specs=[pl.BlockSpec((1, hr, IDX_DIM), lambda b, c, pt: (b, 0, 0)),
                  pl.BlockSpec((1, hr, 1), lambda b, c, pt: (b, 0, 0)),
                  pl.BlockSpec((SUBLANES, IDX_DIM), lambda b, c, pt: (row0 + b, 0))]
                 + [_page_spec((1, IDX_DIM, PAGE_SIZE), n_chunks, page0, i)
                    for i in range(PAGES_PER_STEP)],
        out_specs=pl.BlockSpec((1, n_chunks + 1, SUBLANES, chunk), lambda b, c, pt: (b, 0, 0, 0)),
        scratch_shapes=[pltpu.VMEM((n_chunks + 1, SUBLANES, chunk), _F32)],
    )
    return pl.pallas_call(
        functools.partial(_sidx_kernel, n_sel=n_sel, t_real=t_real, n_chunks=n_chunks),
        grid_spec=grid_spec,
        out_shape=jax.ShapeDtypeStruct((bd, n_chunks + 1, SUBLANES, chunk), _F32),
        compiler_params=_cparams(("arbitrary", "arbitrary")),
        name="sample_index",
    )(page_table, qi_hm, w_hm, ki_new, *([pool_ki] * PAGES_PER_STEP))


def _sattn_kernel(pt_ref, q_ref, bias_ref, kn_ref, vn_ref, *rest, n_chunks, n_kv, kv_rep):
    kpages = rest[:PAGES_PER_STEP]
    vpages = rest[PAGES_PER_STEP:2 * PAGES_PER_STEP]
    o_ref, m_ref, l_ref, acc_ref = rest[2 * PAGES_PER_STEP:]
    c = pl.program_id(1)
    rows = SUBLANES

    @pl.when(c == 0)
    def _():
        m_ref[...] = jnp.full_like(m_ref, -jnp.inf)
        l_ref[...] = jnp.zeros_like(l_ref)
        acc_ref[...] = jnp.zeros_like(acc_ref)

    gr = kv_rep * rows

    def update(kgs, vgs, bias):
        parts = []
        for g in range(n_kv):
            qg = jnp.concatenate(
                [q_ref[:, (g * kv_rep + r) * HEAD_DIM:(g * kv_rep + r + 1) * HEAD_DIM]
                 for r in range(kv_rep)], axis=0).astype(_BF16)
            parts.append(lax.dot_general(qg, kgs[g], _NT, preferred_element_type=_F32))
        s = jnp.concatenate(parts, axis=0) + jnp.concatenate([bias] * (n_kv * kv_rep), axis=0)
        m_old = m_ref[...]
        m_new = jnp.maximum(m_old, jnp.max(s, axis=1, keepdims=True))
        alpha = jnp.exp(m_old - m_new)
        p = jnp.exp(s - m_new)
        l_ref[...] = alpha * l_ref[...] + jnp.sum(p, axis=1, keepdims=True)
        pb = p.astype(_BF16)
        pv = jnp.concatenate(
            [jnp.dot(pb[g * gr:(g + 1) * gr], vgs[g], preferred_element_type=_F32)
             for g in range(n_kv)], axis=0)
        acc_ref[...] = alpha * acc_ref[...] + pv
        m_ref[...] = m_new

    @pl.when(c < n_chunks)
    def _():
        def head_rows(pages, g):
            rows_g = pl.ds(g, PAGE_SIZE, stride=n_kv)
            return jnp.concatenate([p[rows_g, :] for p in pages], axis=0).astype(_BF16)

        update([head_rows(kpages, g) for g in range(n_kv)],
               [head_rows(vpages, g) for g in range(n_kv)], bias_ref[0, 0])

    @pl.when(c == n_chunks)
    def _():
        pad = jnp.zeros((PAGE_SIZE - rows, HEAD_DIM), _BF16)

        def new_rows(ref, g):
            return jnp.concatenate(
                [ref[:, g * HEAD_DIM:(g + 1) * HEAD_DIM].astype(_BF16), pad], axis=0)

        update([new_rows(kn_ref, g) for g in range(n_kv)],
               [new_rows(vn_ref, g) for g in range(n_kv)], bias_ref[0, 0][:, :PAGE_SIZE])
        o = acc_ref[...] / l_ref[...]
        for h in range(n_kv * kv_rep):
            o_ref[:, h * HEAD_DIM:(h + 1) * HEAD_DIM] = (
                o[h * rows:(h + 1) * rows].astype(o_ref.dtype))


def _sample_attention(page_table, q, bias, kv_new, pool_k, pool_v, page0):
    bd, n_pages = page_table.shape
    n_chunks = n_pages // PAGES_PER_STEP
    chunk = PAGES_PER_STEP * PAGE_SIZE
    m, d = q.shape
    kvw = kv_new.shape[1] // 2
    n_kv = kvw // HEAD_DIM
    kv_rep = (d // HEAD_DIM) // n_kv
    grid_spec = pltpu.PrefetchScalarGridSpec(
        num_scalar_prefetch=1,
        grid=(bd, n_chunks + 1),
        in_specs=[pl.BlockSpec((SUBLANES, d), lambda b, c, pt: (b, 0)),
                  pl.BlockSpec((1, 1, SUBLANES, chunk), lambda b, c, pt: (b, c, 0, 0)),
                  pl.BlockSpec((SUBLANES, kvw), lambda b, c, pt: (b, 0)),
                  pl.BlockSpec((SUBLANES, kvw), lambda b, c, pt: (b, 1))]
                 + [_page_spec((PAGE_SIZE * n_kv, HEAD_DIM), n_chunks, page0, i)
                    for i in range(PAGES_PER_STEP)] * 2,
        out_specs=pl.BlockSpec((SUBLANES, d), lambda b, c, pt: (b, 0)),
        scratch_shapes=[pltpu.VMEM((n_kv * kv_rep * SUBLANES, 1), _F32),
                        pltpu.VMEM((n_kv * kv_rep * SUBLANES, 1), _F32),
                        pltpu.VMEM((n_kv * kv_rep * SUBLANES, HEAD_DIM), _F32)],
    )
    return pl.pallas_call(
        functools.partial(_sattn_kernel, n_chunks=n_chunks, n_kv=n_kv, kv_rep=kv_rep),
        grid_spec=grid_spec,
        out_shape=jax.ShapeDtypeStruct((m, d), _F32),
        compiler_params=_cparams(("arbitrary", "arbitrary")),
        name="sample_attention",
    )(page_table, q, bias, kv_new, kv_new,
      *([pool_k] * PAGES_PER_STEP), *([pool_v] * PAGES_PER_STEP))


N_GATE = 5


def _gm_kernel(*refs, starts, ntiles, tms, seq_lens, has_state, cw):
    ng = len(starts)
    it = iter(refs)
    wp_ref, taps_ref = next(it), next(it)
    groups = []
    for gi in range(ng):
        g = dict(x=next(it), attn=next(it))
        if has_state[gi]:
            g["s0"], g["s1"] = next(it), next(it)
        groups.append(g)
    for g in groups:
        g["o"], g["tail"] = next(it), next(it)
    wb_ref = next(it)
    for g in groups:
        g["u"] = next(it)
    jj = pl.program_id(0)
    i = pl.program_id(1)
    nj = pl.num_programs(0) - 1

    @pl.when((jj < nj) & (i < N_GATE))
    def _():
        wb_ref[jj % 2, pl.ds(pl.multiple_of(i * cw, cw), cw), :] = wp_ref[0].astype(_BF16)

    for gi, g in enumerate(groups):
        tm, seq = tms[gi], seq_lens[gi]
        active = (jj > 0) & (i >= starts[gi]) & (i < starts[gi] + ntiles[gi])

        @pl.when(active)
        def _(g=g, gi=gi, tm=tm, seq=seq):
            u_ref = g["u"]
            first = i == starts[gi]

            @pl.when(first)
            def _():
                u_ref[0:SUBLANES, :] = jnp.zeros((SUBLANES, cw), _F32)

            @pl.when(jnp.logical_not(first))
            def _():
                u_ref[0:SUBLANES, :] = u_ref[tm:tm + SUBLANES, :]

            sub = min(tm, SUB_TM)
            for r0 in range(0, tm, sub):
                rs = slice(r0, r0 + sub)
                z = lax.dot_general(g["x"][rs, :], wb_ref[(jj + 1) % 2], _NT,
                                    preferred_element_type=_F32)
                cb, cc, ch, ga, gc = [z[:, s * cw:(s + 1) * cw] for s in range(N_GATE)]
                u = cc * ch
                u_ref[SUBLANES + r0:SUBLANES + r0 + sub, :] = u
                t = ((i - starts[gi]) * tm + r0
                     + lax.broadcasted_iota(jnp.int32, (sub, 1), 0)) % seq
                if "s0" in g:
                    s0, s1 = g["s0"][rs, :], g["s1"][rs, :]
                else:
                    s0 = s1 = jnp.zeros((sub, cw), _F32)
                h1 = SUBLANES - 1 + r0
                u1 = jnp.where(t >= 1, u_ref[h1:h1 + sub, :], s1)
                u2 = jnp.where(t >= 2, u_ref[h1 - 1:h1 - 1 + sub, :], jnp.where(t == 1, s1, s0))
                y = u * taps_ref[2:3, :] + u1 * taps_ref[1:2, :] + u2 * taps_ref[0:1, :]
                merged = (jax.nn.sigmoid(ga) * g["attn"][rs, :].astype(_F32)
                          + jax.nn.sigmoid(gc) * (cb * y))
                g["o"][rs, :] = merged.astype(g["o"].dtype)
            if tm >= seq:
                for s in range(tm // seq):
                    g["tail"][s] = u_ref[(s + 1) * seq:(s + 1) * seq + SUBLANES, :]
            else:
                g["tail"][0] = u_ref[tm:tm + SUBLANES, :]


def _gate_merge(xs, w_t, layer, attns, states, taps, *, col_start, tms, seq_lens, out_dtypes):
    assert CONV_WIDTH == 3
    kdim = w_t.shape[2]
    d = attns[0].shape[1]
    cw = min(d, GATE_CW)
    nj = d // cw
    ng = len(xs)
    ntiles = [x.shape[0] // tm for x, tm in zip(xs, tms)]
    starts = [int(s) for s in np.concatenate([[0], np.cumsum(ntiles)[:-1]])]
    ni = sum(ntiles)
    assert ni >= N_GATE and col_start % SUBLANES == 0 and d % cw == 0

    def tile(gi):
        return lambda i: jnp.clip(i - starts[gi], 0, ntiles[gi] - 1)

    col = lambda jj: jnp.maximum(jj - 1, 0)
    rowsel = lambda jj, i, t: jnp.where(jj > 0, t(i), 0)
    nxt = lambda jj: jnp.minimum(jj, nj - 1)
    seg = lambda i: jnp.minimum(i, N_GATE - 1)
    in_specs = [
        pl.BlockSpec((pl.Element(1), pl.Element(cw), pl.Element(kdim)),
                     lambda jj, i: (layer, pl.multiple_of(col_start + seg(i) * d + nxt(jj) * cw,
                                                          SUBLANES), 0)),
        pl.BlockSpec((CONV_WIDTH, cw), lambda jj, i: (0, col(jj)))]
    args = [w_t, taps]
    for gi in range(ng):
        tm, t = tms[gi], tile(gi)
        in_specs.append(pl.BlockSpec((tm, kdim), lambda jj, i, t=t: (rowsel(jj, i, t), 0)))
        act = pl.BlockSpec((tm, cw), lambda jj, i, t=t: (rowsel(jj, i, t), col(jj)))
        in_specs.append(act)
        args += [xs[gi], attns[gi]]
        if states[gi] is not None:
            in_specs += [act, act]
            args += list(states[gi])
    out_shapes, out_specs = [], []
    for gi in range(ng):
        tm, t, m, seq = tms[gi], tile(gi), xs[gi].shape[0], seq_lens[gi]
        spt, tps = max(tm // seq, 1), max(seq // tm, 1)
        out_shapes += [jax.ShapeDtypeStruct((m, d), out_dtypes[gi]),
                       jax.ShapeDtypeStruct((m // seq, SUBLANES, d), _F32)]
        out_specs += [
            pl.BlockSpec((tm, cw), lambda jj, i, t=t: (rowsel(jj, i, t), col(jj))),
            pl.BlockSpec((spt, SUBLANES, cw),
                         lambda jj, i, t=t, tps=tps: (rowsel(jj, i, t) // tps, 0, col(jj)))]
    scratch = [pltpu.VMEM((2, N_GATE * cw, kdim), _BF16)]
    scratch += [pltpu.VMEM((tm + SUBLANES, cw), _F32) for tm in tms]
    res = pl.pallas_call(
        functools.partial(_gm_kernel, starts=starts, ntiles=ntiles, tms=tuple(tms),
                          seq_lens=tuple(seq_lens),
                          has_state=tuple(s is not None for s in states), cw=cw),
        grid=(nj + 1, ni),
        in_specs=in_specs,
        out_specs=out_specs,
        out_shape=out_shapes,
        scratch_shapes=scratch,
        compiler_params=_cparams(("arbitrary", "arbitrary")),
        name="gate_merge",
    )(*args)
    return [(res[2 * gi], res[2 * gi + 1]) for gi in range(ng)]


def _rope_tables(positions, half, n_rot_lanes=LANES):
    pos = np.asarray(positions, np.float64)[:, None]
    lane = np.arange(LANES)
    inv = ROPE_THETA ** (-(lane % half).astype(np.float64) / half)
    ang = pos * inv[None, :]
    sign = np.where((lane % (2 * half)) < half, -1.0, 1.0)
    rot = (lane < n_rot_lanes)[None, :]
    cos = np.where(rot, np.cos(ang), 1.0)
    sin = np.where(rot, np.sin(ang) * sign[None, :], 0.0)
    return jnp.asarray(cos, _F32), jnp.asarray(sin, _F32)


def kernel(x_prompt, x_sample, cache_k, cache_v, cache_kidx, state_conv, page_table,
           w_in, conv_w, w_o, g_mix, g_mlp, w_up, w_down, g_final):
    B, S, D = x_prompt.shape
    Bd, T, _ = x_sample.shape
    depth = w_in.shape[0]
    n_pool = cache_k.shape[1]
    past = page_table.shape[1] * PAGE_SIZE
    R = SUBLANES
    kvw = N_KV_HEADS * HEAD_DIM
    half, ihalf = HEAD_DIM // 2, IDX_DIM // 2
    Mp, Ms = B * S, Bd * R
    tms = [min(Mp, 1024, S // 2), Ms]
    tms_down = [min(Mp, 2048), Ms]

    c_q, c_k = 0, N_HEADS * HEAD_DIM
    c_v = c_k + kvw
    c_qi = c_v + kvw
    c_kiwi = c_qi + IDX_HEADS * IDX_DIM
    c_gate = c_kiwi + IDX_DIM + IDX_HEADS

    pos_p = np.arange(max(S, tms[0])) % S
    pos_s = np.tile(past + np.arange(R), Bd)
    r128 = [_rope_tables(pos_p, half), _rope_tables(pos_s, half)]
    r64 = [_rope_tables(pos_p, ihalf), _rope_tables(pos_s, ihalf)]
    rkiwi = [_rope_tables(pos_p, ihalf, IDX_DIM), _rope_tables(pos_s, ihalf, IDX_DIM)]

    xp = x_prompt.reshape(Mp, D)
    xs = jnp.pad(x_sample, ((0, 0), (0, R - T), (0, 0))).reshape(Ms, D)
    w_in_t = jnp.swapaxes(w_in, 1, 2)
    pool_k = cache_k.reshape(depth * n_pool * PAGE_SIZE * N_KV_HEADS, HEAD_DIM)
    pool_v = cache_v.reshape(depth * n_pool * PAGE_SIZE * N_KV_HEADS, HEAD_DIM)
    pool_ki_t = jnp.swapaxes(cache_kidx, 2, 3).reshape(depth * n_pool, IDX_DIM, PAGE_SIZE)

    outs = {n: [] for n in ("kp", "vp", "kip", "cp", "ks", "vs", "kis", "cs")}
    for l in range(depth):
        hs = [_rmsnorm(xp, g_mix[l], _BF16), _rmsnorm(xs, g_mix[l], _BF16)]
        mm = functools.partial(_matmul, hs, w_in_t, l, tms=tms, w_t=True)
        qp, qs = mm(col_start=c_q, n_cols=N_HEADS * HEAD_DIM, out_dtype=_BF16,
                    rope=(half, r128), scale=HEAD_DIM ** -0.5)
        kvp, kvs = mm(col_start=c_k, n_cols=2 * kvw, out_dtype=_F32, rope=(half, r128),
                      tn=min(TN, kvw), rope_blocks=kvw // min(TN, kvw))
        qip, qis = mm(col_start=c_qi, n_cols=IDX_HEADS * IDX_DIM, out_dtype=_BF16,
                      rope=(ihalf, r64), pair_major=[True, False])
        kwp, kws = mm(col_start=c_kiwi, n_cols=LANES, out_dtype=_F32, rope=(ihalf, rkiwi))
        page0 = l * n_pool

        kip = kwp[:, :IDX_DIM]
        wit = kwp[:, IDX_DIM:IDX_DIM + IDX_HEADS].T
        attn_p = _prompt_attention(qip, wit, kip, qp, kvp, batch=B, seq=S)

        qi_hm = qis.reshape(Bd, R, IDX_HEADS, IDX_DIM).transpose(0, 2, 1, 3).reshape(
            Bd, IDX_HEADS * R, IDX_DIM)
        kis = kws[:, :IDX_DIM]
        w_hm = kws[:, IDX_DIM:IDX_DIM + IDX_HEADS].reshape(Bd, R, IDX_HEADS).transpose(
            0, 2, 1).reshape(Bd, IDX_HEADS * R, 1)
        bias = _sample_index(page_table, qi_hm, w_hm, kis, pool_ki_t, page0, 0, t_real=T)
        attn_s = _sample_attention(page_table, qs.astype(_F32), bias, kvs, pool_k, pool_v, page0)

        s0, s1 = [jnp.repeat(state_conv[l][:, r:r + 1, :], R, axis=1).reshape(Ms, D)
                  for r in range(CONV_WIDTH - 1)]
        (mg_p, tail_p), (mg_s, tail_s) = _gate_merge(
            hs, w_in_t, l, [attn_p, attn_s], [None, (s0, s1)], conv_w[l], col_start=c_gate,
            tms=tms, seq_lens=[S, R], out_dtypes=[_BF16, _F32])
        xp, xs = _matmul([mg_p, mg_s.astype(_BF16)], w_o, l, col_start=0, n_cols=D,
                         out_dtype=_F32, tms=tms, epi="residual", residuals=[xp, xs])
        h2 = [_rmsnorm(xp, g_mlp[l], _BF16), _rmsnorm(xs, g_mlp[l], _BF16)]
        hid = _matmul(h2, w_up, l, col_start=0, n_cols=w_up.shape[2], out_dtype=_BF16, tms=tms,
                      epi="relu2")
        xp, xs = _matmul(hid, _cast_bf16(w_down, l), 0, col_start=0, n_cols=D, out_dtype=_F32,
                         tms=tms_down, epi="residual", residuals=[xp, xs], tn=min(D, 1024),
                         tk=min(w_down.shape[1], 2048))

        outs["kp"].append(kvp[:, :kvw].reshape(B, S, N_KV_HEADS, HEAD_DIM))
        outs["vp"].append(kvp[:, kvw:].reshape(B, S, N_KV_HEADS, HEAD_DIM))
        outs["kip"].append(kip.reshape(B, S, IDX_DIM))
        outs["cp"].append(tail_p[:, R - (CONV_WIDTH - 1):, :])
        outs["ks"].append(kvs[:, :kvw].reshape(Bd, R, N_KV_HEADS, HEAD_DIM)[:, :T])
        outs["vs"].append(kvs[:, kvw:].reshape(Bd, R, N_KV_HEADS, HEAD_DIM)[:, :T])
        outs["kis"].append(kis.reshape(Bd, R, IDX_DIM)[:, :T])
        outs["cs"].append(tail_s[:, T - (CONV_WIDTH - 1):T, :])

    y_prompt = _rmsnorm(xp, g_final, _F32).reshape(B, S, D)
    y_sample = _rmsnorm(xs, g_final, _F32).reshape(Bd, R, D)[:, :T]
    st = lambda n: jnp.stack(outs[n])
    return (y_prompt, y_sample, st("kp"), st("vp"), st("kip"), st("cp"),
            st("ks"), st("vs"), st("kis"), st("cs"))
```

```python
import functools

import numpy as np
import jax
import jax.numpy as jnp
from jax import lax
from jax.experimental import pallas as pl
from jax.experimental.pallas import tpu as pltpu

HEAD_DIM = 128
N_HEADS = 32
N_KV_HEADS = 8
IDX_HEADS = 32
IDX_DIM = 64
TOPK_MAX = 256
CONV_WIDTH = 3
PAGE_SIZE = 128
ROPE_THETA = 10000.0
NORM_EPS = 1e-6
NEG = -1e30

LANES = 128
SUBLANES = 8
VMEM_LIMIT = 60 * 1024 * 1024
PAGES_PER_STEP = 16
INT_MIN = -2 ** 31
TN = 1024
SUB_TN = 512
SUB_TM = 512
GATE_CW = 256

_F32 = jnp.float32
_BF16 = jnp.bfloat16
_NT = (((1,), (1,)), ((), ()))


def _cparams(sem):
    return pltpu.CompilerParams(dimension_semantics=sem, vmem_limit_bytes=VMEM_LIMIT)


def _rmsnorm_kernel(x_ref, g_ref, o_ref):
    x = x_ref[...]
    y = x * lax.rsqrt(jnp.mean(x * x, axis=-1, keepdims=True) + NORM_EPS)
    o_ref[...] = (y * g_ref[...]).astype(o_ref.dtype)


def _rmsnorm(x, g, out_dtype):
    m, d = x.shape
    tm = min(m, 256)
    return pl.pallas_call(
        _rmsnorm_kernel,
        grid=(m // tm,),
        in_specs=[pl.BlockSpec((tm, d), lambda i: (i, 0)),
                  pl.BlockSpec((1, d), lambda i: (0, 0))],
        out_specs=pl.BlockSpec((tm, d), lambda i: (i, 0)),
        out_shape=jax.ShapeDtypeStruct((m, d), out_dtype),
        compiler_params=_cparams(("arbitrary",)),
        name="rmsnorm",
    )(x, g.reshape(1, d))


def _rope_slab(z, cos, sin, half):
    if 2 * half == LANES:
        partner = pltpu.roll(z, half, 1)
    else:
        lane = lax.broadcasted_iota(jnp.int32, z.shape, 1)
        first = (lane % (2 * half)) < half
        partner = jnp.where(first, pltpu.roll(z, LANES - half, 1), pltpu.roll(z, half, 1))
    return z * cos + partner * sin


def _epilogue(g, z, c0, *, epi, rope_half, scale):
    o_ref = g["o"]
    width = z.shape[1]
    if rope_half:
        cos, sin = g["cos"][...], g["sin"][...]
        for s in range(width // LANES):
            slab = _rope_slab(z[:, s * LANES:(s + 1) * LANES], cos, sin, rope_half)
            if scale != 1.0:
                slab = slab * scale
            if g["pm"]:
                o_ref[c0 // LANES + s] = slab.astype(o_ref.dtype)
            else:
                o_ref[:, c0 + s * LANES:c0 + (s + 1) * LANES] = slab.astype(o_ref.dtype)
        return
    if epi == "relu2":
        r = jnp.maximum(z, 0.0)
        z = r * r
    elif epi == "residual":
        z = g["res"][:, c0:c0 + width] + z
    o_ref[:, c0:c0 + width] = z.astype(o_ref.dtype)


def _unpack_groups(it, ng, rope_half, epi, pair_major):
    groups = []
    for _ in range(ng):
        g = dict(x=next(it))
        if rope_half:
            g["cos"], g["sin"] = next(it), next(it)
        if epi == "residual":
            g["res"] = next(it)
        groups.append(g)
    for gi, g in enumerate(groups):
        g["o"] = next(it)
        g["pm"] = pair_major[gi]
    return groups


def _mm_ws_kernel(*refs, starts, ntiles, npieces, epi, rope_half, rope_blocks, scale, pair_major,
                  w_t, side_cast):
    it = iter(refs)
    wp_ref = next(it)
    side_in = next(it) if side_cast else None
    groups = _unpack_groups(it, len(starts), rope_half, epi, pair_major)
    side_out = next(it) if side_cast else None
    wb_ref = next(it)
    jj = pl.program_id(0)
    i = pl.program_id(1)
    nj = pl.num_programs(0) - 1

    if side_cast:
        @pl.when((jj > 0) & (i < ntiles[0]))
        def _():
            side_out[...] = side_in[...].astype(side_out.dtype)

    @pl.when((jj < nj) & (i < npieces))
    def _():
        piece = wp_ref[0].astype(_BF16)
        rows = piece.shape[0]
        wb_ref[jj % 2, pl.ds(pl.multiple_of(i * rows, rows), rows), :] = piece

    for gi, g in enumerate(groups):
        active = (jj > 0) & (i >= starts[gi]) & (i < starts[gi] + ntiles[gi])

        @pl.when(active)
        def _(g=g):
            slot = (jj + 1) % 2
            tn = wb_ref.shape[1] if w_t else wb_ref.shape[2]
            sub = min(tn, SUB_TN)
            for c0 in range(0, tn, sub):
                if w_t:
                    z = lax.dot_general(g["x"][...], wb_ref[slot, c0:c0 + sub, :], _NT,
                                        preferred_element_type=_F32)
                else:
                    z = jnp.dot(g["x"][...], wb_ref[slot, :, c0:c0 + sub],
                                preferred_element_type=_F32)
                if rope_blocks is None:
                    _epilogue(g, z, c0, epi=epi, rope_half=rope_half, scale=scale)
                else:
                    @pl.when(jj <= rope_blocks)
                    def _():
                        _epilogue(g, z, c0, epi=epi, rope_half=rope_half, scale=scale)

                    @pl.when(jj > rope_blocks)
                    def _():
                        _epilogue(g, z, c0, epi=epi, rope_half=0, scale=1.0)


def _mm_kt_kernel(*refs, starts, ntiles):
    it = iter(refs)
    w_ref = next(it)
    groups = _unpack_groups(it, len(starts), 0, "residual", [False] * len(starts))
    i = pl.program_id(1)
    k = pl.program_id(2)
    for gi, g in enumerate(groups):
        active = (i >= starts[gi]) & (i < starts[gi] + ntiles[gi])

        @pl.when(active)
        def _(g=g):
            @pl.when(k == 0)
            def _():
                g["o"][...] = g["res"][...]

            tm = g["x"].shape[0]
            sub = min(tm, SUB_TM)
            for r0 in range(0, tm, sub):
                g["o"][r0:r0 + sub, :] += jnp.dot(g["x"][r0:r0 + sub, :], w_ref[0].astype(_BF16),
                                                   preferred_element_type=_F32)


def _matmul(xs, w, layer, *, col_start, n_cols, out_dtype, tms, epi="none", rope=None,
            scale=1.0, residuals=None, pair_major=None, tn=TN, tk=None, w_t=False,
            rope_blocks=None, side_cast=None):
    kdim = w.shape[2] if w_t else w.shape[1]
    tn = min(tn, n_cols)
    nj = n_cols // tn
    cb0 = col_start // tn
    assert n_cols % tn == 0
    assert col_start % (SUBLANES if w_t else tn) == 0
    ng = len(xs)
    pair_major = [False] * ng if pair_major is None else pair_major
    ntiles = [x.shape[0] // tm for x, tm in zip(xs, tms)]
    starts = [int(s) for s in np.concatenate([[0], np.cumsum(ntiles)[:-1]])]
    ni = sum(ntiles)
    ws = tk is None
    if ws:
        tk = kdim
        npieces = max(p for p in (8, 4, 2, 1) if p <= ni)
    else:
        assert epi == "residual" and out_dtype == _F32 and not w_t and kdim % tk == 0
        assert rope is None and not any(pair_major)

    def tile(gi):
        return lambda i: jnp.clip(i - starts[gi], 0, ntiles[gi] - 1)

    if ws:
        col = lambda a: jnp.maximum(a[0] - 1, 0)
        krow = lambda a: 0
        rowsel = lambda a, t: jnp.where(a[0] > 0, t(a[1]), 0)
        nxt = lambda a: jnp.minimum(a[0], nj - 1)
        pc = lambda a: jnp.minimum(a[1], npieces - 1)
        if w_t:
            pr = tn // npieces
            w_spec = pl.BlockSpec(
                (pl.Element(1), pl.Element(pr), pl.Element(tk)),
                lambda *a: (layer, pl.multiple_of(col_start + nxt(a) * tn + pc(a) * pr, SUBLANES),
                            0))
        else:
            w_spec = pl.BlockSpec((1, tk // npieces, tn), lambda *a: (layer, pc(a), cb0 + nxt(a)))
    else:
        col = lambda a: a[0]
        krow = lambda a: a[2]
        rowsel = lambda a, t: t(a[1])
        w_spec = pl.BlockSpec((1, tk, tn), lambda *a: (layer, a[2], cb0 + a[0]))

    in_specs, args = [w_spec], [w]
    if side_cast is not None:
        side, side_layer = side_cast
        n_side = nj * ntiles[0]
        sr = side.shape[1] // n_side
        assert ws and starts[0] == 0 and side.shape[1] % n_side == 0 and sr % SUBLANES == 0
        side_blk = lambda a: jnp.clip((a[0] - 1) * ntiles[0] + jnp.minimum(a[1], ntiles[0] - 1),
                                      0, n_side - 1)
        in_specs.append(pl.BlockSpec((1, sr, side.shape[2]),
                                     lambda *a: (side_layer, side_blk(a), 0)))
        args.append(side)
    rope_half = 0
    for gi in range(ng):
        tm, t = tms[gi], tile(gi)
        in_specs.append(pl.BlockSpec((tm, tk), lambda *a, t=t: (rowsel(a, t), krow(a))))
        args.append(xs[gi])
        if rope is not None:
            rope_half = rope[0]
            cos, sin = rope[1][gi]
            nper = cos.shape[0] // tm
            tspec = pl.BlockSpec((tm, LANES),
                                 lambda *a, t=t, nper=nper: (rowsel(a, t) % nper, 0))
            in_specs += [tspec, tspec]
            args += [cos, sin]
        if epi == "residual":
            in_specs.append(pl.BlockSpec((tm, tn), lambda *a, t=t: (rowsel(a, t), col(a))))
            args.append(residuals[gi])
    out_shapes, out_specs = [], []
    for gi in range(ng):
        tm, t, m = tms[gi], tile(gi), xs[gi].shape[0]
        if pair_major[gi]:
            out_shapes.append(jax.ShapeDtypeStruct((n_cols // LANES, m, LANES), out_dtype))
            out_specs.append(pl.BlockSpec((tn // LANES, tm, LANES),
                                          lambda *a, t=t: (col(a), rowsel(a, t), 0)))
        else:
            out_shapes.append(jax.ShapeDtypeStruct((m, n_cols), out_dtype))
            out_specs.append(pl.BlockSpec((tm, tn), lambda *a, t=t: (rowsel(a, t), col(a))))
    if side_cast is not None:
        out_shapes.append(jax.ShapeDtypeStruct((1,) + side.shape[1:], _BF16))
        out_specs.append(pl.BlockSpec((1, sr, side.shape[2]), lambda *a: (0, side_blk(a), 0)))
    if ws:
        body = functools.partial(_mm_ws_kernel, starts=starts, ntiles=ntiles, npieces=npieces,
                                 epi=epi, rope_half=rope_half, rope_blocks=rope_blocks,
                                 scale=scale, pair_major=tuple(pair_major), w_t=w_t,
                                 side_cast=side_cast is not None)
        grid = (nj + 1, ni)
        scratch = [pltpu.VMEM((2, tn, tk) if w_t else (2, tk, tn), _BF16)]
    else:
        body = functools.partial(_mm_kt_kernel, starts=starts, ntiles=ntiles)
        grid = (nj, ni, kdim // tk)
        scratch = []
    return pl.pallas_call(
        body,
        grid=grid,
        in_specs=in_specs,
        out_specs=out_specs,
        out_shape=out_shapes,
        scratch_shapes=scratch,
        compiler_params=_cparams(("arbitrary",) * len(grid)),
        name="matmul_" + epi + ("_rope%d" % rope_half if rope_half else ""),
    )(*args)


def _sortable(x):
    bits = pltpu.bitcast(x, jnp.int32)
    return jnp.where(bits < 0, bits ^ jnp.int32(0x7FFFFFFF), bits)


def _kth_largest(keys, n_sel, count_fn):
    cnt = count_fn((keys >= 0).astype(jnp.int32))
    thr0 = jnp.where(cnt >= n_sel, jnp.int32(0), jnp.int32(INT_MIN))

    def body(i, thr):
        cand = thr | jnp.left_shift(jnp.int32(1), 30 - i)
        cnt = count_fn((keys >= cand).astype(jnp.int32))
        return jnp.where(cnt >= n_sel, cand, thr)

    return lax.fori_loop(0, 31, body, thr0)


def _select_topk(keys, idx, n_sel, count_fn, idx_bits):
    thr = _kth_largest(keys, n_sel, count_fn)
    gt = keys > thr
    eq = keys == thr
    need = n_sel - count_fn(gt.astype(jnp.int32))
    surplus = count_fn(eq.astype(jnp.int32)) - need

    def tie_cut():
        def body(b, lo):
            cand = lo + jnp.left_shift(jnp.int32(1), idx_bits - 1 - b)
            below = count_fn((eq & (idx < cand)).astype(jnp.int32))
            return jnp.where(below < need, cand, lo)

        return lax.fori_loop(0, idx_bits, body, jnp.zeros_like(thr))

    cut = lax.cond(jnp.max(surplus) > 0, tie_cut,
                   lambda: jnp.full_like(thr, jnp.iinfo(jnp.int32).max))
    return gt | (eq & (idx <= cut))


def _pattn_kernel(qi_ref, wit_ref, ki_ref, q_ref, k_ref, v_ref, o_ref, sc_ref, bias_ref, *,
                  n_sel, kv_rep):
    jq = pl.program_id(1)
    g = pl.program_id(2)
    s_len, tq = sc_ref.shape
    n_pairs = qi_ref.shape[0]

    def select(klen):
        key_pos = lax.broadcasted_iota(jnp.int32, (klen, tq), 0)
        q_pos = jq * tq + lax.broadcasted_iota(jnp.int32, (klen, tq), 1)
        adm = key_pos <= q_pos
        if klen <= n_sel:
            bias_ref[:, 0:klen] = jnp.where(adm, 0.0, NEG).astype(_F32).T
            return
        kb = ki_ref[0:klen, :].astype(_BF16)
        zeros = jnp.zeros_like(kb)
        k_even = jnp.concatenate([kb, zeros], axis=1)
        k_odd = jnp.concatenate([zeros, kb], axis=1)
        w_scale = (IDX_HEADS ** -0.5) * (IDX_DIM ** -0.5)
        sc_ref[0:klen, :] = jnp.zeros((klen, tq), _F32)

        def body(p, carry):
            x = qi_ref[p]
            d0 = lax.dot_general(k_even, x, _NT, preferred_element_type=_F32)
            d1 = lax.dot_general(k_odd, x, _NT, preferred_element_type=_F32)
            w0 = wit_ref[pl.ds(2 * p, 1), :] * w_scale
            w1 = wit_ref[pl.ds(2 * p + 1, 1), :] * w_scale
            sc_ref[0:klen, :] += jnp.maximum(d0, 0.0) * w0 + jnp.maximum(d1, 0.0) * w1
            return carry

        lax.fori_loop(0, n_pairs, body, 0)

        keys = _sortable(jnp.where(adm, sc_ref[0:klen, :], NEG))
        sel = _select_topk(keys, key_pos, n_sel, lambda c: jnp.sum(c, axis=0, keepdims=True),
                           (s_len - 1).bit_length()) & adm
        bias_ref[:, 0:klen] = jnp.where(sel, 0.0, NEG).astype(_F32).T

    def attend(klen):
        kg = k_ref[0:klen, :].astype(_BF16)
        vg = v_ref[0:klen, :].astype(_BF16)
        bias = bias_ref[:, 0:klen]
        for r in range(kv_rep):
            qh = q_ref[:, r * HEAD_DIM:(r + 1) * HEAD_DIM]
            s = lax.dot_general(qh, kg, _NT, preferred_element_type=_F32) + bias
            m = jnp.max(s, axis=1, keepdims=True)
            p = jnp.exp(s - m)
            l = jnp.sum(p, axis=1, keepdims=True)
            o = jnp.dot(p.astype(_BF16), vg, preferred_element_type=_F32) / l
            o_ref[:, r * HEAD_DIM:(r + 1) * HEAD_DIM] = o.astype(o_ref.dtype)

    for n in range(1, s_len // tq + 1):
        @pl.when(jq == n - 1)
        def _(klen=n * tq):
            @pl.when(g == 0)
            def _():
                select(klen)

            attend(klen)


def _prompt_attention(qi_pm, wit, ki, q, kv, *, batch, seq):
    m, d = q.shape
    n_kv = kv.shape[1] // (2 * HEAD_DIM)
    kv_rep = (d // HEAD_DIM) // n_kv
    tq = min(seq, 256)
    nq = seq // tq
    n_sel = min(TOPK_MAX, seq // 4)
    n_pairs = qi_pm.shape[0]
    gw = kv_rep * HEAD_DIM
    return pl.pallas_call(
        functools.partial(_pattn_kernel, n_sel=n_sel, kv_rep=kv_rep),
        grid=(batch, nq, n_kv),
        in_specs=[
            pl.BlockSpec((n_pairs, tq, LANES), lambda b, j, g: (0, b * nq + j, 0)),
            pl.BlockSpec((wit.shape[0], tq), lambda b, j, g: (0, b * nq + j)),
            pl.BlockSpec((seq, ki.shape[1]), lambda b, j, g: (b, 0)),
            pl.BlockSpec((tq, gw), lambda b, j, g: (b * nq + j, g)),
            pl.BlockSpec((seq, HEAD_DIM), lambda b, j, g: (b, g)),
            pl.BlockSpec((seq, HEAD_DIM), lambda b, j, g: (b, n_kv + g)),
        ],
        out_specs=pl.BlockSpec((tq, gw), lambda b, j, g: (b * nq + j, g)),
        out_shape=jax.ShapeDtypeStruct((m, d), _BF16),
        scratch_shapes=[pltpu.VMEM((seq, tq), _F32), pltpu.VMEM((tq, seq), _F32)],
        compiler_params=_cparams(("arbitrary", "arbitrary", "arbitrary")),
        name="prompt_attention",
    )(qi_pm, wit, ki, q, kv, kv)


def _sidx_kernel(pt_ref, qi_ref, w_ref, kin_ref, *rest, n_sel, t_real, n_chunks):
    pages = rest[:PAGES_PER_STEP]
    bias_ref, sc_ref = rest[PAGES_PER_STEP], rest[PAGES_PER_STEP + 1]
    c = pl.program_id(1)
    rows = SUBLANES
    chunk = PAGES_PER_STEP * PAGE_SIZE
    w = w_ref[0] * ((IDX_HEADS ** -0.5) * (IDX_DIM ** -0.5))
    qi = qi_ref[0]

    def scores(d):
        d = jnp.maximum(d, 0.0) * w
        acc = d[0:rows]
        for h in range(1, IDX_HEADS):
            acc = acc + d[h * rows:(h + 1) * rows]
        return acc

    @pl.when(c < n_chunks)
    def _():
        keys_t = jnp.concatenate([p[0] for p in pages], axis=1).astype(_BF16)
        sc_ref[c] = scores(jnp.dot(qi, keys_t, preferred_element_type=_F32))

    @pl.when(c == n_chunks)
    def _():
        kn = kin_ref[...].astype(_BF16)
        kn = jnp.concatenate([kn, jnp.zeros((PAGE_SIZE - rows, kn.shape[1]), _BF16)], axis=0)
        s_new = scores(lax.dot_general(qi, kn, _NT, preferred_element_type=_F32))
        s_new = jnp.concatenate([s_new, jnp.zeros((rows, chunk - PAGE_SIZE), _F32)], axis=1)
        shape = (n_chunks + 1, rows, chunk)
        ci = lax.broadcasted_iota(jnp.int32, shape, 0)
        qrow = lax.broadcasted_iota(jnp.int32, shape, 1)
        lane = lax.broadcasted_iota(jnp.int32, shape, 2)
        adm = (ci < n_chunks) | ((lane <= qrow) & (lane < t_real))
        sc_ref[n_chunks] = s_new
        keys = _sortable(jnp.where(adm, sc_ref[...], NEG))
        count = lambda x: jnp.sum(jnp.sum(x, axis=0, keepdims=True), axis=2, keepdims=True)
        key_idx = ci * chunk + lane
        n_keys = (n_chunks + 1) * chunk
        sel = _select_topk(keys, key_idx, n_sel, count, (n_keys - 1).bit_length()) & adm
        bias_ref[0] = jnp.where(sel, 0.0, NEG).astype(_F32)


def _page_spec(block, n_chunks, page0, i):
    zeros = (0,) * (len(block) - 1)
    return pl.BlockSpec(
        block,
        lambda b, c, pt: (page0 + pt[b, jnp.minimum(c, n_chunks - 1) * PAGES_PER_STEP + i],)
        + zeros)


def _sample_index(page_table, qi_hm, w_hm, ki_new, pool_ki, page0, row0, *, t_real):
    bd, n_pages = page_table.shape
    n_chunks = n_pages // PAGES_PER_STEP
    chunk = PAGES_PER_STEP * PAGE_SIZE
    past = n_pages * PAGE_SIZE
    n_sel = min(TOPK_MAX, (past + t_real) // 4)
    hr = qi_hm.shape[1]
    grid_spec = pltpu.PrefetchScalarGridSpec(
        num_scalar_prefetch=1,
        grid=(bd, n_chunks + 1),
        in_specs=[pl.BlockSpec((1, hr, IDX_DIM), lambda b, c, pt: (b, 0, 0)),
                  pl.BlockSpec((1, hr, 1), lambda b, c, pt: (b, 0, 0)),
                  pl.BlockSpec((SUBLANES, IDX_DIM), lambda b, c, pt: (row0 + b, 0))]
                 + [_page_spec((1, IDX_DIM, PAGE_SIZE), n_chunks, page0, i)
                    for i in range(PAGES_PER_STEP)],
        out_specs=pl.BlockSpec((1, n_chunks + 1, SUBLANES, chunk), lambda b, c, pt: (b, 0, 0, 0)),
        scratch_shapes=[pltpu.VMEM((n_chunks + 1, SUBLANES, chunk), _F32)],
    )
    return pl.pallas_call(
        functools.partial(_sidx_kernel, n_sel=n_sel, t_real=t_real, n_chunks=n_chunks),
        grid_spec=grid_spec,
        out_shape=jax.ShapeDtypeStruct((bd, n_chunks + 1, SUBLANES, chunk), _F32),
        compiler_params=_cparams(("arbitrary", "arbitrary")),
        name="sample_index",
    )(page_table, qi_hm, w_hm, ki_new, *([pool_ki] * PAGES_PER_STEP))


def _sattn_kernel(pt_ref, q_ref, bias_ref, kn_ref, vn_ref, *rest, n_chunks, n_kv, kv_rep):
    kpages = rest[:PAGES_PER_STEP]
    vpages = rest[PAGES_PER_STEP:2 * PAGES_PER_STEP]
    o_ref, m_ref, l_ref, acc_ref = rest[2 * PAGES_PER_STEP:]
    c = pl.program_id(1)
    rows = SUBLANES

    @pl.when(c == 0)
    def _():
        m_ref[...] = jnp.full_like(m_ref, -jnp.inf)
        l_ref[...] = jnp.zeros_like(l_ref)
        acc_ref[...] = jnp.zeros_like(acc_ref)

    gr = kv_rep * rows

    def update(kgs, vgs, bias):
        parts = []
        for g in range(n_kv):
            qg = jnp.concatenate(
                [q_ref[:, (g * kv_rep + r) * HEAD_DIM:(g * kv_rep + r + 1) * HEAD_DIM]
                 for r in range(kv_rep)], axis=0).astype(_BF16)
            parts.append(lax.dot_general(qg, kgs[g], _NT, preferred_element_type=_F32))
        s = jnp.concatenate(parts, axis=0) + jnp.concatenate([bias] * (n_kv * kv_rep), axis=0)
        m_old = m_ref[...]
        m_new = jnp.maximum(m_old, jnp.max(s, axis=1, keepdims=True))
        alpha = jnp.exp(m_old - m_new)
        p = jnp.exp(s - m_new)
        l_ref[...] = alpha * l_ref[...] + jnp.sum(p, axis=1, keepdims=True)
        pb = p.astype(_BF16)
        pv = jnp.concatenate(
            [jnp.dot(pb[g * gr:(g + 1) * gr], vgs[g], preferred_element_type=_F32)
             for g in range(n_kv)], axis=0)
        acc_ref[...] = alpha * acc_ref[...] + pv
        m_ref[...] = m_new

    @pl.when(c < n_chunks)
    def _():
        def head_rows(pages, g):
            rows_g = pl.ds(g, PAGE_SIZE, stride=n_kv)
            return jnp.concatenate([p[rows_g, :] for p in pages], axis=0).astype(_BF16)

        update([head_rows(kpages, g) for g in range(n_kv)],
               [head_rows(vpages, g) for g in range(n_kv)], bias_ref[0, 0])

    @pl.when(c == n_chunks)
    def _():
        pad = jnp.zeros((PAGE_SIZE - rows, HEAD_DIM), _BF16)

        def new_rows(ref, g):
            return jnp.concatenate(
                [ref[:, g * HEAD_DIM:(g + 1) * HEAD_DIM].astype(_BF16), pad], axis=0)

        update([new_rows(kn_ref, g) for g in range(n_kv)],
               [new_rows(vn_ref, g) for g in range(n_kv)], bias_ref[0, 0][:, :PAGE_SIZE])
        o = acc_ref[...] / l_ref[...]
        for h in range(n_kv * kv_rep):
            o_ref[:, h * HEAD_DIM:(h + 1) * HEAD_DIM] = (
                o[h * rows:(h + 1) * rows].astype(o_ref.dtype))


def _sample_attention(page_table, q, bias, kv_new, pool_k, pool_v, page0):
    bd, n_pages = page_table.shape
    n_chunks = n_pages // PAGES_PER_STEP
    chunk = PAGES_PER_STEP * PAGE_SIZE
    m, d = q.shape
    kvw = kv_new.shape[1] // 2
    n_kv = kvw // HEAD_DIM
    kv_rep = (d // HEAD_DIM) // n_kv
    grid_spec = pltpu.PrefetchScalarGridSpec(
        num_scalar_prefetch=1,
        grid=(bd, n_chunks + 1),
        in_specs=[pl.BlockSpec((SUBLANES, d), lambda b, c, pt: (b, 0)),
                  pl.BlockSpec((1, 1, SUBLANES, chunk), lambda b, c, pt: (b, c, 0, 0)),
                  pl.BlockSpec((SUBLANES, kvw), lambda b, c, pt: (b, 0)),
                  pl.BlockSpec((SUBLANES, kvw), lambda b, c, pt: (b, 1))]
                 + [_page_spec((PAGE_SIZE * n_kv, HEAD_DIM), n_chunks, page0, i)
                    for i in range(PAGES_PER_STEP)] * 2,
        out_specs=pl.BlockSpec((SUBLANES, d), lambda b, c, pt: (b, 0)),
        scratch_shapes=[pltpu.VMEM((n_kv * kv_rep * SUBLANES, 1), _F32),
                        pltpu.VMEM((n_kv * kv_rep * SUBLANES, 1), _F32),
                        pltpu.VMEM((n_kv * kv_rep * SUBLANES, HEAD_DIM), _F32)],
    )
    return pl.pallas_call(
        functools.partial(_sattn_kernel, n_chunks=n_chunks, n_kv=n_kv, kv_rep=kv_rep),
        grid_spec=grid_spec,
        out_shape=jax.ShapeDtypeStruct((m, d), _F32),
        compiler_params=_cparams(("arbitrary", "arbitrary")),
        name="sample_attention",
    )(page_table, q, bias, kv_new, kv_new,
      *([pool_k] * PAGES_PER_STEP), *([pool_v] * PAGES_PER_STEP))


N_GATE = 5


def _gm_kernel(*refs, starts, ntiles, tms, seq_lens, has_state, cw):
    ng = len(starts)
    it = iter(refs)
    wp_ref, taps_ref = next(it), next(it)
    groups = []
    for gi in range(ng):
        g = dict(x=next(it), attn=next(it))
        if has_state[gi]:
            g["s0"], g["s1"] = next(it), next(it)
        groups.append(g)
    for g in groups:
        g["o"], g["tail"] = next(it), next(it)
    wb_ref = next(it)
    for g in groups:
        g["u"] = next(it)
    jj = pl.program_id(0)
    i = pl.program_id(1)
    nj = pl.num_programs(0) - 1

    @pl.when((jj < nj) & (i < N_GATE))
    def _():
        wb_ref[jj % 2, pl.ds(pl.multiple_of(i * cw, cw), cw), :] = wp_ref[0].astype(_BF16)

    for gi, g in enumerate(groups):
        tm, seq = tms[gi], seq_lens[gi]
        active = (jj > 0) & (i >= starts[gi]) & (i < starts[gi] + ntiles[gi])

        @pl.when(active)
        def _(g=g, gi=gi, tm=tm, seq=seq):
            u_ref = g["u"]
            first = i == starts[gi]

            @pl.when(first)
            def _():
                u_ref[0:SUBLANES, :] = jnp.zeros((SUBLANES, cw), _F32)

            @pl.when(jnp.logical_not(first))
            def _():
                u_ref[0:SUBLANES, :] = u_ref[tm:tm + SUBLANES, :]

            sub = min(tm, SUB_TM)
            for r0 in range(0, tm, sub):
                rs = slice(r0, r0 + sub)
                z = lax.dot_general(g["x"][rs, :], wb_ref[(jj + 1) % 2], _NT,
                                    preferred_element_type=_F32)
                cb, cc, ch, ga, gc = [z[:, s * cw:(s + 1) * cw] for s in range(N_GATE)]
                u = cc * ch
                u_ref[SUBLANES + r0:SUBLANES + r0 + sub, :] = u
                t = ((i - starts[gi]) * tm + r0
                     + lax.broadcasted_iota(jnp.int32, (sub, 1), 0)) % seq
                if "s0" in g:
                    s0, s1 = g["s0"][rs, :], g["s1"][rs, :]
                else:
                    s0 = s1 = jnp.zeros((sub, cw), _F32)
                h1 = SUBLANES - 1 + r0
                u1 = jnp.where(t >= 1, u_ref[h1:h1 + sub, :], s1)
                u2 = jnp.where(t >= 2, u_ref[h1 - 1:h1 - 1 + sub, :], jnp.where(t == 1, s1, s0))
                y = u * taps_ref[2:3, :] + u1 * taps_ref[1:2, :] + u2 * taps_ref[0:1, :]
                merged = (jax.nn.sigmoid(ga) * g["attn"][rs, :].astype(_F32)
                          + jax.nn.sigmoid(gc) * (cb * y))
                g["o"][rs, :] = merged.astype(g["o"].dtype)
            if tm >= seq:
                for s in range(tm // seq):
                    g["tail"][s] = u_ref[(s + 1) * seq:(s + 1) * seq + SUBLANES, :]
            else:
                g["tail"][0] = u_ref[tm:tm + SUBLANES, :]


def _gate_merge(xs, w_t, layer, attns, states, taps, *, col_start, tms, seq_lens, out_dtypes):
    assert CONV_WIDTH == 3
    kdim = w_t.shape[2]
    d = attns[0].shape[1]
    cw = min(d, GATE_CW)
    nj = d // cw
    ng = len(xs)
    ntiles = [x.shape[0] // tm for x, tm in zip(xs, tms)]
    starts = [int(s) for s in np.concatenate([[0], np.cumsum(ntiles)[:-1]])]
    ni = sum(ntiles)
    assert ni >= N_GATE and col_start % SUBLANES == 0 and d % cw == 0

    def tile(gi):
        return lambda i: jnp.clip(i - starts[gi], 0, ntiles[gi] - 1)

    col = lambda jj: jnp.maximum(jj - 1, 0)
    rowsel = lambda jj, i, t: jnp.where(jj > 0, t(i), 0)
    nxt = lambda jj: jnp.minimum(jj, nj - 1)
    seg = lambda i: jnp.minimum(i, N_GATE - 1)
    in_specs = [
        pl.BlockSpec((pl.Element(1), pl.Element(cw), pl.Element(kdim)),
                     lambda jj, i: (layer, pl.multiple_of(col_start + seg(i) * d + nxt(jj) * cw,
                                                          SUBLANES), 0)),
        pl.BlockSpec((CONV_WIDTH, cw), lambda jj, i: (0, col(jj)))]
    args = [w_t, taps]
    for gi in range(ng):
        tm, t = tms[gi], tile(gi)
        in_specs.append(pl.BlockSpec((tm, kdim), lambda jj, i, t=t: (rowsel(jj, i, t), 0)))
        act = pl.BlockSpec((tm, cw), lambda jj, i, t=t: (rowsel(jj, i, t), col(jj)))
        in_specs.append(act)
        args += [xs[gi], attns[gi]]
        if states[gi] is not None:
            in_specs += [act, act]
            args += list(states[gi])
    out_shapes, out_specs = [], []
    for gi in range(ng):
        tm, t, m, seq = tms[gi], tile(gi), xs[gi].shape[0], seq_lens[gi]
        spt, tps = max(tm // seq, 1), max(seq // tm, 1)
        out_shapes += [jax.ShapeDtypeStruct((m, d), out_dtypes[gi]),
                       jax.ShapeDtypeStruct((m // seq, SUBLANES, d), _F32)]
        out_specs += [
            pl.BlockSpec((tm, cw), lambda jj, i, t=t: (rowsel(jj, i, t), col(jj))),
            pl.BlockSpec((spt, SUBLANES, cw),
                         lambda jj, i, t=t, tps=tps: (rowsel(jj, i, t) // tps, 0, col(jj)))]
    scratch = [pltpu.VMEM((2, N_GATE * cw, kdim), _BF16)]
    scratch += [pltpu.VMEM((tm + SUBLANES, cw), _F32) for tm in tms]
    res = pl.pallas_call(
        functools.partial(_gm_kernel, starts=starts, ntiles=ntiles, tms=tuple(tms),
                          seq_lens=tuple(seq_lens),
                          has_state=tuple(s is not None for s in states), cw=cw),
        grid=(nj + 1, ni),
        in_specs=in_specs,
        out_specs=out_specs,
        out_shape=out_shapes,
        scratch_shapes=scratch,
        compiler_params=_cparams(("arbitrary", "arbitrary")),
        name="gate_merge",
    )(*args)
    return [(res[2 * gi], res[2 * gi + 1]) for gi in range(ng)]


def _rope_tables(positions, half, n_rot_lanes=LANES):
    pos = np.asarray(positions, np.float64)[:, None]
    lane = np.arange(LANES)
    inv = ROPE_THETA ** (-(lane % half).astype(np.float64) / half)
    ang = pos * inv[None, :]
    sign = np.where((lane % (2 * half)) < half, -1.0, 1.0)
    rot = (lane < n_rot_lanes)[None, :]
    cos = np.where(rot, np.cos(ang), 1.0)
    sin = np.where(rot, np.sin(ang) * sign[None, :], 0.0)
    return jnp.asarray(cos, _F32), jnp.asarray(sin, _F32)


def kernel(x_prompt, x_sample, cache_k, cache_v, cache_kidx, state_conv, page_table,
           w_in, conv_w, w_o, g_mix, g_mlp, w_up, w_down, g_final):
    B, S, D = x_prompt.shape
    Bd, T, _ = x_sample.shape
    depth = w_in.shape[0]
    n_pool = cache_k.shape[1]
    past = page_table.shape[1] * PAGE_SIZE
    R = SUBLANES
    kvw = N_KV_HEADS * HEAD_DIM
    half, ihalf = HEAD_DIM // 2, IDX_DIM // 2
    Mp, Ms = B * S, Bd * R
    tms = [min(Mp, 1024, S // 2), Ms]
    tms_down = [min(Mp, 2048), Ms]

    c_q, c_k = 0, N_HEADS * HEAD_DIM
    c_v = c_k + kvw
    c_qi = c_v + kvw
    c_kiwi = c_qi + IDX_HEADS * IDX_DIM
    c_gate = c_kiwi + IDX_DIM + IDX_HEADS

    pos_p = np.arange(max(S, tms[0])) % S
    pos_s = np.tile(past + np.arange(R), Bd)
    r128 = [_rope_tables(pos_p, half), _rope_tables(pos_s, half)]
    r64 = [_rope_tables(pos_p, ihalf), _rope_tables(pos_s, ihalf)]
    rkiwi = [_rope_tables(pos_p, ihalf, IDX_DIM), _rope_tables(pos_s, ihalf, IDX_DIM)]

    xp = x_prompt.reshape(Mp, D)
    xs = jnp.pad(x_sample, ((0, 0), (0, R - T), (0, 0))).reshape(Ms, D)
    w_in_t = jnp.swapaxes(w_in, 1, 2)
    pool_k = cache_k.reshape(depth * n_pool * PAGE_SIZE * N_KV_HEADS, HEAD_DIM)
    pool_v = cache_v.reshape(depth * n_pool * PAGE_SIZE * N_KV_HEADS, HEAD_DIM)
    pool_ki_t = jnp.swapaxes(cache_kidx, 2, 3).reshape(depth * n_pool, IDX_DIM, PAGE_SIZE)

    outs = {n: [] for n in ("kp", "vp", "kip", "cp", "ks", "vs", "kis", "cs")}
    for l in range(depth):
        hs = [_rmsnorm(xp, g_mix[l], _BF16), _rmsnorm(xs, g_mix[l], _BF16)]
        mm = functools.partial(_matmul, hs, w_in_t, l, tms=tms, w_t=True)
        qp, qs = mm(col_start=c_q, n_cols=N_HEADS * HEAD_DIM, out_dtype=_BF16,
                    rope=(half, r128), scale=HEAD_DIM ** -0.5)
        kvp, kvs = mm(col_start=c_k, n_cols=2 * kvw, out_dtype=_F32, rope=(half, r128),
                      tn=min(TN, kvw), rope_blocks=kvw // min(TN, kvw))
        qip, qis = mm(col_start=c_qi, n_cols=IDX_HEADS * IDX_DIM, out_dtype=_BF16,
                      rope=(ihalf, r64), pair_major=[True, False])
        kwp, kws = mm(col_start=c_kiwi, n_cols=LANES, out_dtype=_F32, rope=(ihalf, rkiwi))
        page0 = l * n_pool

        kip = kwp[:, :IDX_DIM]
        wit = kwp[:, IDX_DIM:IDX_DIM + IDX_HEADS].T
        attn_p = _prompt_attention(qip, wit, kip, qp, kvp, batch=B, seq=S)

        qi_hm = qis.reshape(Bd, R, IDX_HEADS, IDX_DIM).transpose(0, 2, 1, 3).reshape(
            Bd, IDX_HEADS * R, IDX_DIM)
        kis = kws[:, :IDX_DIM]
        w_hm = kws[:, IDX_DIM:IDX_DIM + IDX_HEADS].reshape(Bd, R, IDX_HEADS).transpose(
            0, 2, 1).reshape(Bd, IDX_HEADS * R, 1)
        bias = _sample_index(page_table, qi_hm, w_hm, kis, pool_ki_t, page0, 0, t_real=T)
        attn_s = _sample_attention(page_table, qs.astype(_F32), bias, kvs, pool_k, pool_v, page0)

        s0, s1 = [jnp.repeat(state_conv[l][:, r:r + 1, :], R, axis=1).reshape(Ms, D)
                  for r in range(CONV_WIDTH - 1)]
        (mg_p, tail_p), (mg_s, tail_s) = _gate_merge(
            hs, w_in_t, l, [attn_p, attn_s], [None, (s0, s1)], conv_w[l], col_start=c_gate,
            tms=tms, seq_lens=[S, R], out_dtypes=[_BF16, _F32])
        xp, xs = _matmul([mg_p, mg_s.astype(_BF16)], w_o, l, col_start=0, n_cols=D,
                         out_dtype=_F32, tms=tms, epi="residual", residuals=[xp, xs])
        h2 = [_rmsnorm(xp, g_mlp[l], _BF16), _rmsnorm(xs, g_mlp[l], _BF16)]
        hid_p, hid_s, w_down_bf16 = _matmul(h2, w_up, l, col_start=0, n_cols=w_up.shape[2],
                                            out_dtype=_BF16, tms=tms, epi="relu2",
                                            side_cast=(w_down, l))
        xp, xs = _matmul([hid_p, hid_s], w_down_bf16, 0, col_start=0, n_cols=D, out_dtype=_F32,
                         tms=tms_down, epi="residual", residuals=[xp, xs], tn=min(D, 1024),
                         tk=min(w_down.shape[1], 2048))

        outs["kp"].append(kvp[:, :kvw].reshape(B, S, N_KV_HEADS, HEAD_DIM))
        outs["vp"].append(kvp[:, kvw:].reshape(B, S, N_KV_HEADS, HEAD_DIM))
        outs["kip"].append(kip.reshape(B, S, IDX_DIM))
        outs["cp"].append(tail_p[:, R - (CONV_WIDTH - 1):, :])
        outs["ks"].append(kvs[:, :kvw].reshape(Bd, R, N_KV_HEADS, HEAD_DIM)[:, :T])
        outs["vs"].append(kvs[:, kvw:].reshape(Bd, R, N_KV_HEADS, HEAD_DIM)[:, :T])
        outs["kis"].append(kis.reshape(Bd, R, IDX_DIM)[:, :T])
        outs["cs"].append(tail_s[:, T - (CONV_WIDTH - 1):T, :])

    y_prompt = _rmsnorm(xp, g_final, _F32).reshape(B, S, D)
    y_sample = _rmsnorm(xs, g_final, _F32).reshape(Bd, R, D)[:, :T]
    st = lambda n: jnp.stack(outs[n])
    return (y_prompt, y_sample, st("kp"), st("vp"), st("kip"), st("cp"),
            st("ks"), st("vs"), st("kis"), st("cs"))
```

```python
import functools

import numpy as np
import jax
import jax.numpy as jnp
from jax import lax
from jax.experimental import pallas as pl
from jax.experimental.pallas import tpu as pltpu

HEAD_DIM = 128
N_HEADS = 32
N_KV_HEADS = 8
IDX_HEADS = 32
IDX_DIM = 64
TOPK_MAX = 256
CONV_WIDTH = 3
PAGE_SIZE = 128
ROPE_THETA = 10000.0
NORM_EPS = 1e-6
NEG = -1e30

LANES = 128
SUBLANES = 8
VMEM_LIMIT = 60 * 1024 * 1024
PAGES_PER_STEP = 16
INT_MIN = -2 ** 31
TN = 1024
SUB_TN = 512
SUB_TM = 512
GATE_CW = 256

_F32 = jnp.float32
_BF16 = jnp.bfloat16
_NT = (((1,), (1,)), ((), ()))


def _cparams(sem):
    return pltpu.CompilerParams(dimension_semantics=sem, vmem_limit_bytes=VMEM_LIMIT)


def _rmsnorm_kernel(x_ref, g_ref, o_ref):
    x = x_ref[...]
    y = x * lax.rsqrt(jnp.mean(x * x, axis=-1, keepdims=True) + NORM_EPS)
    o_ref[...] = (y * g_ref[...]).astype(o_ref.dtype)


def _rmsnorm(x, g, out_dtype):
    m, d = x.shape
    tm = min(m, 256)
    return pl.pallas_call(
        _rmsnorm_kernel,
        grid=(m // tm,),
        in_specs=[pl.BlockSpec((tm, d), lambda i: (i, 0)),
                  pl.BlockSpec((1, d), lambda i: (0, 0))],
        out_specs=pl.BlockSpec((tm, d), lambda i: (i, 0)),
        out_shape=jax.ShapeDtypeStruct((m, d), out_dtype),
        compiler_params=_cparams(("arbitrary",)),
        name="rmsnorm",
    )(x, g.reshape(1, d))


def _rope_slab(z, cos, sin, half):
    if 2 * half == LANES:
        partner = pltpu.roll(z, half, 1)
    else:
        lane = lax.broadcasted_iota(jnp.int32, z.shape, 1)
        first = (lane % (2 * half)) < half
        partner = jnp.where(first, pltpu.roll(z, LANES - half, 1), pltpu.roll(z, half, 1))
    return z * cos + partner * sin


def _epilogue(g, z, c0, *, epi, rope_half, scale):
    o_ref = g["o"]
    width = z.shape[1]
    if rope_half:
        cos, sin = g["cos"][...], g["sin"][...]
        for s in range(width // LANES):
            slab = _rope_slab(z[:, s * LANES:(s + 1) * LANES], cos, sin, rope_half)
            if scale != 1.0:
                slab = slab * scale
            if g["pm"]:
                o_ref[c0 // LANES + s] = slab.astype(o_ref.dtype)
            else:
                o_ref[:, c0 + s * LANES:c0 + (s + 1) * LANES] = slab.astype(o_ref.dtype)
        return
    if epi == "relu2":
        r = jnp.maximum(z, 0.0)
        z = r * r
    elif epi == "residual":
        z = g["res"][:, c0:c0 + width] + z
    if g["pm"]:
        for s in range(width // LANES):
            o_ref[c0 // LANES + s] = z[:, s * LANES:(s + 1) * LANES].astype(o_ref.dtype)
    else:
        o_ref[:, c0:c0 + width] = z.astype(o_ref.dtype)


def _unpack_groups(it, ng, rope_half, epi, pair_major):
    groups = []
    for _ in range(ng):
        g = dict(x=next(it))
        if rope_half:
            g["cos"], g["sin"] = next(it), next(it)
        if epi == "residual":
            g["res"] = next(it)
        groups.append(g)
    for gi, g in enumerate(groups):
        g["o"] = next(it)
        g["pm"] = pair_major[gi]
    return groups


def _mm_ws_kernel(*refs, starts, ntiles, npieces, epi, rope_half, rope_blocks, scale, pair_major,
                  w_t, side_cast):
    it = iter(refs)
    wp_ref = next(it)
    side_in = next(it) if side_cast else None
    groups = _unpack_groups(it, len(starts), rope_half, epi, pair_major)
    side_out = next(it) if side_cast else None
    wb_ref = next(it)
    jj = pl.program_id(0)
    i = pl.program_id(1)
    nj = pl.num_programs(0) - 1

    if side_cast:
        @pl.when((jj > 0) & (i < ntiles[0]))
        def _():
            side_out[...] = side_in[...].astype(side_out.dtype)

    @pl.when((jj < nj) & (i < npieces))
    def _():
        piece = wp_ref[0].astype(_BF16)
        rows = piece.shape[0]
        wb_ref[jj % 2, pl.ds(pl.multiple_of(i * rows, rows), rows), :] = piece

    for gi, g in enumerate(groups):
        active = (jj > 0) & (i >= starts[gi]) & (i < starts[gi] + ntiles[gi])

        @pl.when(active)
        def _(g=g):
            slot = (jj + 1) % 2
            tn = wb_ref.shape[1] if w_t else wb_ref.shape[2]
            sub = min(tn, SUB_TN)
            for c0 in range(0, tn, sub):
                if w_t:
                    z = lax.dot_general(g["x"][...], wb_ref[slot, c0:c0 + sub, :], _NT,
                                        preferred_element_type=_F32)
                else:
                    z = jnp.dot(g["x"][...], wb_ref[slot, :, c0:c0 + sub],
                                preferred_element_type=_F32)
                if rope_blocks is None:
                    _epilogue(g, z, c0, epi=epi, rope_half=rope_half, scale=scale)
                else:
                    @pl.when(jj <= rope_blocks)
                    def _():
                        _epilogue(g, z, c0, epi=epi, rope_half=rope_half, scale=scale)

                    @pl.when(jj > rope_blocks)
                    def _():
                        _epilogue(g, z, c0, epi=epi, rope_half=0, scale=1.0)


def _mm_kt_kernel(*refs, starts, ntiles):
    it = iter(refs)
    w_ref = next(it)
    groups = _unpack_groups(it, len(starts), 0, "residual", [False] * len(starts))
    i = pl.program_id(1)
    k = pl.program_id(2)
    for gi, g in enumerate(groups):
        active = (i >= starts[gi]) & (i < starts[gi] + ntiles[gi])

        @pl.when(active)
        def _(g=g):
            @pl.when(k == 0)
            def _():
                g["o"][...] = g["res"][...]

            tm = g["x"].shape[0]
            sub = min(tm, SUB_TM)
            for r0 in range(0, tm, sub):
                g["o"][r0:r0 + sub, :] += jnp.dot(g["x"][r0:r0 + sub, :], w_ref[0].astype(_BF16),
                                                   preferred_element_type=_F32)


def _matmul(xs, w, layer, *, col_start, n_cols, out_dtype, tms, epi="none", rope=None,
            scale=1.0, residuals=None, pair_major=None, tn=TN, tk=None, w_t=False,
            rope_blocks=None, side_cast=None):
    kdim = w.shape[2] if w_t else w.shape[1]
    tn = min(tn, n_cols)
    nj = n_cols // tn
    cb0 = col_start // tn
    assert n_cols % tn == 0
    assert col_start % (SUBLANES if w_t else tn) == 0
    ng = len(xs)
    pair_major = [False] * ng if pair_major is None else pair_major
    ntiles = [x.shape[0] // tm for x, tm in zip(xs, tms)]
    starts = [int(s) for s in np.concatenate([[0], np.cumsum(ntiles)[:-1]])]
    ni = sum(ntiles)
    ws = tk is None
    if ws:
        tk = kdim
        npieces = max(p for p in (8, 4, 2, 1) if p <= ni)
    else:
        assert epi == "residual" and out_dtype == _F32 and not w_t and kdim % tk == 0
        assert rope is None and not any(pair_major)

    def tile(gi):
        return lambda i: jnp.clip(i - starts[gi], 0, ntiles[gi] - 1)

    if ws:
        col = lambda a: jnp.maximum(a[0] - 1, 0)
        krow = lambda a: 0
        rowsel = lambda a, t: jnp.where(a[0] > 0, t(a[1]), 0)
        nxt = lambda a: jnp.minimum(a[0], nj - 1)
        pc = lambda a: jnp.minimum(a[1], npieces - 1)
        if w_t:
            pr = tn // npieces
            w_spec = pl.BlockSpec(
                (pl.Element(1), pl.Element(pr), pl.Element(tk)),
                lambda *a: (layer, pl.multiple_of(col_start + nxt(a) * tn + pc(a) * pr, SUBLANES),
                            0))
        else:
            w_spec = pl.BlockSpec((1, tk // npieces, tn), lambda *a: (layer, pc(a), cb0 + nxt(a)))
    else:
        col = lambda a: a[0]
        krow = lambda a: a[2]
        rowsel = lambda a, t: t(a[1])
        w_spec = pl.BlockSpec((1, tk, tn), lambda *a: (layer, a[2], cb0 + a[0]))

    in_specs, args = [w_spec], [w]
    if side_cast is not None:
        side, side_layer = side_cast
        n_side = nj * ntiles[0]
        sr = side.shape[1] // n_side
        assert ws and starts[0] == 0 and side.shape[1] % n_side == 0 and sr % SUBLANES == 0
        side_blk = lambda a: jnp.clip((a[0] - 1) * ntiles[0] + jnp.minimum(a[1], ntiles[0] - 1),
                                      0, n_side - 1)
        in_specs.append(pl.BlockSpec((1, sr, side.shape[2]),
                                     lambda *a: (side_layer, side_blk(a), 0)))
        args.append(side)
    rope_half = 0
    for gi in range(ng):
        tm, t = tms[gi], tile(gi)
        in_specs.append(pl.BlockSpec((tm, tk), lambda *a, t=t: (rowsel(a, t), krow(a))))
        args.append(xs[gi])
        if rope is not None:
            rope_half = rope[0]
            cos, sin = rope[1][gi]
            nper = cos.shape[0] // tm
            tspec = pl.BlockSpec((tm, LANES),
                                 lambda *a, t=t, nper=nper: (rowsel(a, t) % nper, 0))
            in_specs += [tspec, tspec]
            args += [cos, sin]
        if epi == "residual":
            in_specs.append(pl.BlockSpec((tm, tn), lambda *a, t=t: (rowsel(a, t), col(a))))
            args.append(residuals[gi])
    out_shapes, out_specs = [], []
    for gi in range(ng):
        tm, t, m = tms[gi], tile(gi), xs[gi].shape[0]
        if pair_major[gi]:
            out_shapes.append(jax.ShapeDtypeStruct((n_cols // LANES, m, LANES), out_dtype))
            out_specs.append(pl.BlockSpec((tn // LANES, tm, LANES),
                                          lambda *a, t=t: (col(a), rowsel(a, t), 0)))
        else:
            out_shapes.append(jax.ShapeDtypeStruct((m, n_cols), out_dtype))
            out_specs.append(pl.BlockSpec((tm, tn), lambda *a, t=t: (rowsel(a, t), col(a))))
    if side_cast is not None:
        out_shapes.append(jax.ShapeDtypeStruct((1,) + side.shape[1:], _BF16))
        out_specs.append(pl.BlockSpec((1, sr, side.shape[2]), lambda *a: (0, side_blk(a), 0)))
    if ws:
        body = functools.partial(_mm_ws_kernel, starts=starts, ntiles=ntiles, npieces=npieces,
                                 epi=epi, rope_half=rope_half, rope_blocks=rope_blocks,
                                 scale=scale, pair_major=tuple(pair_major), w_t=w_t,
                                 side_cast=side_cast is not None)
        grid = (nj + 1, ni)
        scratch = [pltpu.VMEM((2, tn, tk) if w_t else (2, tk, tn), _BF16)]
    else:
        body = functools.partial(_mm_kt_kernel, starts=starts, ntiles=ntiles)
        grid = (nj, ni, kdim // tk)
        scratch = []
    return pl.pallas_call(
        body,
        grid=grid,
        in_specs=in_specs,
        out_specs=out_specs,
        out_shape=out_shapes,
        scratch_shapes=scratch,
        compiler_params=_cparams(("arbitrary",) * len(grid)),
        name="matmul_" + epi + ("_rope%d" % rope_half if rope_half else ""),
    )(*args)


def _sortable(x):
    bits = pltpu.bitcast(x, jnp.int32)
    return jnp.where(bits < 0, bits ^ jnp.int32(0x7FFFFFFF), bits)


def _kth_largest(keys, n_sel, count_fn):
    cnt = count_fn((keys >= 0).astype(jnp.int32))
    thr0 = jnp.where(cnt >= n_sel, jnp.int32(0), jnp.int32(INT_MIN))

    def body(i, thr):
        cand = thr | jnp.left_shift(jnp.int32(1), 30 - i)
        cnt = count_fn((keys >= cand).astype(jnp.int32))
        return jnp.where(cnt >= n_sel, cand, thr)

    return lax.fori_loop(0, 31, body, thr0)


def _select_topk(keys, idx, n_sel, count_fn, idx_bits):
    thr = _kth_largest(keys, n_sel, count_fn)
    gt = keys > thr
    eq = keys == thr
    need = n_sel - count_fn(gt.astype(jnp.int32))
    surplus = count_fn(eq.astype(jnp.int32)) - need

    def tie_cut():
        def body(b, lo):
            cand = lo + jnp.left_shift(jnp.int32(1), idx_bits - 1 - b)
            below = count_fn((eq & (idx < cand)).astype(jnp.int32))
            return jnp.where(below < need, cand, lo)

        return lax.fori_loop(0, idx_bits, body, jnp.zeros_like(thr))

    cut = lax.cond(jnp.max(surplus) > 0, tie_cut,
                   lambda: jnp.full_like(thr, jnp.iinfo(jnp.int32).max))
    return gt | (eq & (idx <= cut))


def _pattn_kernel(qi_ref, wit_ref, ki_ref, q_ref, k_ref, v_ref, o_ref, sc_ref, bias_ref, *,
                  n_sel, kv_rep):
    jq = pl.program_id(1)
    g = pl.program_id(2)
    s_len, tq = sc_ref.shape
    n_pairs = qi_ref.shape[0]

    def select(klen):
        key_pos = lax.broadcasted_iota(jnp.int32, (klen, tq), 0)
        q_pos = jq * tq + lax.broadcasted_iota(jnp.int32, (klen, tq), 1)
        adm = key_pos <= q_pos
        if klen <= n_sel:
            bias_ref[:, 0:klen] = jnp.where(adm, 0.0, NEG).astype(_F32).T
            return
        kb = ki_ref[0:klen, :].astype(_BF16)
        zeros = jnp.zeros_like(kb)
        k_even = jnp.concatenate([kb, zeros], axis=1)
        k_odd = jnp.concatenate([zeros, kb], axis=1)
        w_scale = (IDX_HEADS ** -0.5) * (IDX_DIM ** -0.5)
        sc_ref[0:klen, :] = jnp.zeros((klen, tq), _F32)

        def body(p, carry):
            x = qi_ref[p]
            d0 = lax.dot_general(k_even, x, _NT, preferred_element_type=_F32)
            d1 = lax.dot_general(k_odd, x, _NT, preferred_element_type=_F32)
            w0 = wit_ref[pl.ds(2 * p, 1), :] * w_scale
            w1 = wit_ref[pl.ds(2 * p + 1, 1), :] * w_scale
            sc_ref[0:klen, :] += jnp.maximum(d0, 0.0) * w0 + jnp.maximum(d1, 0.0) * w1
            return carry

        lax.fori_loop(0, n_pairs, body, 0)

        keys = _sortable(jnp.where(adm, sc_ref[0:klen, :], NEG))
        sel = _select_topk(keys, key_pos, n_sel, lambda c: jnp.sum(c, axis=0, keepdims=True),
                           (s_len - 1).bit_length()) & adm
        bias_ref[:, 0:klen] = jnp.where(sel, 0.0, NEG).astype(_F32).T

    def attend(klen):
        kg = k_ref[0, 0:klen, :].astype(_BF16)
        vg = v_ref[0, 0:klen, :].astype(_BF16)
        bias = bias_ref[:, 0:klen]
        for r in range(kv_rep):
            qh = q_ref[:, r * HEAD_DIM:(r + 1) * HEAD_DIM]
            s = lax.dot_general(qh, kg, _NT, preferred_element_type=_F32) + bias
            m = jnp.max(s, axis=1, keepdims=True)
            p = jnp.exp(s - m)
            l = jnp.sum(p, axis=1, keepdims=True)
            o = jnp.dot(p.astype(_BF16), vg, preferred_element_type=_F32) / l
            o_ref[:, r * HEAD_DIM:(r + 1) * HEAD_DIM] = o.astype(o_ref.dtype)

    for n in range(1, s_len // tq + 1):
        @pl.when(jq == n - 1)
        def _(klen=n * tq):
            @pl.when(g == 0)
            def _():
                select(klen)

            attend(klen)


def _prompt_attention(qi_pm, wit, ki, q, kv, *, batch, seq):
    m, d = q.shape
    n_kv = kv.shape[0] // 2
    kv_rep = (d // HEAD_DIM) // n_kv
    tq = min(seq, 256)
    nq = seq // tq
    n_sel = min(TOPK_MAX, seq // 4)
    n_pairs = qi_pm.shape[0]
    gw = kv_rep * HEAD_DIM
    return pl.pallas_call(
        functools.partial(_pattn_kernel, n_sel=n_sel, kv_rep=kv_rep),
        grid=(batch, nq, n_kv),
        in_specs=[
            pl.BlockSpec((n_pairs, tq, LANES), lambda b, j, g: (0, b * nq + j, 0)),
            pl.BlockSpec((wit.shape[0], tq), lambda b, j, g: (0, b * nq + j)),
            pl.BlockSpec((seq, ki.shape[1]), lambda b, j, g: (b, 0)),
            pl.BlockSpec((tq, gw), lambda b, j, g: (b * nq + j, g)),
            pl.BlockSpec((1, seq, HEAD_DIM), lambda b, j, g: (g, b, 0)),
            pl.BlockSpec((1, seq, HEAD_DIM), lambda b, j, g: (n_kv + g, b, 0)),
        ],
        out_specs=pl.BlockSpec((tq, gw), lambda b, j, g: (b * nq + j, g)),
        out_shape=jax.ShapeDtypeStruct((m, d), _BF16),
        scratch_shapes=[pltpu.VMEM((seq, tq), _F32), pltpu.VMEM((tq, seq), _F32)],
        compiler_params=_cparams(("arbitrary", "arbitrary", "arbitrary")),
        name="prompt_attention",
    )(qi_pm, wit, ki, q, kv, kv)


def _sidx_kernel(pt_ref, qi_ref, w_ref, kin_ref, *rest, n_sel, t_real, n_chunks):
    pages = rest[:PAGES_PER_STEP]
    bias_ref, sc_ref = rest[PAGES_PER_STEP], rest[PAGES_PER_STEP + 1]
    c = pl.program_id(1)
    rows = SUBLANES
    chunk = PAGES_PER_STEP * PAGE_SIZE
    w = w_ref[0] * ((IDX_HEADS ** -0.5) * (IDX_DIM ** -0.5))
    qi = qi_ref[0]

    def scores(d):
        d = jnp.maximum(d, 0.0) * w
        acc = d[0:rows]
        for h in range(1, IDX_HEADS):
            acc = acc + d[h * rows:(h + 1) * rows]
        return acc

    @pl.when(c < n_chunks)
    def _():
        keys_t = jnp.concatenate([p[0] for p in pages], axis=1).astype(_BF16)
        sc_ref[c] = scores(jnp.dot(qi, keys_t, preferred_element_type=_F32))

    @pl.when(c == n_chunks)
    def _():
        kn = kin_ref[...].astype(_BF16)
        kn = jnp.concatenate([kn, jnp.zeros((PAGE_SIZE - rows, kn.shape[1]), _BF16)], axis=0)
        s_new = scores(lax.dot_general(qi, kn, _NT, preferred_element_type=_F32))
        s_new = jnp.concatenate([s_new, jnp.zeros((rows, chunk - PAGE_SIZE), _F32)], axis=1)
        shape = (n_chunks + 1, rows, chunk)
        ci = lax.broadcasted_iota(jnp.int32, shape, 0)
        qrow = lax.broadcasted_iota(jnp.int32, shape, 1)
        lane = lax.broadcasted_iota(jnp.int32, shape, 2)
        adm = (ci < n_chunks) | ((lane <= qrow) & (lane < t_real))
        sc_ref[n_chunks] = s_new
        keys = _sortable(jnp.where(adm, sc_ref[...], NEG))
        count = lambda x: jnp.sum(jnp.sum(x, axis=0, keepdims=True), axis=2, keepdims=True)
        key_idx = ci * chunk + lane
        n_keys = (n_chunks + 1) * chunk
        sel = _select_topk(keys, key_idx, n_sel, count, (n_keys - 1).bit_length()) & adm
        bias_ref[0] = jnp.where(sel, 0.0, NEG).astype(_F32)


def _page_spec(block, n_chunks, page0, i):
    zeros = (0,) * (len(block) - 1)
    return pl.BlockSpec(
        block,
        lambda b, c, pt: (page0 + pt[b, jnp.minimum(c, n_chunks - 1) * PAGES_PER_STEP + i],)
        + zeros)


def _sample_index(page_table, qi_hm, w_hm, ki_new, pool_ki, page0, row0, *, t_real):
    bd, n_pages = page_table.shape
    n_chunks = n_pages // PAGES_PER_STEP
    chunk = PAGES_PER_STEP * PAGE_SIZE
    past = n_pages * PAGE_SIZE
    n_sel = min(TOPK_MAX, (past + t_real) // 4)
    hr = qi_hm.shape[1]
    grid_spec = pltpu.PrefetchScalarGridSpec(
        num_scalar_prefetch=1,
        grid=(bd, n_chunks + 1),
        in_specs=[pl.BlockSpec((1, hr, IDX_DIM), lambda b, c, pt: (b, 0, 0)),
                  pl.BlockSpec((1, hr, 1), lambda b, c, pt: (b, 0, 0)),
                  pl.BlockSpec((SUBLANES, IDX_DIM), lambda b, c, pt: (row0 + b, 0))]
                 + [_page_spec((1, IDX_DIM, PAGE_SIZE), n_chunks, page0, i)
                    for i in range(PAGES_PER_STEP)],
        out_specs=pl.BlockSpec((1, n_chunks + 1, SUBLANES, chunk), lambda b, c, pt: (b, 0, 0, 0)),
        scratch_shapes=[pltpu.VMEM((n_chunks + 1, SUBLANES, chunk), _F32)],
    )
    return pl.pallas_call(
        functools.partial(_sidx_kernel, n_sel=n_sel, t_real=t_real, n_chunks=n_chunks),
        grid_spec=grid_spec,
        out_shape=jax.ShapeDtypeStruct((bd, n_chunks + 1, SUBLANES, chunk), _F32),
        compiler_params=_cparams(("arbitrary", "arbitrary")),
        name="sample_index",
    )(page_table, qi_hm, w_hm, ki_new, *([pool_ki] * PAGES_PER_STEP))


def _sattn_kernel(pt_ref, q_ref, bias_ref, kn_ref, vn_ref, *rest, n_chunks, n_kv, kv_rep):
    kpages = rest[:PAGES_PER_STEP]
    vpages = rest[PAGES_PER_STEP:2 * PAGES_PER_STEP]
    o_ref, m_ref, l_ref, acc_ref = rest[2 * PAGES_PER_STEP:]
    c = pl.program_id(1)
    rows = SUBLANES

    @pl.when(c == 0)
    def _():
        m_ref[...] = jnp.full_like(m_ref, -jnp.inf)
        l_ref[...] = jnp.zeros_like(l_ref)
        acc_ref[...] = jnp.zeros_like(acc_ref)

    gr = kv_rep * rows

    def update(kgs, vgs, bias):
        parts = []
        for g in range(n_kv):
            qg = jnp.concatenate(
                [q_ref[:, (g * kv_rep + r) * HEAD_DIM:(g * kv_rep + r + 1) * HEAD_DIM]
                 for r in range(kv_rep)], axis=0).astype(_BF16)
            parts.append(lax.dot_general(qg, kgs[g], _NT, preferred_element_type=_F32))
        s = jnp.concatenate(parts, axis=0) + jnp.concatenate([bias] * (n_kv * kv_rep), axis=0)
        m_old = m_ref[...]
        m_new = jnp.maximum(m_old, jnp.max(s, axis=1, keepdims=True))
        alpha = jnp.exp(m_old - m_new)
        p = jnp.exp(s - m_new)
        l_ref[...] = alpha * l_ref[...] + jnp.sum(p, axis=1, keepdims=True)
        pb = p.astype(_BF16)
        pv = jnp.concatenate(
            [jnp.dot(pb[g * gr:(g + 1) * gr], vgs[g], preferred_element_type=_F32)
             for g in range(n_kv)], axis=0)
        acc_ref[...] = alpha * acc_ref[...] + pv
        m_ref[...] = m_new

    @pl.when(c < n_chunks)
    def _():
        def head_rows(pages, g):
            rows_g = pl.ds(g, PAGE_SIZE, stride=n_kv)
            return jnp.concatenate([p[rows_g, :] for p in pages], axis=0).astype(_BF16)

        update([head_rows(kpages, g) for g in range(n_kv)],
               [head_rows(vpages, g) for g in range(n_kv)], bias_ref[0, 0])

    @pl.when(c == n_chunks)
    def _():
        pad = jnp.zeros((PAGE_SIZE - rows, HEAD_DIM), _BF16)

        def new_rows(ref, g):
            return jnp.concatenate(
                [ref[:, g * HEAD_DIM:(g + 1) * HEAD_DIM].astype(_BF16), pad], axis=0)

        update([new_rows(kn_ref, g) for g in range(n_kv)],
               [new_rows(vn_ref, g) for g in range(n_kv)], bias_ref[0, 0][:, :PAGE_SIZE])
        o = acc_ref[...] / l_ref[...]
        for h in range(n_kv * kv_rep):
            o_ref[:, h * HEAD_DIM:(h + 1) * HEAD_DIM] = (
                o[h * rows:(h + 1) * rows].astype(o_ref.dtype))


def _sample_attention(page_table, q, bias, kv_new, pool_k, pool_v, page0):
    bd, n_pages = page_table.shape
    n_chunks = n_pages // PAGES_PER_STEP
    chunk = PAGES_PER_STEP * PAGE_SIZE
    m, d = q.shape
    kvw = kv_new.shape[1] // 2
    n_kv = kvw // HEAD_DIM
    kv_rep = (d // HEAD_DIM) // n_kv
    grid_spec = pltpu.PrefetchScalarGridSpec(
        num_scalar_prefetch=1,
        grid=(bd, n_chunks + 1),
        in_specs=[pl.BlockSpec((SUBLANES, d), lambda b, c, pt: (b, 0)),
                  pl.BlockSpec((1, 1, SUBLANES, chunk), lambda b, c, pt: (b, c, 0, 0)),
                  pl.BlockSpec((SUBLANES, kvw), lambda b, c, pt: (b, 0)),
                  pl.BlockSpec((SUBLANES, kvw), lambda b, c, pt: (b, 1))]
                 + [_page_spec((PAGE_SIZE * n_kv, HEAD_DIM), n_chunks, page0, i)
                    for i in range(PAGES_PER_STEP)] * 2,
        out_specs=pl.BlockSpec((SUBLANES, d), lambda b, c, pt: (b, 0)),
        scratch_shapes=[pltpu.VMEM((n_kv * kv_rep * SUBLANES, 1), _F32),
                        pltpu.VMEM((n_kv * kv_rep * SUBLANES, 1), _F32),
                        pltpu.VMEM((n_kv * kv_rep * SUBLANES, HEAD_DIM), _F32)],
    )
    return pl.pallas_call(
        functools.partial(_sattn_kernel, n_chunks=n_chunks, n_kv=n_kv, kv_rep=kv_rep),
        grid_spec=grid_spec,
        out_shape=jax.ShapeDtypeStruct((m, d), _F32),
        compiler_params=_cparams(("arbitrary", "arbitrary")),
        name="sample_attention",
    )(page_table, q, bias, kv_new, kv_new,
      *([pool_k] * PAGES_PER_STEP), *([pool_v] * PAGES_PER_STEP))


N_GATE = 5


def _gm_kernel(*refs, starts, ntiles, tms, seq_lens, has_state, cw):
    ng = len(starts)
    it = iter(refs)
    wp_ref, taps_ref = next(it), next(it)
    groups = []
    for gi in range(ng):
        g = dict(x=next(it), attn=next(it))
        if has_state[gi]:
            g["s0"], g["s1"] = next(it), next(it)
        groups.append(g)
    for g in groups:
        g["o"], g["tail"] = next(it), next(it)
    wb_ref = next(it)
    for g in groups:
        g["u"] = next(it)
    jj = pl.program_id(0)
    i = pl.program_id(1)
    nj = pl.num_programs(0) - 1

    @pl.when((jj < nj) & (i < N_GATE))
    def _():
        wb_ref[jj % 2, pl.ds(pl.multiple_of(i * cw, cw), cw), :] = wp_ref[0].astype(_BF16)

    for gi, g in enumerate(groups):
        tm, seq = tms[gi], seq_lens[gi]
        active = (jj > 0) & (i >= starts[gi]) & (i < starts[gi] + ntiles[gi])

        @pl.when(active)
        def _(g=g, gi=gi, tm=tm, seq=seq):
            u_ref = g["u"]
            first = i == starts[gi]

            @pl.when(first)
            def _():
                u_ref[0:SUBLANES, :] = jnp.zeros((SUBLANES, cw), _F32)

            @pl.when(jnp.logical_not(first))
            def _():
                u_ref[0:SUBLANES, :] = u_ref[tm:tm + SUBLANES, :]

            sub = min(tm, SUB_TM)
            for r0 in range(0, tm, sub):
                rs = slice(r0, r0 + sub)
                z = lax.dot_general(g["x"][rs, :], wb_ref[(jj + 1) % 2], _NT,
                                    preferred_element_type=_F32)
                cb, cc, ch, ga, gc = [z[:, s * cw:(s + 1) * cw] for s in range(N_GATE)]
                u = cc * ch
                u_ref[SUBLANES + r0:SUBLANES + r0 + sub, :] = u
                t = ((i - starts[gi]) * tm + r0
                     + lax.broadcasted_iota(jnp.int32, (sub, 1), 0)) % seq
                if "s0" in g:
                    s0, s1 = g["s0"][rs, :], g["s1"][rs, :]
                else:
                    s0 = s1 = jnp.zeros((sub, cw), _F32)
                h1 = SUBLANES - 1 + r0
                u1 = jnp.where(t >= 1, u_ref[h1:h1 + sub, :], s1)
                u2 = jnp.where(t >= 2, u_ref[h1 - 1:h1 - 1 + sub, :], jnp.where(t == 1, s1, s0))
                y = u * taps_ref[2:3, :] + u1 * taps_ref[1:2, :] + u2 * taps_ref[0:1, :]
                merged = (jax.nn.sigmoid(ga) * g["attn"][rs, :].astype(_F32)
                          + jax.nn.sigmoid(gc) * (cb * y))
                g["o"][rs, :] = merged.astype(g["o"].dtype)
            if tm >= seq:
                for s in range(tm // seq):
                    g["tail"][s] = u_ref[(s + 1) * seq:(s + 1) * seq + SUBLANES, :]
            else:
                g["tail"][0] = u_ref[tm:tm + SUBLANES, :]


def _gate_merge(xs, w_t, layer, attns, states, taps, *, col_start, tms, seq_lens, out_dtypes):
    assert CONV_WIDTH == 3
    kdim = w_t.shape[2]
    d = attns[0].shape[1]
    cw = min(d, GATE_CW)
    nj = d // cw
    ng = len(xs)
    ntiles = [x.shape[0] // tm for x, tm in zip(xs, tms)]
    starts = [int(s) for s in np.concatenate([[0], np.cumsum(ntiles)[:-1]])]
    ni = sum(ntiles)
    assert ni >= N_GATE and col_start % SUBLANES == 0 and d % cw == 0

    def tile(gi):
        return lambda i: jnp.clip(i - starts[gi], 0, ntiles[gi] - 1)

    col = lambda jj: jnp.maximum(jj - 1, 0)
    rowsel = lambda jj, i, t: jnp.where(jj > 0, t(i), 0)
    nxt = lambda jj: jnp.minimum(jj, nj - 1)
    seg = lambda i: jnp.minimum(i, N_GATE - 1)
    in_specs = [
        pl.BlockSpec((pl.Element(1), pl.Element(cw), pl.Element(kdim)),
                     lambda jj, i: (layer, pl.multiple_of(col_start + seg(i) * d + nxt(jj) * cw,
                                                          SUBLANES), 0)),
        pl.BlockSpec((CONV_WIDTH, cw), lambda jj, i: (0, col(jj)))]
    args = [w_t, taps]
    for gi in range(ng):
        tm, t = tms[gi], tile(gi)
        in_specs.append(pl.BlockSpec((tm, kdim), lambda jj, i, t=t: (rowsel(jj, i, t), 0)))
        act = pl.BlockSpec((tm, cw), lambda jj, i, t=t: (rowsel(jj, i, t), col(jj)))
        in_specs.append(act)
        args += [xs[gi], attns[gi]]
        if states[gi] is not None:
            in_specs += [act, act]
            args += list(states[gi])
    out_shapes, out_specs = [], []
    for gi in range(ng):
        tm, t, m, seq = tms[gi], tile(gi), xs[gi].shape[0], seq_lens[gi]
        spt, tps = max(tm // seq, 1), max(seq // tm, 1)
        out_shapes += [jax.ShapeDtypeStruct((m, d), out_dtypes[gi]),
                       jax.ShapeDtypeStruct((m // seq, SUBLANES, d), _F32)]
        out_specs += [
            pl.BlockSpec((tm, cw), lambda jj, i, t=t: (rowsel(jj, i, t), col(jj))),
            pl.BlockSpec((spt, SUBLANES, cw),
                         lambda jj, i, t=t, tps=tps: (rowsel(jj, i, t) // tps, 0, col(jj)))]
    scratch = [pltpu.VMEM((2, N_GATE * cw, kdim), _BF16)]
    scratch += [pltpu.VMEM((tm + SUBLANES, cw), _F32) for tm in tms]
    res = pl.pallas_call(
        functools.partial(_gm_kernel, starts=starts, ntiles=ntiles, tms=tuple(tms),
                          seq_lens=tuple(seq_lens),
                          has_state=tuple(s is not None for s in states), cw=cw),
        grid=(nj + 1, ni),
        in_specs=in_specs,
        out_specs=out_specs,
        out_shape=out_shapes,
        scratch_shapes=scratch,
        compiler_params=_cparams(("arbitrary", "arbitrary")),
        name="gate_merge",
    )(*args)
    return [(res[2 * gi], res[2 * gi + 1]) for gi in range(ng)]


def _rope_tables(positions, half, n_rot_lanes=LANES):
    pos = np.asarray(positions, np.float64)[:, None]
    lane = np.arange(LANES)
    inv = ROPE_THETA ** (-(lane % half).astype(np.float64) / half)
    ang = pos * inv[None, :]
    sign = np.where((lane % (2 * half)) < half, -1.0, 1.0)
    rot = (lane < n_rot_lanes)[None, :]
    cos = np.where(rot, np.cos(ang), 1.0)
    sin = np.where(rot, np.sin(ang) * sign[None, :], 0.0)
    return jnp.asarray(cos, _F32), jnp.asarray(sin, _F32)


def kernel(x_prompt, x_sample, cache_k, cache_v, cache_kidx, state_conv, page_table,
           w_in, conv_w, w_o, g_mix, g_mlp, w_up, w_down, g_final):
    B, S, D = x_prompt.shape
    Bd, T, _ = x_sample.shape
    depth = w_in.shape[0]
    n_pool = cache_k.shape[1]
    past = page_table.shape[1] * PAGE_SIZE
    R = SUBLANES
    kvw = N_KV_HEADS * HEAD_DIM
    half, ihalf = HEAD_DIM // 2, IDX_DIM // 2
    Mp, Ms = B * S, Bd * R
    tms = [min(Mp, 1024, S // 2), Ms]
    tms_down = [min(Mp, 2048), Ms]

    c_q, c_k = 0, N_HEADS * HEAD_DIM
    c_v = c_k + kvw
    c_qi = c_v + kvw
    c_kiwi = c_qi + IDX_HEADS * IDX_DIM
    c_gate = c_kiwi + IDX_DIM + IDX_HEADS

    pos_p = np.arange(max(S, tms[0])) % S
    pos_s = np.tile(past + np.arange(R), Bd)
    r128 = [_rope_tables(pos_p, half), _rope_tables(pos_s, half)]
    r64 = [_rope_tables(pos_p, ihalf), _rope_tables(pos_s, ihalf)]
    rkiwi = [_rope_tables(pos_p, ihalf, IDX_DIM), _rope_tables(pos_s, ihalf, IDX_DIM)]

    xp = x_prompt.reshape(Mp, D)
    xs = jnp.pad(x_sample, ((0, 0), (0, R - T), (0, 0))).reshape(Ms, D)
    w_in_t = jnp.swapaxes(w_in, 1, 2)
    pool_k = cache_k.reshape(depth * n_pool * PAGE_SIZE * N_KV_HEADS, HEAD_DIM)
    pool_v = cache_v.reshape(depth * n_pool * PAGE_SIZE * N_KV_HEADS, HEAD_DIM)
    pool_ki_t = jnp.swapaxes(cache_kidx, 2, 3).reshape(depth * n_pool, IDX_DIM, PAGE_SIZE)

    outs = {n: [] for n in ("kp", "vp", "kip", "cp", "ks", "vs", "kis", "cs")}
    for l in range(depth):
        hs = [_rmsnorm(xp, g_mix[l], _BF16), _rmsnorm(xs, g_mix[l], _BF16)]
        mm = functools.partial(_matmul, hs, w_in_t, l, tms=tms, w_t=True)
        qp, qs = mm(col_start=c_q, n_cols=N_HEADS * HEAD_DIM, out_dtype=_BF16,
                    rope=(half, r128), scale=HEAD_DIM ** -0.5)
        kvp, kvs = mm(col_start=c_k, n_cols=2 * kvw, out_dtype=_F32, rope=(half, r128),
                      tn=min(TN, kvw), rope_blocks=kvw // min(TN, kvw),
                      pair_major=[True, False])
        qip, qis = mm(col_start=c_qi, n_cols=IDX_HEADS * IDX_DIM, out_dtype=_BF16,
                      rope=(ihalf, r64), pair_major=[True, False])
        kwp, kws = mm(col_start=c_kiwi, n_cols=LANES, out_dtype=_F32, rope=(ihalf, rkiwi))
        page0 = l * n_pool

        kip = kwp[:, :IDX_DIM]
        wit = kwp[:, IDX_DIM:IDX_DIM + IDX_HEADS].T
        attn_p = _prompt_attention(qip, wit, kip, qp, kvp, batch=B, seq=S)

        qi_hm = qis.reshape(Bd, R, IDX_HEADS, IDX_DIM).transpose(0, 2, 1, 3).reshape(
            Bd, IDX_HEADS * R, IDX_DIM)
        kis = kws[:, :IDX_DIM]
        w_hm = kws[:, IDX_DIM:IDX_DIM + IDX_HEADS].reshape(Bd, R, IDX_HEADS).transpose(
            0, 2, 1).reshape(Bd, IDX_HEADS * R, 1)
        bias = _sample_index(page_table, qi_hm, w_hm, kis, pool_ki_t, page0, 0, t_real=T)
        attn_s = _sample_attention(page_table, qs.astype(_F32), bias, kvs, pool_k, pool_v, page0)

        s0, s1 = [jnp.repeat(state_conv[l][:, r:r + 1, :], R, axis=1).reshape(Ms, D)
                  for r in range(CONV_WIDTH - 1)]
        (mg_p, tail_p), (mg_s, tail_s) = _gate_merge(
            hs, w_in_t, l, [attn_p, attn_s], [None, (s0, s1)], conv_w[l], col_start=c_gate,
            tms=tms, seq_lens=[S, R], out_dtypes=[_BF16, _F32])
        xp, xs = _matmul([mg_p, mg_s.astype(_BF16)], w_o, l, col_start=0, n_cols=D,
                         out_dtype=_F32, tms=tms, epi="residual", residuals=[xp, xs])
        h2 = [_rmsnorm(xp, g_mlp[l], _BF16), _rmsnorm(xs, g_mlp[l], _BF16)]
        hid_p, hid_s, w_down_bf16 = _matmul(h2, w_up, l, col_start=0, n_cols=w_up.shape[2],
                                            out_dtype=_BF16, tms=tms, epi="relu2",
                                            side_cast=(w_down, l))
        xp, xs = _matmul([hid_p, hid_s], w_down_bf16, 0, col_start=0, n_cols=D, out_dtype=_F32,
                         tms=tms_down, epi="residual", residuals=[xp, xs], tn=min(D, 1024),
                         tk=min(w_down.shape[1], 2048))

        kv_tok = kvp.reshape(2, N_KV_HEADS, B, S, HEAD_DIM).transpose(0, 2, 3, 1, 4)
        outs["kp"].append(kv_tok[0])
        outs["vp"].append(kv_tok[1])
        outs["kip"].append(kip.reshape(B, S, IDX_DIM))
        outs["cp"].append(tail_p[:, R - (CONV_WIDTH - 1):, :])
        outs["ks"].append(kvs[:, :kvw].reshape(Bd, R, N_KV_HEADS, HEAD_DIM)[:, :T])
        outs["vs"].append(kvs[:, kvw:].reshape(Bd, R, N_KV_HEADS, HEAD_DIM)[:, :T])
        outs["kis"].append(kis.reshape(Bd, R, IDX_DIM)[:, :T])
        outs["cs"].append(tail_s[:, T - (CONV_WIDTH - 1):T, :])

    y_prompt = _rmsnorm(xp, g_final, _F32).reshape(B, S, D)
    y_sample = _rmsnorm(xs, g_final, _F32).reshape(Bd, R, D)[:, :T]
    st = lambda n: jnp.stack(outs[n])
    return (y_prompt, y_sample, st("kp"), st("vp"), st("kip"), st("cp"),
            st("ks"), st("vs"), st("kis"), st("cs"))
```

```python
import functools

import numpy as np
import jax
import jax.numpy as jnp
from jax import lax
from jax.experimental import pallas as pl
from jax.experimental.pallas import tpu as pltpu

HEAD_DIM = 128
N_HEADS = 32
N_KV_HEADS = 8
IDX_HEADS = 32
IDX_DIM = 64
TOPK_MAX = 256
CONV_WIDTH = 3
PAGE_SIZE = 128
ROPE_THETA = 10000.0
NORM_EPS = 1e-6
NEG = -1e30

LANES = 128
SUBLANES = 8
VMEM_LIMIT = 60 * 1024 * 1024
PAGES_PER_STEP = 16
ATTN_KV_PER_STEP = 2
INT_MIN = -2 ** 31
TN = 1024
SUB_TN = 512
SUB_TM = 512
GATE_CW = 256

_F32 = jnp.float32
_BF16 = jnp.bfloat16
_NT = (((1,), (1,)), ((), ()))


def _cparams(sem):
    return pltpu.CompilerParams(dimension_semantics=sem, vmem_limit_bytes=VMEM_LIMIT)


def _rmsnorm_kernel(x_ref, g_ref, o_ref):
    x = x_ref[...]
    y = x * lax.rsqrt(jnp.mean(x * x, axis=-1, keepdims=True) + NORM_EPS)
    o_ref[...] = (y * g_ref[...]).astype(o_ref.dtype)


def _rmsnorm(x, g, out_dtype):
    m, d = x.shape
    tm = min(m, 512)
    return pl.pallas_call(
        _rmsnorm_kernel,
        grid=(m // tm,),
        in_specs=[pl.BlockSpec((tm, d), lambda i: (i, 0)),
                  pl.BlockSpec((1, d), lambda i: (0, 0))],
        out_specs=pl.BlockSpec((tm, d), lambda i: (i, 0)),
        out_shape=jax.ShapeDtypeStruct((m, d), out_dtype),
        compiler_params=_cparams(("arbitrary",)),
        name="rmsnorm",
    )(x, g.reshape(1, d))


def _rope_slab(z, cos, sin, half):
    if 2 * half == LANES:
        partner = pltpu.roll(z, half, 1)
    else:
        lane = lax.broadcasted_iota(jnp.int32, z.shape, 1)
        first = (lane % (2 * half)) < half
        partner = jnp.where(first, pltpu.roll(z, LANES - half, 1), pltpu.roll(z, half, 1))
    return z * cos + partner * sin


def _epilogue(g, z, c0, *, epi, rope_half, scale):
    o_ref = g["o"]
    width = z.shape[1]
    if rope_half:
        cos, sin = g["cos"][...], g["sin"][...]
        for s in range(width // LANES):
            slab = _rope_slab(z[:, s * LANES:(s + 1) * LANES], cos, sin, rope_half)
            if scale != 1.0:
                slab = slab * scale
            if g["pm"]:
                o_ref[c0 // LANES + s] = slab.astype(o_ref.dtype)
            else:
                o_ref[:, c0 + s * LANES:c0 + (s + 1) * LANES] = slab.astype(o_ref.dtype)
        return
    if epi == "relu2":
        r = jnp.maximum(z, 0.0)
        z = r * r
    elif epi == "residual":
        z = g["res"][:, c0:c0 + width] + z
    o_ref[:, c0:c0 + width] = z.astype(o_ref.dtype)


def _unpack_groups(it, ng, rope_half, epi, pair_major):
    groups = []
    for _ in range(ng):
        g = dict(x=next(it))
        if rope_half:
            g["cos"], g["sin"] = next(it), next(it)
        if epi == "residual":
            g["res"] = next(it)
        groups.append(g)
    for gi, g in enumerate(groups):
        g["o"] = next(it)
        g["pm"] = pair_major[gi]
    return groups


def _mm_ws_kernel(*refs, starts, ntiles, npieces, epi, rope_half, rope_blocks, scale, pair_major,
                  w_t, side_cast):
    it = iter(refs)
    wp_ref = next(it)
    side_in = next(it) if side_cast else None
    groups = _unpack_groups(it, len(starts), rope_half, epi, pair_major)
    side_out = next(it) if side_cast else None
    wb_ref = next(it)
    jj = pl.program_id(0)
    i = pl.program_id(1)
    nj = pl.num_programs(0) - 1

    if side_cast:
        @pl.when((jj > 0) & (i < ntiles[0]))
        def _():
            side_out[...] = side_in[...].astype(side_out.dtype)

    @pl.when((jj < nj) & (i < npieces))
    def _():
        piece = wp_ref[0].astype(_BF16)
        rows = piece.shape[0]
        wb_ref[jj % 2, pl.ds(pl.multiple_of(i * rows, rows), rows), :] = piece

    for gi, g in enumerate(groups):
        active = (jj > 0) & (i >= starts[gi]) & (i < starts[gi] + ntiles[gi])

        @pl.when(active)
        def _(g=g):
            slot = (jj + 1) % 2
            tn = wb_ref.shape[1] if w_t else wb_ref.shape[2]
            sub = min(tn, SUB_TN)
            for c0 in range(0, tn, sub):
                if w_t:
                    z = lax.dot_general(g["x"][...], wb_ref[slot, c0:c0 + sub, :], _NT,
                                        preferred_element_type=_F32)
                else:
                    z = jnp.dot(g["x"][...], wb_ref[slot, :, c0:c0 + sub],
                                preferred_element_type=_F32)
                if rope_blocks is None:
                    _epilogue(g, z, c0, epi=epi, rope_half=rope_half, scale=scale)
                else:
                    @pl.when(jj <= rope_blocks)
                    def _():
                        _epilogue(g, z, c0, epi=epi, rope_half=rope_half, scale=scale)

                    @pl.when(jj > rope_blocks)
                    def _():
                        _epilogue(g, z, c0, epi=epi, rope_half=0, scale=1.0)


def _mm_kt_kernel(*refs, starts, ntiles):
    it = iter(refs)
    w_ref = next(it)
    groups = _unpack_groups(it, len(starts), 0, "residual", [False] * len(starts))
    i = pl.program_id(1)
    k = pl.program_id(2)
    for gi, g in enumerate(groups):
        active = (i >= starts[gi]) & (i < starts[gi] + ntiles[gi])

        @pl.when(active)
        def _(g=g):
            @pl.when(k == 0)
            def _():
                g["o"][...] = g["res"][...]

            tm = g["x"].shape[0]
            sub = min(tm, SUB_TM)
            for r0 in range(0, tm, sub):
                g["o"][r0:r0 + sub, :] += jnp.dot(g["x"][r0:r0 + sub, :], w_ref[0].astype(_BF16),
                                                   preferred_element_type=_F32)


def _matmul(xs, w, layer, *, col_start, n_cols, out_dtype, tms, epi="none", rope=None,
            scale=1.0, residuals=None, pair_major=None, tn=TN, tk=None, w_t=False,
            rope_blocks=None, side_cast=None):
    kdim = w.shape[2] if w_t else w.shape[1]
    tn = min(tn, n_cols)
    nj = n_cols // tn
    cb0 = col_start // tn
    assert n_cols % tn == 0
    assert col_start % (SUBLANES if w_t else tn) == 0
    ng = len(xs)
    pair_major = [False] * ng if pair_major is None else pair_major
    ntiles = [x.shape[0] // tm for x, tm in zip(xs, tms)]
    starts = [int(s) for s in np.concatenate([[0], np.cumsum(ntiles)[:-1]])]
    ni = sum(ntiles)
    ws = tk is None
    if ws:
        tk = kdim
        npieces = max(p for p in (8, 4, 2, 1) if p <= ni)
    else:
        assert epi == "residual" and out_dtype == _F32 and not w_t and kdim % tk == 0
        assert rope is None and not any(pair_major)

    def tile(gi):
        return lambda i: jnp.clip(i - starts[gi], 0, ntiles[gi] - 1)

    if ws:
        col = lambda a: jnp.maximum(a[0] - 1, 0)
        krow = lambda a: 0
        rowsel = lambda a, t: jnp.where(a[0] > 0, t(a[1]), 0)
        nxt = lambda a: jnp.minimum(a[0], nj - 1)
        pc = lambda a: jnp.minimum(a[1], npieces - 1)
        if w_t:
            pr = tn // npieces
            w_spec = pl.BlockSpec(
                (pl.Element(1), pl.Element(pr), pl.Element(tk)),
                lambda *a: (layer, pl.multiple_of(col_start + nxt(a) * tn + pc(a) * pr, SUBLANES),
                            0))
        else:
            w_spec = pl.BlockSpec((1, tk // npieces, tn), lambda *a: (layer, pc(a), cb0 + nxt(a)))
    else:
        col = lambda a: a[0]
        krow = lambda a: a[2]
        rowsel = lambda a, t: t(a[1])
        w_spec = pl.BlockSpec((1, tk, tn), lambda *a: (layer, a[2], cb0 + a[0]))

    in_specs, args = [w_spec], [w]
    if side_cast is not None:
        side, side_layer = side_cast
        n_side = nj * ntiles[0]
        sr = side.shape[1] // n_side
        assert ws and starts[0] == 0 and side.shape[1] % n_side == 0 and sr % SUBLANES == 0
        side_blk = lambda a: jnp.clip((a[0] - 1) * ntiles[0] + jnp.minimum(a[1], ntiles[0] - 1),
                                      0, n_side - 1)
        in_specs.append(pl.BlockSpec((1, sr, side.shape[2]),
                                     lambda *a: (side_layer, side_blk(a), 0)))
        args.append(side)
    rope_half = 0
    for gi in range(ng):
        tm, t = tms[gi], tile(gi)
        in_specs.append(pl.BlockSpec((tm, tk), lambda *a, t=t: (rowsel(a, t), krow(a))))
        args.append(xs[gi])
        if rope is not None:
            rope_half = rope[0]
            cos, sin = rope[1][gi]
            nper = cos.shape[0] // tm
            tspec = pl.BlockSpec((tm, LANES),
                                 lambda *a, t=t, nper=nper: (rowsel(a, t) % nper, 0))
            in_specs += [tspec, tspec]
            args += [cos, sin]
        if epi == "residual":
            in_specs.append(pl.BlockSpec((tm, tn), lambda *a, t=t: (rowsel(a, t), col(a))))
            args.append(residuals[gi])
    out_shapes, out_specs = [], []
    for gi in range(ng):
        tm, t, m = tms[gi], tile(gi), xs[gi].shape[0]
        if pair_major[gi]:
            out_shapes.append(jax.ShapeDtypeStruct((n_cols // LANES, m, LANES), out_dtype))
            out_specs.append(pl.BlockSpec((tn // LANES, tm, LANES),
                                          lambda *a, t=t: (col(a), rowsel(a, t), 0)))
        else:
            out_shapes.append(jax.ShapeDtypeStruct((m, n_cols), out_dtype))
            out_specs.append(pl.BlockSpec((tm, tn), lambda *a, t=t: (rowsel(a, t), col(a))))
    if side_cast is not None:
        out_shapes.append(jax.ShapeDtypeStruct((1,) + side.shape[1:], _BF16))
        out_specs.append(pl.BlockSpec((1, sr, side.shape[2]), lambda *a: (0, side_blk(a), 0)))
    if ws:
        body = functools.partial(_mm_ws_kernel, starts=starts, ntiles=ntiles, npieces=npieces,
                                 epi=epi, rope_half=rope_half, rope_blocks=rope_blocks,
                                 scale=scale, pair_major=tuple(pair_major), w_t=w_t,
                                 side_cast=side_cast is not None)
        grid = (nj + 1, ni)
        scratch = [pltpu.VMEM((2, tn, tk) if w_t else (2, tk, tn), _BF16)]
    else:
        body = functools.partial(_mm_kt_kernel, starts=starts, ntiles=ntiles)
        grid = (nj, ni, kdim // tk)
        scratch = []
    return pl.pallas_call(
        body,
        grid=grid,
        in_specs=in_specs,
        out_specs=out_specs,
        out_shape=out_shapes,
        scratch_shapes=scratch,
        compiler_params=_cparams(("arbitrary",) * len(grid)),
        name="matmul_" + epi + ("_rope%d" % rope_half if rope_half else ""),
    )(*args)


def _sortable(x):
    bits = pltpu.bitcast(x, jnp.int32)
    return jnp.where(bits < 0, bits ^ jnp.int32(0x7FFFFFFF), bits)


def _kth_largest(keys, n_sel, count_fn):
    cnt = count_fn((keys >= 0).astype(jnp.int32))
    thr0 = jnp.where(cnt >= n_sel, jnp.int32(0), jnp.int32(INT_MIN))

    def body(i, thr):
        cand = thr | jnp.left_shift(jnp.int32(1), 30 - i)
        cnt = count_fn((keys >= cand).astype(jnp.int32))
        return jnp.where(cnt >= n_sel, cand, thr)

    return lax.fori_loop(0, 31, body, thr0)


def _select_topk(keys, idx, n_sel, count_fn, idx_bits):
    thr = _kth_largest(keys, n_sel, count_fn)
    gt = keys > thr
    eq = keys == thr
    need = n_sel - count_fn(gt.astype(jnp.int32))
    surplus = count_fn(eq.astype(jnp.int32)) - need

    def tie_cut():
        def body(b, lo):
            cand = lo + jnp.left_shift(jnp.int32(1), idx_bits - 1 - b)
            below = count_fn((eq & (idx < cand)).astype(jnp.int32))
            return jnp.where(below < need, cand, lo)

        return lax.fori_loop(0, idx_bits, body, jnp.zeros_like(thr))

    cut = lax.cond(jnp.max(surplus) > 0, tie_cut,
                   lambda: jnp.full_like(thr, jnp.iinfo(jnp.int32).max))
    return gt | (eq & (idx <= cut))


def _pattn_kernel(qi_ref, wit_ref, ki_ref, q_ref, k_ref, v_ref, o_ref, sc_ref, bias_ref, *,
                  n_sel, kv_rep):
    jq = pl.program_id(1)
    g = pl.program_id(2)
    s_len, tq = sc_ref.shape
    n_pairs = qi_ref.shape[0]

    def select(klen):
        key_pos = lax.broadcasted_iota(jnp.int32, (klen, tq), 0)
        q_pos = jq * tq + lax.broadcasted_iota(jnp.int32, (klen, tq), 1)
        adm = key_pos <= q_pos
        if klen <= n_sel:
            bias_ref[:, 0:klen] = jnp.where(adm, 0.0, NEG).astype(_F32).T
            return
        kb = ki_ref[0:klen, :].astype(_BF16)
        zeros = jnp.zeros_like(kb)
        k_even = jnp.concatenate([kb, zeros], axis=1)
        k_odd = jnp.concatenate([zeros, kb], axis=1)
        w_scale = (IDX_HEADS ** -0.5) * (IDX_DIM ** -0.5)
        sc_ref[0:klen, :] = jnp.zeros((klen, tq), _F32)

        def body(p, carry):
            x = qi_ref[p]
            d0 = lax.dot_general(k_even, x, _NT, preferred_element_type=_F32)
            d1 = lax.dot_general(k_odd, x, _NT, preferred_element_type=_F32)
            w0 = wit_ref[pl.ds(2 * p, 1), :] * w_scale
            w1 = wit_ref[pl.ds(2 * p + 1, 1), :] * w_scale
            sc_ref[0:klen, :] += jnp.maximum(d0, 0.0) * w0 + jnp.maximum(d1, 0.0) * w1
            return carry

        lax.fori_loop(0, n_pairs, body, 0)

        keys = _sortable(jnp.where(adm, sc_ref[0:klen, :], NEG))
        sel = _select_topk(keys, key_pos, n_sel, lambda c: jnp.sum(c, axis=0, keepdims=True),
                           (s_len - 1).bit_length()) & adm
        bias_ref[:, 0:klen] = jnp.where(sel, 0.0, NEG).astype(_F32).T

    def attend(klen):
        bias = bias_ref[:, 0:klen]
        for h in range(k_ref.shape[1] // HEAD_DIM * kv_rep):
            kv_cols = slice(h // kv_rep * HEAD_DIM, (h // kv_rep + 1) * HEAD_DIM)
            kg = k_ref[0:klen, kv_cols].astype(_BF16)
            vg = v_ref[0:klen, kv_cols].astype(_BF16)
            qh = q_ref[:, h * HEAD_DIM:(h + 1) * HEAD_DIM]
            s = lax.dot_general(qh, kg, _NT, preferred_element_type=_F32) + bias
            m = jnp.max(s, axis=1, keepdims=True)
            p = jnp.exp(s - m)
            l = jnp.sum(p, axis=1, keepdims=True)
            o = jnp.dot(p.astype(_BF16), vg, preferred_element_type=_F32) / l
            o_ref[:, h * HEAD_DIM:(h + 1) * HEAD_DIM] = o.astype(o_ref.dtype)

    for n in range(1, s_len // tq + 1):
        @pl.when(jq == n - 1)
        def _(klen=n * tq):
            @pl.when(g == 0)
            def _():
                select(klen)

            attend(klen)


def _prompt_attention(qi_pm, wit, ki, q, kv, *, batch, seq):
    m, d = q.shape
    n_kv = kv.shape[1] // (2 * HEAD_DIM)
    kv_rep = (d // HEAD_DIM) // n_kv
    tq = min(seq, 256)
    nq = seq // tq
    n_sel = min(TOPK_MAX, seq // 4)
    n_pairs = qi_pm.shape[0]
    gps = ATTN_KV_PER_STEP if n_kv % ATTN_KV_PER_STEP == 0 else 1
    gw = gps * kv_rep * HEAD_DIM
    ng = n_kv // gps
    return pl.pallas_call(
        functools.partial(_pattn_kernel, n_sel=n_sel, kv_rep=kv_rep),
        grid=(batch, nq, ng),
        in_specs=[
            pl.BlockSpec((n_pairs, tq, LANES), lambda b, j, g: (0, b * nq + j, 0)),
            pl.BlockSpec((wit.shape[0], tq), lambda b, j, g: (0, b * nq + j)),
            pl.BlockSpec((seq, ki.shape[1]), lambda b, j, g: (b, 0)),
            pl.BlockSpec((tq, gw), lambda b, j, g: (b * nq + j, g)),
            pl.BlockSpec((seq, gps * HEAD_DIM), lambda b, j, g: (b, g)),
            pl.BlockSpec((seq, gps * HEAD_DIM), lambda b, j, g: (b, ng + g)),
        ],
        out_specs=pl.BlockSpec((tq, gw), lambda b, j, g: (b * nq + j, g)),
        out_shape=jax.ShapeDtypeStruct((m, d), _BF16),
        scratch_shapes=[pltpu.VMEM((seq, tq), _F32), pltpu.VMEM((tq, seq), _F32)],
        compiler_params=_cparams(("arbitrary", "arbitrary", "arbitrary")),
        name="prompt_attention",
    )(qi_pm, wit, ki, q, kv, kv)


def _sidx_kernel(pt_ref, qi_ref, w_ref, kin_ref, *rest, n_sel, t_real, n_chunks):
    pages = rest[:PAGES_PER_STEP]
    bias_ref, sc_ref = rest[PAGES_PER_STEP], rest[PAGES_PER_STEP + 1]
    c = pl.program_id(1)
    rows = SUBLANES
    chunk = PAGES_PER_STEP * PAGE_SIZE
    w = w_ref[0] * ((IDX_HEADS ** -0.5) * (IDX_DIM ** -0.5))
    qi = qi_ref[0]

    def scores(d):
        d = jnp.maximum(d, 0.0) * w
        acc = d[0:rows]
        for h in range(1, IDX_HEADS):
            acc = acc + d[h * rows:(h + 1) * rows]
        return acc

    @pl.when(c < n_chunks)
    def _():
        keys_t = jnp.concatenate([p[0] for p in pages], axis=1).astype(_BF16)
        sc_ref[c] = scores(jnp.dot(qi, keys_t, preferred_element_type=_F32))

    @pl.when(c == n_chunks)
    def _():
        kn = kin_ref[...].astype(_BF16)
        kn = jnp.concatenate([kn, jnp.zeros((PAGE_SIZE - rows, kn.shape[1]), _BF16)], axis=0)
        s_new = scores(lax.dot_general(qi, kn, _NT, preferred_element_type=_F32))
        s_new = jnp.concatenate([s_new, jnp.zeros((rows, chunk - PAGE_SIZE), _F32)], axis=1)
        shape = (n_chunks + 1, rows, chunk)
        ci = lax.broadcasted_iota(jnp.int32, shape, 0)
        qrow = lax.broadcasted_iota(jnp.int32, shape, 1)
        lane = lax.broadcasted_iota(jnp.int32, shape, 2)
        adm = (ci < n_chunks) | ((lane <= qrow) & (lane < t_real))
        sc_ref[n_chunks] = s_new
        keys = _sortable(jnp.where(adm, sc_ref[...], NEG))
        count = lambda x: jnp.sum(jnp.sum(x, axis=0, keepdims=True), axis=2, keepdims=True)
        key_idx = ci * chunk + lane
        n_keys = (n_chunks + 1) * chunk
        sel = _select_topk(keys, key_idx, n_sel, count, (n_keys - 1).bit_length()) & adm
        bias_ref[0] = jnp.where(sel, 0.0, NEG).astype(_F32)


def _page_spec(block, n_chunks, page0, i):
    zeros = (0,) * (len(block) - 1)
    return pl.BlockSpec(
        block,
        lambda b, c, pt: (page0 + pt[b, jnp.minimum(c, n_chunks - 1) * PAGES_PER_STEP + i],)
        + zeros)


def _sample_index(page_table, qi_hm, w_hm, ki_new, pool_ki, page0, row0, *, t_real):
    bd, n_pages = page_table.shape
    n_chunks = n_pages // PAGES_PER_STEP
    chunk = PAGES_PER_STEP * PAGE_SIZE
    past = n_pages * PAGE_SIZE
    n_sel = min(TOPK_MAX, (past + t_real) // 4)
    hr = qi_hm.shape[1]
    grid_spec = pltpu.PrefetchScalarGridSpec(
        num_scalar_prefetch=1,
        grid=(bd, n_chunks + 1),
        in_specs=[pl.BlockSpec((1, hr, IDX_DIM), lambda b, c, pt: (b, 0, 0)),
                  pl.BlockSpec((1, hr, 1), lambda b, c, pt: (b, 0, 0)),
                  pl.BlockSpec((SUBLANES, IDX_DIM), lambda b, c, pt: (row0 + b, 0))]
                 + [_page_spec((1, IDX_DIM, PAGE_SIZE), n_chunks, page0, i)
                    for i in range(PAGES_PER_STEP)],
        out_specs=pl.BlockSpec((1, n_chunks + 1, SUBLANES, chunk), lambda b, c, pt: (b, 0, 0, 0)),
        scratch_shapes=[pltpu.VMEM((n_chunks + 1, SUBLANES, chunk), _F32)],
    )
    return pl.pallas_call(
        functools.partial(_sidx_kernel, n_sel=n_sel, t_real=t_real, n_chunks=n_chunks),
        grid_spec=grid_spec,
        out_shape=jax.ShapeDtypeStruct((bd, n_chunks + 1, SUBLANES, chunk), _F32),
        compiler_params=_cparams(("arbitrary", "arbitrary")),
        name="sample_index",
    )(page_table, qi_hm, w_hm, ki_new, *([pool_ki] * PAGES_PER_STEP))


def _sattn_kernel(pt_ref, q_ref, bias_ref, kn_ref, vn_ref, *rest, n_chunks, n_kv, kv_rep):
    kpages = rest[:PAGES_PER_STEP]
    vpages = rest[PAGES_PER_STEP:2 * PAGES_PER_STEP]
    o_ref, m_ref, l_ref, acc_ref = rest[2 * PAGES_PER_STEP:]
    c = pl.program_id(1)
    rows = SUBLANES

    @pl.when(c == 0)
    def _():
        m_ref[...] = jnp.full_like(m_ref, -jnp.inf)
        l_ref[...] = jnp.zeros_like(l_ref)
        acc_ref[...] = jnp.zeros_like(acc_ref)

    gr = kv_rep * rows

    def update(kgs, vgs, bias):
        parts = []
        for g in range(n_kv):
            qg = jnp.concatenate(
                [q_ref[:, (g * kv_rep + r) * HEAD_DIM:(g * kv_rep + r + 1) * HEAD_DIM]
                 for r in range(kv_rep)], axis=0).astype(_BF16)
            parts.append(lax.dot_general(qg, kgs[g], _NT, preferred_element_type=_F32))
        s = jnp.concatenate(parts, axis=0) + jnp.concatenate([bias] * (n_kv * kv_rep), axis=0)
        m_old = m_ref[...]
        m_new = jnp.maximum(m_old, jnp.max(s, axis=1, keepdims=True))
        alpha = jnp.exp(m_old - m_new)
        p = jnp.exp(s - m_new)
        l_ref[...] = alpha * l_ref[...] + jnp.sum(p, axis=1, keepdims=True)
        pb = p.astype(_BF16)
        pv = jnp.concatenate(
            [jnp.dot(pb[g * gr:(g + 1) * gr], vgs[g], preferred_element_type=_F32)
             for g in range(n_kv)], axis=0)
        acc_ref[...] = alpha * acc_ref[...] + pv
        m_ref[...] = m_new

    @pl.when(c < n_chunks)
    def _():
        def head_rows(pages, g):
            rows_g = pl.ds(g, PAGE_SIZE, stride=n_kv)
            return jnp.concatenate([p[rows_g, :] for p in pages], axis=0).astype(_BF16)

        update([head_rows(kpages, g) for g in range(n_kv)],
               [head_rows(vpages, g) for g in range(n_kv)], bias_ref[0, 0])

    @pl.when(c == n_chunks)
    def _():
        pad = jnp.zeros((PAGE_SIZE - rows, HEAD_DIM), _BF16)

        def new_rows(ref, g):
            return jnp.concatenate(
                [ref[:, g * HEAD_DIM:(g + 1) * HEAD_DIM].astype(_BF16), pad], axis=0)

        update([new_rows(kn_ref, g) for g in range(n_kv)],
               [new_rows(vn_ref, g) for g in range(n_kv)], bias_ref[0, 0][:, :PAGE_SIZE])
        o = acc_ref[...] / l_ref[...]
        for h in range(n_kv * kv_rep):
            o_ref[:, h * HEAD_DIM:(h + 1) * HEAD_DIM] = (
                o[h * rows:(h + 1) * rows].astype(o_ref.dtype))


def _sample_attention(page_table, q, bias, kv_new, pool_k, pool_v, page0):
    bd, n_pages = page_table.shape
    n_chunks = n_pages // PAGES_PER_STEP
    chunk = PAGES_PER_STEP * PAGE_SIZE
    m, d = q.shape
    kvw = kv_new.shape[1] // 2
    n_kv = kvw // HEAD_DIM
    kv_rep = (d // HEAD_DIM) // n_kv
    grid_spec = pltpu.PrefetchScalarGridSpec(
        num_scalar_prefetch=1,
        grid=(bd, n_chunks + 1),
        in_specs=[pl.BlockSpec((SUBLANES, d), lambda b, c, pt: (b, 0)),
                  pl.BlockSpec((1, 1, SUBLANES, chunk), lambda b, c, pt: (b, c, 0, 0)),
                  pl.BlockSpec((SUBLANES, kvw), lambda b, c, pt: (b, 0)),
                  pl.BlockSpec((SUBLANES, kvw), lambda b, c, pt: (b, 1))]
                 + [_page_spec((PAGE_SIZE * n_kv, HEAD_DIM), n_chunks, page0, i)
                    for i in range(PAGES_PER_STEP)] * 2,
        out_specs=pl.BlockSpec((SUBLANES, d), lambda b, c, pt: (b, 0)),
        scratch_shapes=[pltpu.VMEM((n_kv * kv_rep * SUBLANES, 1), _F32),
                        pltpu.VMEM((n_kv * kv_rep * SUBLANES, 1), _F32),
                        pltpu.VMEM((n_kv * kv_rep * SUBLANES, HEAD_DIM), _F32)],
    )
    return pl.pallas_call(
        functools.partial(_sattn_kernel, n_chunks=n_chunks, n_kv=n_kv, kv_rep=kv_rep),
        grid_spec=grid_spec,
        out_shape=jax.ShapeDtypeStruct((m, d), _F32),
        compiler_params=_cparams(("arbitrary", "arbitrary")),
        name="sample_attention",
    )(page_table, q, bias, kv_new, kv_new,
      *([pool_k] * PAGES_PER_STEP), *([pool_v] * PAGES_PER_STEP))


N_GATE = 5


def _gm_kernel(*refs, starts, ntiles, tms, seq_lens, has_state, cw):
    ng = len(starts)
    it = iter(refs)
    wp_ref, taps_ref = next(it), next(it)
    groups = []
    for gi in range(ng):
        g = dict(x=next(it), attn=next(it))
        if has_state[gi]:
            g["s0"], g["s1"] = next(it), next(it)
        groups.append(g)
    for g in groups:
        g["o"], g["tail"] = next(it), next(it)
    wb_ref = next(it)
    for g in groups:
        g["u"] = next(it)
    jj = pl.program_id(0)
    i = pl.program_id(1)
    nj = pl.num_programs(0) - 1

    @pl.when((jj < nj) & (i < N_GATE))
    def _():
        wb_ref[jj % 2, pl.ds(pl.multiple_of(i * cw, cw), cw), :] = wp_ref[0].astype(_BF16)

    for gi, g in enumerate(groups):
        tm, seq = tms[gi], seq_lens[gi]
        active = (jj > 0) & (i >= starts[gi]) & (i < starts[gi] + ntiles[gi])

        @pl.when(active)
        def _(g=g, gi=gi, tm=tm, seq=seq):
            u_ref = g["u"]
            first = i == starts[gi]

            @pl.when(first)
            def _():
                u_ref[0:SUBLANES, :] = jnp.zeros((SUBLANES, cw), _F32)

            @pl.when(jnp.logical_not(first))
            def _():
                u_ref[0:SUBLANES, :] = u_ref[tm:tm + SUBLANES, :]

            sub = min(tm, SUB_TM)
            for r0 in range(0, tm, sub):
                rs = slice(r0, r0 + sub)
                z = lax.dot_general(g["x"][rs, :], wb_ref[(jj + 1) % 2], _NT,
                                    preferred_element_type=_F32)
                cb, cc, ch, ga, gc = [z[:, s * cw:(s + 1) * cw] for s in range(N_GATE)]
                u = cc * ch
                u_ref[SUBLANES + r0:SUBLANES + r0 + sub, :] = u
                t = ((i - starts[gi]) * tm + r0
                     + lax.broadcasted_iota(jnp.int32, (sub, 1), 0)) % seq
                if "s0" in g:
                    s0, s1 = g["s0"][rs, :], g["s1"][rs, :]
                else:
                    s0 = s1 = jnp.zeros((sub, cw), _F32)
                h1 = SUBLANES - 1 + r0
                u1 = jnp.where(t >= 1, u_ref[h1:h1 + sub, :], s1)
                u2 = jnp.where(t >= 2, u_ref[h1 - 1:h1 - 1 + sub, :], jnp.where(t == 1, s1, s0))
                y = u * taps_ref[2:3, :] + u1 * taps_ref[1:2, :] + u2 * taps_ref[0:1, :]
                merged = (jax.nn.sigmoid(ga) * g["attn"][rs, :].astype(_F32)
                          + jax.nn.sigmoid(gc) * (cb * y))
                g["o"][rs, :] = merged.astype(g["o"].dtype)
            if tm >= seq:
                for s in range(tm // seq):
                    g["tail"][s] = u_ref[(s + 1) * seq:(s + 1) * seq + SUBLANES, :]
            else:
                g["tail"][0] = u_ref[tm:tm + SUBLANES, :]


def _gate_merge(xs, w_t, layer, attns, states, taps, *, col_start, tms, seq_lens, out_dtypes):
    assert CONV_WIDTH == 3
    kdim = w_t.shape[2]
    d = attns[0].shape[1]
    cw = min(d, GATE_CW)
    nj = d // cw
    ng = len(xs)
    ntiles = [x.shape[0] // tm for x, tm in zip(xs, tms)]
    starts = [int(s) for s in np.concatenate([[0], np.cumsum(ntiles)[:-1]])]
    ni = sum(ntiles)
    assert ni >= N_GATE and col_start % SUBLANES == 0 and d % cw == 0

    def tile(gi):
        return lambda i: jnp.clip(i - starts[gi], 0, ntiles[gi] - 1)

    col = lambda jj: jnp.maximum(jj - 1, 0)
    rowsel = lambda jj, i, t: jnp.where(jj > 0, t(i), 0)
    nxt = lambda jj: jnp.minimum(jj, nj - 1)
    seg = lambda i: jnp.minimum(i, N_GATE - 1)
    in_specs = [
        pl.BlockSpec((pl.Element(1), pl.Element(cw), pl.Element(kdim)),
                     lambda jj, i: (layer, pl.multiple_of(col_start + seg(i) * d + nxt(jj) * cw,
                                                          SUBLANES), 0)),
        pl.BlockSpec((CONV_WIDTH, cw), lambda jj, i: (0, col(jj)))]
    args = [w_t, taps]
    for gi in range(ng):
        tm, t = tms[gi], tile(gi)
        in_specs.append(pl.BlockSpec((tm, kdim), lambda jj, i, t=t: (rowsel(jj, i, t), 0)))
        act = pl.BlockSpec((tm, cw), lambda jj, i, t=t: (rowsel(jj, i, t), col(jj)))
        in_specs.append(act)
        args += [xs[gi], attns[gi]]
        if states[gi] is not None:
            in_specs += [act, act]
            args += list(states[gi])
    out_shapes, out_specs = [], []
    for gi in range(ng):
        tm, t, m, seq = tms[gi], tile(gi), xs[gi].shape[0], seq_lens[gi]
        spt, tps = max(tm // seq, 1), max(seq // tm, 1)
        out_shapes += [jax.ShapeDtypeStruct((m, d), out_dtypes[gi]),
                       jax.ShapeDtypeStruct((m // seq, SUBLANES, d), _F32)]
        out_specs += [
            pl.BlockSpec((tm, cw), lambda jj, i, t=t: (rowsel(jj, i, t), col(jj))),
            pl.BlockSpec((spt, SUBLANES, cw),
                         lambda jj, i, t=t, tps=tps: (rowsel(jj, i, t) // tps, 0, col(jj)))]
    scratch = [pltpu.VMEM((2, N_GATE * cw, kdim), _BF16)]
    scratch += [pltpu.VMEM((tm + SUBLANES, cw), _F32) for tm in tms]
    res = pl.pallas_call(
        functools.partial(_gm_kernel, starts=starts, ntiles=ntiles, tms=tuple(tms),
                          seq_lens=tuple(seq_lens),
                          has_state=tuple(s is not None for s in states), cw=cw),
        grid=(nj + 1, ni),
        in_specs=in_specs,
        out_specs=out_specs,
        out_shape=out_shapes,
        scratch_shapes=scratch,
        compiler_params=_cparams(("arbitrary", "arbitrary")),
        name="gate_merge",
    )(*args)
    return [(res[2 * gi], res[2 * gi + 1]) for gi in range(ng)]


def _rope_tables(positions, half, n_rot_lanes=LANES):
    pos = np.asarray(positions, np.float64)[:, None]
    lane = np.arange(LANES)
    inv = ROPE_THETA ** (-(lane % half).astype(np.float64) / half)
    ang = pos * inv[None, :]
    sign = np.where((lane % (2 * half)) < half, -1.0, 1.0)
    rot = (lane < n_rot_lanes)[None, :]
    cos = np.where(rot, np.cos(ang), 1.0)
    sin = np.where(rot, np.sin(ang) * sign[None, :], 0.0)
    return jnp.asarray(cos, _F32), jnp.asarray(sin, _F32)


def kernel(x_prompt, x_sample, cache_k, cache_v, cache_kidx, state_conv, page_table,
           w_in, conv_w, w_o, g_mix, g_mlp, w_up, w_down, g_final):
    B, S, D = x_prompt.shape
    Bd, T, _ = x_sample.shape
    depth = w_in.shape[0]
    n_pool = cache_k.shape[1]
    past = page_table.shape[1] * PAGE_SIZE
    R = SUBLANES
    kvw = N_KV_HEADS * HEAD_DIM
    half, ihalf = HEAD_DIM // 2, IDX_DIM // 2
    Mp, Ms = B * S, Bd * R
    tms = [min(Mp, 1024, S // 2), Ms]
    tms_down = [min(Mp, 2048), Ms]

    c_q, c_k = 0, N_HEADS * HEAD_DIM
    c_v = c_k + kvw
    c_qi = c_v + kvw
    c_kiwi = c_qi + IDX_HEADS * IDX_DIM
    c_gate = c_kiwi + IDX_DIM + IDX_HEADS

    pos_p = np.arange(max(S, tms[0])) % S
    pos_s = np.tile(past + np.arange(R), Bd)
    r128 = [_rope_tables(pos_p, half), _rope_tables(pos_s, half)]
    r64 = [_rope_tables(pos_p, ihalf), _rope_tables(pos_s, ihalf)]
    rkiwi = [_rope_tables(pos_p, ihalf, IDX_DIM), _rope_tables(pos_s, ihalf, IDX_DIM)]

    xp = x_prompt.reshape(Mp, D)
    xs = jnp.pad(x_sample, ((0, 0), (0, R - T), (0, 0))).reshape(Ms, D)
    w_in_t = jnp.swapaxes(w_in, 1, 2)
    pool_k = cache_k.reshape(depth * n_pool * PAGE_SIZE * N_KV_HEADS, HEAD_DIM)
    pool_v = cache_v.reshape(depth * n_pool * PAGE_SIZE * N_KV_HEADS, HEAD_DIM)
    pool_ki_t = jnp.swapaxes(cache_kidx, 2, 3).reshape(depth * n_pool, IDX_DIM, PAGE_SIZE)

    outs = {n: [] for n in ("kp", "vp", "kip", "cp", "ks", "vs", "kis", "cs")}
    for l in range(depth):
        hs = [_rmsnorm(xp, g_mix[l], _BF16), _rmsnorm(xs, g_mix[l], _BF16)]
        mm = functools.partial(_matmul, hs, w_in_t, l, tms=tms, w_t=True)
        qp, qs = mm(col_start=c_q, n_cols=N_HEADS * HEAD_DIM, out_dtype=_BF16,
                    rope=(half, r128), scale=HEAD_DIM ** -0.5)
        kvp, kvs = mm(col_start=c_k, n_cols=2 * kvw, out_dtype=_F32, rope=(half, r128),
                      tn=min(TN, kvw), rope_blocks=kvw // min(TN, kvw))
        qip, qis = mm(col_start=c_qi, n_cols=IDX_HEADS * IDX_DIM, out_dtype=_BF16,
                      rope=(ihalf, r64), pair_major=[True, False])
        kwp, kws = mm(col_start=c_kiwi, n_cols=LANES, out_dtype=_F32, rope=(ihalf, rkiwi))
        page0 = l * n_pool

        kip = kwp[:, :IDX_DIM]
        wit = kwp[:, IDX_DIM:IDX_DIM + IDX_HEADS].T
        attn_p = _prompt_attention(qip, wit, kip, qp, kvp, batch=B, seq=S)

        qi_hm = qis.reshape(Bd, R, IDX_HEADS, IDX_DIM).transpose(0, 2, 1, 3).reshape(
            Bd, IDX_HEADS * R, IDX_DIM)
        kis = kws[:, :IDX_DIM]
        w_hm = kws[:, IDX_DIM:IDX_DIM + IDX_HEADS].reshape(Bd, R, IDX_HEADS).transpose(
            0, 2, 1).reshape(Bd, IDX_HEADS * R, 1)
        bias = _sample_index(page_table, qi_hm, w_hm, kis, pool_ki_t, page0, 0, t_real=T)
        attn_s = _sample_attention(page_table, qs.astype(_F32), bias, kvs, pool_k, pool_v, page0)

        s0, s1 = [jnp.repeat(state_conv[l][:, r:r + 1, :], R, axis=1).reshape(Ms, D)
                  for r in range(CONV_WIDTH - 1)]
        (mg_p, tail_p), (mg_s, tail_s) = _gate_merge(
            hs, w_in_t, l, [attn_p, attn_s], [None, (s0, s1)], conv_w[l], col_start=c_gate,
            tms=tms, seq_lens=[S, R], out_dtypes=[_BF16, _F32])
        xp, xs = _matmul([mg_p, mg_s.astype(_BF16)], w_o, l, col_start=0, n_cols=D,
                         out_dtype=_F32, tms=tms, epi="residual", residuals=[xp, xs])
        h2 = [_rmsnorm(xp, g_mlp[l], _BF16), _rmsnorm(xs, g_mlp[l], _BF16)]
        hid_p, hid_s, w_down_bf16 = _matmul(h2, w_up, l, col_start=0, n_cols=w_up.shape[2],
                                            out_dtype=_BF16, tms=tms, epi="relu2",
                                            side_cast=(w_down, l))
        xp, xs = _matmul([hid_p, hid_s], w_down_bf16, 0, col_start=0, n_cols=D, out_dtype=_F32,
                         tms=tms_down, epi="residual", residuals=[xp, xs], tn=min(D, 1024),
                         tk=min(w_down.shape[1], 2048))

        outs["kp"].append(kvp[:, :kvw].reshape(B, S, N_KV_HEADS, HEAD_DIM))
        outs["vp"].append(kvp[:, kvw:].reshape(B, S, N_KV_HEADS, HEAD_DIM))
        outs["kip"].append(kip.reshape(B, S, IDX_DIM))
        outs["cp"].append(tail_p[:, R - (CONV_WIDTH - 1):, :])
        outs["ks"].append(kvs[:, :kvw].reshape(Bd, R, N_KV_HEADS, HEAD_DIM)[:, :T])
        outs["vs"].append(kvs[:, kvw:].reshape(Bd, R, N_KV_HEADS, HEAD_DIM)[:, :T])
        outs["kis"].append(kis.reshape(Bd, R, IDX_DIM)[:, :T])
        outs["cs"].append(tail_s[:, T - (CONV_WIDTH - 1):T, :])

    y_prompt = _rmsnorm(xp, g_final, _F32).reshape(B, S, D)
    y_sample = _rmsnorm(xs, g_final, _F32).reshape(Bd, R, D)[:, :T]
    st = lambda n: jnp.stack(outs[n])
    return (y_prompt, y_sample, st("kp"), st("vp"), st("kip"), st("cp"),
            st("ks"), st("vs"), st("kis"), st("cs"))
```

```python
import functools

import numpy as np
import jax
import jax.numpy as jnp
from jax import lax
from jax.experimental import pallas as pl
from jax.experimental.pallas import tpu as pltpu

HEAD_DIM = 128
N_HEADS = 32
N_KV_HEADS = 8
IDX_HEADS = 32
IDX_DIM = 64
TOPK_MAX = 256
CONV_WIDTH = 3
PAGE_SIZE = 128
ROPE_THETA = 10000.0
NORM_EPS = 1e-6
NEG = -1e30

LANES = 128
SUBLANES = 8
VMEM_LIMIT = 60 * 1024 * 1024
PAGES_PER_STEP = 16
ATTN_VARIANT_BUDGET = 16
INT_MIN = -2 ** 31
TN = 1024
SUB_TN = 512
SUB_TM = 512
GATE_CW = 256

_F32 = jnp.float32
_BF16 = jnp.bfloat16
_NT = (((1,), (1,)), ((), ()))


def _cparams(sem):
    return pltpu.CompilerParams(dimension_semantics=sem, vmem_limit_bytes=VMEM_LIMIT)


def _rmsnorm_kernel(x_ref, g_ref, o_ref):
    x = x_ref[...]
    y = x * lax.rsqrt(jnp.mean(x * x, axis=-1, keepdims=True) + NORM_EPS)
    o_ref[...] = (y * g_ref[...]).astype(o_ref.dtype)


def _rmsnorm(x, g, out_dtype):
    m, d = x.shape
    tm = min(m, 256)
    return pl.pallas_call(
        _rmsnorm_kernel,
        grid=(m // tm,),
        in_specs=[pl.BlockSpec((tm, d), lambda i: (i, 0)),
                  pl.BlockSpec((1, d), lambda i: (0, 0))],
        out_specs=pl.BlockSpec((tm, d), lambda i: (i, 0)),
        out_shape=jax.ShapeDtypeStruct((m, d), out_dtype),
        compiler_params=_cparams(("arbitrary",)),
        name="rmsnorm",
    )(x, g.reshape(1, d))


def _rope_slab(z, cos, sin, half):
    if 2 * half == LANES:
        partner = pltpu.roll(z, half, 1)
    else:
        lane = lax.broadcasted_iota(jnp.int32, z.shape, 1)
        first = (lane % (2 * half)) < half
        partner = jnp.where(first, pltpu.roll(z, LANES - half, 1), pltpu.roll(z, half, 1))
    return z * cos + partner * sin


def _epilogue(g, z, c0, *, epi, rope_half, scale):
    o_ref = g["o"]
    width = z.shape[1]
    if rope_half:
        cos, sin = g["cos"][...], g["sin"][...]
        for s in range(width // LANES):
            slab = _rope_slab(z[:, s * LANES:(s + 1) * LANES], cos, sin, rope_half)
            if scale != 1.0:
                slab = slab * scale
            if g["pm"]:
                o_ref[c0 // LANES + s] = slab.astype(o_ref.dtype)
            else:
                o_ref[:, c0 + s * LANES:c0 + (s + 1) * LANES] = slab.astype(o_ref.dtype)
        return
    if epi == "relu2":
        r = jnp.maximum(z, 0.0)
        z = r * r
    elif epi == "residual":
        z = g["res"][:, c0:c0 + width] + z
    o_ref[:, c0:c0 + width] = z.astype(o_ref.dtype)


def _unpack_groups(it, ng, rope_half, epi, pair_major):
    groups = []
    for _ in range(ng):
        g = dict(x=next(it))
        if rope_half:
            g["cos"], g["sin"] = next(it), next(it)
        if epi == "residual":
            g["res"] = next(it)
        groups.append(g)
    for gi, g in enumerate(groups):
        g["o"] = next(it)
        g["pm"] = pair_major[gi]
    return groups


def _mm_ws_kernel(*refs, starts, ntiles, npieces, epi, rope_half, rope_blocks, scale, pair_major,
                  w_t, side_cast):
    it = iter(refs)
    wp_ref = next(it)
    side_in = next(it) if side_cast else None
    groups = _unpack_groups(it, len(starts), rope_half, epi, pair_major)
    side_out = next(it) if side_cast else None
    wb_ref = next(it)
    jj = pl.program_id(0)
    i = pl.program_id(1)
    nj = pl.num_programs(0) - 1

    if side_cast:
        @pl.when((jj > 0) & (i < ntiles[0]))
        def _():
            side_out[...] = side_in[...].astype(side_out.dtype)

    @pl.when((jj < nj) & (i < npieces))
    def _():
        piece = wp_ref[0].astype(_BF16)
        rows = piece.shape[0]
        wb_ref[jj % 2, pl.ds(pl.multiple_of(i * rows, rows), rows), :] = piece

    for gi, g in enumerate(groups):
        active = (jj > 0) & (i >= starts[gi]) & (i < starts[gi] + ntiles[gi])

        @pl.when(active)
        def _(g=g):
            slot = (jj + 1) % 2
            tn = wb_ref.shape[1] if w_t else wb_ref.shape[2]
            sub = min(tn, SUB_TN)
            for c0 in range(0, tn, sub):
                if w_t:
                    z = lax.dot_general(g["x"][...], wb_ref[slot, c0:c0 + sub, :], _NT,
                                        preferred_element_type=_F32)
                else:
                    z = jnp.dot(g["x"][...], wb_ref[slot, :, c0:c0 + sub],
                                preferred_element_type=_F32)
                if rope_blocks is None:
                    _epilogue(g, z, c0, epi=epi, rope_half=rope_half, scale=scale)
                else:
                    @pl.when(jj <= rope_blocks)
                    def _():
                        _epilogue(g, z, c0, epi=epi, rope_half=rope_half, scale=scale)

                    @pl.when(jj > rope_blocks)
                    def _():
                        _epilogue(g, z, c0, epi=epi, rope_half=0, scale=1.0)


def _mm_kt_kernel(*refs, starts, ntiles):
    it = iter(refs)
    w_ref = next(it)
    groups = _unpack_groups(it, len(starts), 0, "residual", [False] * len(starts))
    i = pl.program_id(1)
    k = pl.program_id(2)
    for gi, g in enumerate(groups):
        active = (i >= starts[gi]) & (i < starts[gi] + ntiles[gi])

        @pl.when(active)
        def _(g=g):
            @pl.when(k == 0)
            def _():
                g["o"][...] = g["res"][...]

            tm = g["x"].shape[0]
            sub = min(tm, SUB_TM)
            for r0 in range(0, tm, sub):
                g["o"][r0:r0 + sub, :] += jnp.dot(g["x"][r0:r0 + sub, :], w_ref[0].astype(_BF16),
                                                   preferred_element_type=_F32)


def _matmul(xs, w, layer, *, col_start, n_cols, out_dtype, tms, epi="none", rope=None,
            scale=1.0, residuals=None, pair_major=None, tn=TN, tk=None, w_t=False,
            rope_blocks=None, side_cast=None):
    kdim = w.shape[2] if w_t else w.shape[1]
    tn = min(tn, n_cols)
    nj = n_cols // tn
    cb0 = col_start // tn
    assert n_cols % tn == 0
    assert col_start % (SUBLANES if w_t else tn) == 0
    ng = len(xs)
    pair_major = [False] * ng if pair_major is None else pair_major
    ntiles = [x.shape[0] // tm for x, tm in zip(xs, tms)]
    starts = [int(s) for s in np.concatenate([[0], np.cumsum(ntiles)[:-1]])]
    ni = sum(ntiles)
    ws = tk is None
    if ws:
        tk = kdim
        npieces = max(p for p in (8, 4, 2, 1) if p <= ni)
    else:
        assert epi == "residual" and out_dtype == _F32 and not w_t and kdim % tk == 0
        assert rope is None and not any(pair_major)

    def tile(gi):
        return lambda i: jnp.clip(i - starts[gi], 0, ntiles[gi] - 1)

    if ws:
        col = lambda a: jnp.maximum(a[0] - 1, 0)
        krow = lambda a: 0
        rowsel = lambda a, t: jnp.where(a[0] > 0, t(a[1]), 0)
        nxt = lambda a: jnp.minimum(a[0], nj - 1)
        pc = lambda a: jnp.minimum(a[1], npieces - 1)
        if w_t:
            pr = tn // npieces
            w_spec = pl.BlockSpec(
                (pl.Element(1), pl.Element(pr), pl.Element(tk)),
                lambda *a: (layer, pl.multiple_of(col_start + nxt(a) * tn + pc(a) * pr, SUBLANES),
                            0))
        else:
            w_spec = pl.BlockSpec((1, tk // npieces, tn), lambda *a: (layer, pc(a), cb0 + nxt(a)))
    else:
        col = lambda a: a[0]
        krow = lambda a: a[2]
        rowsel = lambda a, t: t(a[1])
        w_spec = pl.BlockSpec((1, tk, tn), lambda *a: (layer, a[2], cb0 + a[0]))

    in_specs, args = [w_spec], [w]
    if side_cast is not None:
        side, side_layer = side_cast
        n_side = nj * ntiles[0]
        sr = side.shape[1] // n_side
        assert ws and starts[0] == 0 and side.shape[1] % n_side == 0 and sr % SUBLANES == 0
        side_blk = lambda a: jnp.clip((a[0] - 1) * ntiles[0] + jnp.minimum(a[1], ntiles[0] - 1),
                                      0, n_side - 1)
        in_specs.append(pl.BlockSpec((1, sr, side.shape[2]),
                                     lambda *a: (side_layer, side_blk(a), 0)))
        args.append(side)
    rope_half = 0
    for gi in range(ng):
        tm, t = tms[gi], tile(gi)
        in_specs.append(pl.BlockSpec((tm, tk), lambda *a, t=t: (rowsel(a, t), krow(a))))
        args.append(xs[gi])
        if rope is not None:
            rope_half = rope[0]
            cos, sin = rope[1][gi]
            nper = cos.shape[0] // tm
            tspec = pl.BlockSpec((tm, LANES),
                                 lambda *a, t=t, nper=nper: (rowsel(a, t) % nper, 0))
            in_specs += [tspec, tspec]
            args += [cos, sin]
        if epi == "residual":
            in_specs.append(pl.BlockSpec((tm, tn), lambda *a, t=t: (rowsel(a, t), col(a))))
            args.append(residuals[gi])
    out_shapes, out_specs = [], []
    for gi in range(ng):
        tm, t, m = tms[gi], tile(gi), xs[gi].shape[0]
        if pair_major[gi]:
            out_shapes.append(jax.ShapeDtypeStruct((n_cols // LANES, m, LANES), out_dtype))
            out_specs.append(pl.BlockSpec((tn // LANES, tm, LANES),
                                          lambda *a, t=t: (col(a), rowsel(a, t), 0)))
        else:
            out_shapes.append(jax.ShapeDtypeStruct((m, n_cols), out_dtype))
            out_specs.append(pl.BlockSpec((tm, tn), lambda *a, t=t: (rowsel(a, t), col(a))))
    if side_cast is not None:
        out_shapes.append(jax.ShapeDtypeStruct((1,) + side.shape[1:], _BF16))
        out_specs.append(pl.BlockSpec((1, sr, side.shape[2]), lambda *a: (0, side_blk(a), 0)))
    if ws:
        body = functools.partial(_mm_ws_kernel, starts=starts, ntiles=ntiles, npieces=npieces,
                                 epi=epi, rope_half=rope_half, rope_blocks=rope_blocks,
                                 scale=scale, pair_major=tuple(pair_major), w_t=w_t,
                                 side_cast=side_cast is not None)
        grid = (nj + 1, ni)
        scratch = [pltpu.VMEM((2, tn, tk) if w_t else (2, tk, tn), _BF16)]
    else:
        body = functools.partial(_mm_kt_kernel, starts=starts, ntiles=ntiles)
        grid = (nj, ni, kdim // tk)
        scratch = []
    return pl.pallas_call(
        body,
        grid=grid,
        in_specs=in_specs,
        out_specs=out_specs,
        out_shape=out_shapes,
        scratch_shapes=scratch,
        compiler_params=_cparams(("arbitrary",) * len(grid)),
        name="matmul_" + epi + ("_rope%d" % rope_half if rope_half else ""),
    )(*args)


def _sortable(x):
    bits = pltpu.bitcast(x, jnp.int32)
    return jnp.where(bits < 0, bits ^ jnp.int32(0x7FFFFFFF), bits)


def _kth_largest(keys, n_sel, count_fn):
    cnt = count_fn((keys >= 0).astype(jnp.int32))
    thr0 = jnp.where(cnt >= n_sel, jnp.int32(0), jnp.int32(INT_MIN))

    def body(i, thr):
        cand = thr | jnp.left_shift(jnp.int32(1), 30 - i)
        cnt = count_fn((keys >= cand).astype(jnp.int32))
        return jnp.where(cnt >= n_sel, cand, thr)

    return lax.fori_loop(0, 31, body, thr0)


def _select_topk(keys, idx, n_sel, count_fn, idx_bits):
    thr = _kth_largest(keys, n_sel, count_fn)
    gt = keys > thr
    eq = keys == thr
    need = n_sel - count_fn(gt.astype(jnp.int32))
    surplus = count_fn(eq.astype(jnp.int32)) - need

    def tie_cut():
        def body(b, lo):
            cand = lo + jnp.left_shift(jnp.int32(1), idx_bits - 1 - b)
            below = count_fn((eq & (idx < cand)).astype(jnp.int32))
            return jnp.where(below < need, cand, lo)

        return lax.fori_loop(0, idx_bits, body, jnp.zeros_like(thr))

    cut = lax.cond(jnp.max(surplus) > 0, tie_cut,
                   lambda: jnp.full_like(thr, jnp.iinfo(jnp.int32).max))
    return gt | (eq & (idx <= cut))


def _pattn_kernel(*refs, n_sel, kv_rep, jq0, n_tiles):
    qi_ref, wit_ref, ki_ref, q_ref, k_ref, v_ref = refs[:6]
    o_ref, sc_ref, bias_ref = refs[-3:]
    jq = jq0 + pl.program_id(1)
    g = pl.program_id(2)
    s_len, tq = sc_ref.shape
    n_pairs = qi_ref.shape[0]

    def select(klen):
        key_pos = lax.broadcasted_iota(jnp.int32, (klen, tq), 0)
        q_pos = jq * tq + lax.broadcasted_iota(jnp.int32, (klen, tq), 1)
        adm = key_pos <= q_pos
        if klen <= n_sel:
            bias_ref[:, 0:klen] = jnp.where(adm, 0.0, NEG).astype(_F32).T
            return
        kb = ki_ref[0:klen, :].astype(_BF16)
        zeros = jnp.zeros_like(kb)
        k_even = jnp.concatenate([kb, zeros], axis=1)
        k_odd = jnp.concatenate([zeros, kb], axis=1)
        w_scale = (IDX_HEADS ** -0.5) * (IDX_DIM ** -0.5)
        sc_ref[0:klen, :] = jnp.zeros((klen, tq), _F32)

        def body(p, carry):
            x = qi_ref[p]
            d0 = lax.dot_general(k_even, x, _NT, preferred_element_type=_F32)
            d1 = lax.dot_general(k_odd, x, _NT, preferred_element_type=_F32)
            w0 = wit_ref[pl.ds(2 * p, 1), :] * w_scale
            w1 = wit_ref[pl.ds(2 * p + 1, 1), :] * w_scale
            sc_ref[0:klen, :] += jnp.maximum(d0, 0.0) * w0 + jnp.maximum(d1, 0.0) * w1
            return carry

        lax.fori_loop(0, n_pairs, body, 0)

        keys = _sortable(jnp.where(adm, sc_ref[0:klen, :], NEG))
        sel = _select_topk(keys, key_pos, n_sel, lambda c: jnp.sum(c, axis=0, keepdims=True),
                           (s_len - 1).bit_length()) & adm
        bias_ref[:, 0:klen] = jnp.where(sel, 0.0, NEG).astype(_F32).T

    def attend(klen):
        kg = k_ref[0:klen, :].astype(_BF16)
        vg = v_ref[0:klen, :].astype(_BF16)
        bias = bias_ref[:, 0:klen]
        for r in range(kv_rep):
            qh = q_ref[:, r * HEAD_DIM:(r + 1) * HEAD_DIM]
            s = lax.dot_general(qh, kg, _NT, preferred_element_type=_F32) + bias
            m = jnp.max(s, axis=1, keepdims=True)
            p = jnp.exp(s - m)
            l = jnp.sum(p, axis=1, keepdims=True)
            o = jnp.dot(p.astype(_BF16), vg, preferred_element_type=_F32) / l
            o_ref[:, r * HEAD_DIM:(r + 1) * HEAD_DIM] = o.astype(o_ref.dtype)

    for n in range(jq0 + 1, jq0 + n_tiles + 1):
        @pl.when(jq == n - 1)
        def _(klen=n * tq):
            @pl.when(g == 0)
            def _():
                select(klen)

            attend(klen)


def _prompt_attention(qi_pm, wit, ki, q, kv, *, batch, seq):
    m, d = q.shape
    n_kv = kv.shape[1] // (2 * HEAD_DIM)
    kv_rep = (d // HEAD_DIM) // n_kv
    tq = min(seq, 256)
    nq = seq // tq
    n_sel = min(TOPK_MAX, seq // 4)
    n_pairs = qi_pm.shape[0]
    gw = kv_rep * HEAD_DIM
    groups, start, load = [], 0, 0
    for t in range(nq):
        if load + t + 1 > ATTN_VARIANT_BUDGET and t > start:
            groups.append((start, t - start))
            start, load = t, 0
        load += t + 1
    groups.append((start, nq - start))
    out = None
    for jq0, n_tiles in groups:
        row = lambda b, j, jq0=jq0: b * nq + jq0 + j
        in_specs = [
            pl.BlockSpec((n_pairs, tq, LANES), lambda b, j, g, row=row: (0, row(b, j), 0)),
            pl.BlockSpec((wit.shape[0], tq), lambda b, j, g, row=row: (0, row(b, j))),
            pl.BlockSpec((seq, ki.shape[1]), lambda b, j, g: (b, 0)),
            pl.BlockSpec((tq, gw), lambda b, j, g, row=row: (row(b, j), g)),
            pl.BlockSpec((seq, HEAD_DIM), lambda b, j, g: (b, g)),
            pl.BlockSpec((seq, HEAD_DIM), lambda b, j, g: (b, n_kv + g)),
        ]
        args = [qi_pm, wit, ki, q, kv, kv]
        aliases = {}
        if out is not None:
            in_specs.append(pl.BlockSpec(memory_space=pl.ANY))
            args.append(out)
            aliases = {len(args) - 1: 0}
        out = pl.pallas_call(
            functools.partial(_pattn_kernel, n_sel=n_sel, kv_rep=kv_rep, jq0=jq0,
                              n_tiles=n_tiles),
            grid=(batch, n_tiles, n_kv),
            in_specs=in_specs,
            out_specs=pl.BlockSpec((tq, gw), lambda b, j, g, row=row: (row(b, j), g)),
            out_shape=jax.ShapeDtypeStruct((m, d), _BF16),
            scratch_shapes=[pltpu.VMEM((seq, tq), _F32), pltpu.VMEM((tq, seq), _F32)],
            input_output_aliases=aliases,
            compiler_params=_cparams(("arbitrary", "arbitrary", "arbitrary")),
            name="prompt_attention",
        )(*args)
    return out


def _sidx_kernel(pt_ref, qi_ref, w_ref, kin_ref, *rest, n_sel, t_real, n_chunks):
    pages = rest[:PAGES_PER_STEP]
    bias_ref, sc_ref = rest[PAGES_PER_STEP], rest[PAGES_PER_STEP + 1]
    c = pl.program_id(1)
    rows = SUBLANES
    chunk = PAGES_PER_STEP * PAGE_SIZE
    w = w_ref[0] * ((IDX_HEADS ** -0.5) * (IDX_DIM ** -0.5))
    qi = qi_ref[0]

    def scores(d):
        d = jnp.maximum(d, 0.0) * w
        acc = d[0:rows]
        for h in range(1, IDX_HEADS):
            acc = acc + d[h * rows:(h + 1) * rows]
        return acc

    @pl.when(c < n_chunks)
    def _():
        keys_t = jnp.concatenate([p[0] for p in pages], axis=1).astype(_BF16)
        sc_ref[c] = scores(jnp.dot(qi, keys_t, preferred_element_type=_F32))

    @pl.when(c == n_chunks)
    def _():
        kn = kin_ref[...].astype(_BF16)
        kn = jnp.concatenate([kn, jnp.zeros((PAGE_SIZE - rows, kn.shape[1]), _BF16)], axis=0)
        s_new = scores(lax.dot_general(qi, kn, _NT, preferred_element_type=_F32))
        s_new = jnp.concatenate([s_new, jnp.zeros((rows, chunk - PAGE_SIZE), _F32)], axis=1)
        shape = (n_chunks + 1, rows, chunk)
        ci = lax.broadcasted_iota(jnp.int32, shape, 0)
        qrow = lax.broadcasted_iota(jnp.int32, shape, 1)
        lane = lax.broadcasted_iota(jnp.int32, shape, 2)
        adm = (ci < n_chunks) | ((lane <= qrow) & (lane < t_real))
        sc_ref[n_chunks] = s_new
        keys = _sortable(jnp.where(adm, sc_ref[...], NEG))
        count = lambda x: jnp.sum(jnp.sum(x, axis=0, keepdims=True), axis=2, keepdims=True)
        key_idx = ci * chunk + lane
        n_keys = (n_chunks + 1) * chunk
        sel = _select_topk(keys, key_idx, n_sel, count, (n_keys - 1).bit_length()) & adm
        bias_ref[0] = jnp.where(sel, 0.0, NEG).astype(_F32)


def _page_spec(block, n_chunks, page0, i):
    zeros = (0,) * (len(block) - 1)
    return pl.BlockSpec(
        block,
        lambda b, c, pt: (page0 + pt[b, jnp.minimum(c, n_chunks - 1) * PAGES_PER_STEP + i],)
        + zeros)


def _sample_index(page_table, qi_hm, w_hm, ki_new, pool_ki, page0, row0, *, t_real):
    bd, n_pages = page_table.shape
    n_chunks = n_pages // PAGES_PER_STEP
    chunk = PAGES_PER_STEP * PAGE_SIZE
    past = n_pages * PAGE_SIZE
    n_sel = min(TOPK_MAX, (past + t_real) // 4)
    hr = qi_hm.shape[1]
    grid_spec = pltpu.PrefetchScalarGridSpec(
        num_scalar_prefetch=1,
        grid=(bd, n_chunks + 1),
        in_specs=[pl.BlockSpec((1, hr, IDX_DIM), lambda b, c, pt: (b, 0, 0)),
                  pl.BlockSpec((1, hr, 1), lambda b, c, pt: (b, 0, 0)),
                  pl.BlockSpec((SUBLANES, IDX_DIM), lambda b, c, pt: (row0 + b, 0))]
                 + [_page_spec((1, IDX_DIM, PAGE_SIZE), n_chunks, page0, i)
                    for i in range(PAGES_PER_STEP)],
        out_specs=pl.BlockSpec((1, n_chunks + 1, SUBLANES, chunk), lambda b, c, pt: (b, 0, 0, 0)),
        scratch_shapes=[pltpu.VMEM((n_chunks + 1, SUBLANES, chunk), _F32)],
    )
    return pl.pallas_call(
        functools.partial(_sidx_kernel, n_sel=n_sel, t_real=t_real, n_chunks=n_chunks),
        grid_spec=grid_spec,
        out_shape=jax.ShapeDtypeStruct((bd, n_chunks + 1, SUBLANES, chunk), _F32),
        compiler_params=_cparams(("arbitrary", "arbitrary")),
        name="sample_index",
    )(page_table, qi_hm, w_hm, ki_new, *([pool_ki] * PAGES_PER_STEP))


def _sattn_kernel(pt_ref, q_ref, bias_ref, kn_ref, vn_ref, *rest, n_chunks, n_kv, kv_rep):
    kpages = rest[:PAGES_PER_STEP]
    vpages = rest[PAGES_PER_STEP:2 * PAGES_PER_STEP]
    o_ref, m_ref, l_ref, acc_ref = rest[2 * PAGES_PER_STEP:]
    c = pl.program_id(1)
    rows = SUBLANES

    @pl.when(c == 0)
    def _():
        m_ref[...] = jnp.full_like(m_ref, -jnp.inf)
        l_ref[...] = jnp.zeros_like(l_ref)
        acc_ref[...] = jnp.zeros_like(acc_ref)

    gr = kv_rep * rows

    def update(kgs, vgs, bias):
        parts = []
        for g in range(n_kv):
            qg = jnp.concatenate(
                [q_ref[:, (g * kv_rep + r) * HEAD_DIM:(g * kv_rep + r + 1) * HEAD_DIM]
                 for r in range(kv_rep)], axis=0).astype(_BF16)
            parts.append(lax.dot_general(qg, kgs[g], _NT, preferred_element_type=_F32))
        s = jnp.concatenate(parts, axis=0) + jnp.concatenate([bias] * (n_kv * kv_rep), axis=0)
        m_old = m_ref[...]
        m_new = jnp.maximum(m_old, jnp.max(s, axis=1, keepdims=True))
        alpha = jnp.exp(m_old - m_new)
        p = jnp.exp(s - m_new)
        l_ref[...] = alpha * l_ref[...] + jnp.sum(p, axis=1, keepdims=True)
        pb = p.astype(_BF16)
        pv = jnp.concatenate(
            [jnp.dot(pb[g * gr:(g + 1) * gr], vgs[g], preferred_element_type=_F32)
             for g in range(n_kv)], axis=0)
        acc_ref[...] = alpha * acc_ref[...] + pv
        m_ref[...] = m_new

    @pl.when(c < n_chunks)
    def _():
        def head_rows(pages, g):
            rows_g = pl.ds(g, PAGE_SIZE, stride=n_kv)
            return jnp.concatenate([p[rows_g, :] for p in pages], axis=0).astype(_BF16)

        update([head_rows(kpages, g) for g in range(n_kv)],
               [head_rows(vpages, g) for g in range(n_kv)], bias_ref[0, 0])

    @pl.when(c == n_chunks)
    def _():
        pad = jnp.zeros((PAGE_SIZE - rows, HEAD_DIM), _BF16)

        def new_rows(ref, g):
            return jnp.concatenate(
                [ref[:, g * HEAD_DIM:(g + 1) * HEAD_DIM].astype(_BF16), pad], axis=0)

        update([new_rows(kn_ref, g) for g in range(n_kv)],
               [new_rows(vn_ref, g) for g in range(n_kv)], bias_ref[0, 0][:, :PAGE_SIZE])
        o = acc_ref[...] / l_ref[...]
        for h in range(n_kv * kv_rep):
            o_ref[:, h * HEAD_DIM:(h + 1) * HEAD_DIM] = (
                o[h * rows:(h + 1) * rows].astype(o_ref.dtype))


def _sample_attention(page_table, q, bias, kv_new, pool_k, pool_v, page0):
    bd, n_pages = page_table.shape
    n_chunks = n_pages // PAGES_PER_STEP
    chunk = PAGES_PER_STEP * PAGE_SIZE
    m, d = q.shape
    kvw = kv_new.shape[1] // 2
    n_kv = kvw // HEAD_DIM
    kv_rep = (d // HEAD_DIM) // n_kv
    grid_spec = pltpu.PrefetchScalarGridSpec(
        num_scalar_prefetch=1,
        grid=(bd, n_chunks + 1),
        in_specs=[pl.BlockSpec((SUBLANES, d), lambda b, c, pt: (b, 0)),
                  pl.BlockSpec((1, 1, SUBLANES, chunk), lambda b, c, pt: (b, c, 0, 0)),
                  pl.BlockSpec((SUBLANES, kvw), lambda b, c, pt: (b, 0)),
                  pl.BlockSpec((SUBLANES, kvw), lambda b, c, pt: (b, 1))]
                 + [_page_spec((PAGE_SIZE * n_kv, HEAD_DIM), n_chunks, page0, i)
                    for i in range(PAGES_PER_STEP)] * 2,
        out_specs=pl.BlockSpec((SUBLANES, d), lambda b, c, pt: (b, 0)),
        scratch_shapes=[pltpu.VMEM((n_kv * kv_rep * SUBLANES, 1), _F32),
                        pltpu.VMEM((n_kv * kv_rep * SUBLANES, 1), _F32),
                        pltpu.VMEM((n_kv * kv_rep * SUBLANES, HEAD_DIM), _F32)],
    )
    return pl.pallas_call(
        functools.partial(_sattn_kernel, n_chunks=n_chunks, n_kv=n_kv, kv_rep=kv_rep),
        grid_spec=grid_spec,
        out_shape=jax.ShapeDtypeStruct((m, d), _F32),
        compiler_params=_cparams(("arbitrary", "arbitrary")),
        name="sample_attention",
    )(page_table, q, bias, kv_new, kv_new,
      *([pool_k] * PAGES_PER_STEP), *([pool_v] * PAGES_PER_STEP))


N_GATE = 5


def _gm_kernel(*refs, starts, ntiles, tms, seq_lens, has_state, cw):
    ng = len(starts)
    it = iter(refs)
    wp_ref, taps_ref = next(it), next(it)
    groups = []
    for gi in range(ng):
        g = dict(x=next(it), attn=next(it))
        if has_state[gi]:
            g["s0"], g["s1"] = next(it), next(it)
        groups.append(g)
    for g in groups:
        g["o"], g["tail"] = next(it), next(it)
    wb_ref = next(it)
    for g in groups:
        g["u"] = next(it)
    jj = pl.program_id(0)
    i = pl.program_id(1)
    nj = pl.num_programs(0) - 1

    @pl.when((jj < nj) & (i < N_GATE))
    def _():
        wb_ref[jj % 2, pl.ds(pl.multiple_of(i * cw, cw), cw), :] = wp_ref[0].astype(_BF16)

    for gi, g in enumerate(groups):
        tm, seq = tms[gi], seq_lens[gi]
        active = (jj > 0) & (i >= starts[gi]) & (i < starts[gi] + ntiles[gi])

        @pl.when(active)
        def _(g=g, gi=gi, tm=tm, seq=seq):
            u_ref = g["u"]
            first = i == starts[gi]

            @pl.when(first)
            def _():
                u_ref[0:SUBLANES, :] = jnp.zeros((SUBLANES, cw), _F32)

            @pl.when(jnp.logical_not(first))
            def _():
                u_ref[0:SUBLANES, :] = u_ref[tm:tm + SUBLANES, :]

            sub = min(tm, SUB_TM)
            for r0 in range(0, tm, sub):
                rs = slice(r0, r0 + sub)
                z = lax.dot_general(g["x"][rs, :], wb_ref[(jj + 1) % 2], _NT,
                                    preferred_element_type=_F32)
                cb, cc, ch, ga, gc = [z[:, s * cw:(s + 1) * cw] for s in range(N_GATE)]
                u = cc * ch
                u_ref[SUBLANES + r0:SUBLANES + r0 + sub, :] = u
                t = ((i - starts[gi]) * tm + r0
                     + lax.broadcasted_iota(jnp.int32, (sub, 1), 0)) % seq
                if "s0" in g:
                    s0, s1 = g["s0"][rs, :], g["s1"][rs, :]
                else:
                    s0 = s1 = jnp.zeros((sub, cw), _F32)
                h1 = SUBLANES - 1 + r0
                u1 = jnp.where(t >= 1, u_ref[h1:h1 + sub, :], s1)
                u2 = jnp.where(t >= 2, u_ref[h1 - 1:h1 - 1 + sub, :], jnp.where(t == 1, s1, s0))
                y = u * taps_ref[2:3, :] + u1 * taps_ref[1:2, :] + u2 * taps_ref[0:1, :]
                merged = (jax.nn.sigmoid(ga) * g["attn"][rs, :].astype(_F32)
                          + jax.nn.sigmoid(gc) * (cb * y))
                g["o"][rs, :] = merged.astype(g["o"].dtype)
            if tm >= seq:
                for s in range(tm // seq):
                    g["tail"][s] = u_ref[(s + 1) * seq:(s + 1) * seq + SUBLANES, :]
            else:
                g["tail"][0] = u_ref[tm:tm + SUBLANES, :]


def _gate_merge(xs, w_t, layer, attns, states, taps, *, col_start, tms, seq_lens, out_dtypes):
    assert CONV_WIDTH == 3
    kdim = w_t.shape[2]
    d = attns[0].shape[1]
    cw = min(d, GATE_CW)
    nj = d // cw
    ng = len(xs)
    ntiles = [x.shape[0] // tm for x, tm in zip(xs, tms)]
    starts = [int(s) for s in np.concatenate([[0], np.cumsum(ntiles)[:-1]])]
    ni = sum(ntiles)
    assert ni >= N_GATE and col_start % SUBLANES == 0 and d % cw == 0

    def tile(gi):
        return lambda i: jnp.clip(i - starts[gi], 0, ntiles[gi] - 1)

    col = lambda jj: jnp.maximum(jj - 1, 0)
    rowsel = lambda jj, i, t: jnp.where(jj > 0, t(i), 0)
    nxt = lambda jj: jnp.minimum(jj, nj - 1)
    seg = lambda i: jnp.minimum(i, N_GATE - 1)
    in_specs = [
        pl.BlockSpec((pl.Element(1), pl.Element(cw), pl.Element(kdim)),
                     lambda jj, i: (layer, pl.multiple_of(col_start + seg(i) * d + nxt(jj) * cw,
                                                          SUBLANES), 0)),
        pl.BlockSpec((CONV_WIDTH, cw), lambda jj, i: (0, col(jj)))]
    args = [w_t, taps]
    for gi in range(ng):
        tm, t = tms[gi], tile(gi)
        in_specs.append(pl.BlockSpec((tm, kdim), lambda jj, i, t=t: (rowsel(jj, i, t), 0)))
        act = pl.BlockSpec((tm, cw), lambda jj, i, t=t: (rowsel(jj, i, t), col(jj)))
        in_specs.append(act)
        args += [xs[gi], attns[gi]]
        if states[gi] is not None:
            in_specs += [act, act]
            args += list(states[gi])
    out_shapes, out_specs = [], []
    for gi in range(ng):
        tm, t, m, seq = tms[gi], tile(gi), xs[gi].shape[0], seq_lens[gi]
        spt, tps = max(tm // seq, 1), max(seq // tm, 1)
        out_shapes += [jax.ShapeDtypeStruct((m, d), out_dtypes[gi]),
                       jax.ShapeDtypeStruct((m // seq, SUBLANES, d), _F32)]
        out_specs += [
            pl.BlockSpec((tm, cw), lambda jj, i, t=t: (rowsel(jj, i, t), col(jj))),
            pl.BlockSpec((spt, SUBLANES, cw),
                         lambda jj, i, t=t, tps=tps: (rowsel(jj, i, t) // tps, 0, col(jj)))]
    scratch = [pltpu.VMEM((2, N_GATE * cw, kdim), _BF16)]
    scratch += [pltpu.VMEM((tm + SUBLANES, cw), _F32) for tm in tms]
    res = pl.pallas_call(
        functools.partial(_gm_kernel, starts=starts, ntiles=ntiles, tms=tuple(tms),
                          seq_lens=tuple(seq_lens),
                          has_state=tuple(s is not None for s in states), cw=cw),
        grid=(nj + 1, ni),
        in_specs=in_specs,
        out_specs=out_specs,
        out_shape=out_shapes,
        scratch_shapes=scratch,
        compiler_params=_cparams(("arbitrary", "arbitrary")),
        name="gate_merge",
    )(*args)
    return [(res[2 * gi], res[2 * gi + 1]) for gi in range(ng)]


def _rope_tables(positions, half, n_rot_lanes=LANES):
    pos = np.asarray(positions, np.float64)[:, None]
    lane = np.arange(LANES)
    inv = ROPE_THETA ** (-(lane % half).astype(np.float64) / half)
    ang = pos * inv[None, :]
    sign = np.where((lane % (2 * half)) < half, -1.0, 1.0)
    rot = (lane < n_rot_lanes)[None, :]
    cos = np.where(rot, np.cos(ang), 1.0)
    sin = np.where(rot, np.sin(ang) * sign[None, :], 0.0)
    return jnp.asarray(cos, _F32), jnp.asarray(sin, _F32)


def kernel(x_prompt, x_sample, cache_k, cache_v, cache_kidx, state_conv, page_table,
           w_in, conv_w, w_o, g_mix, g_mlp, w_up, w_down, g_final):
    B, S, D = x_prompt.shape
    Bd, T, _ = x_sample.shape
    depth = w_in.shape[0]
    n_pool = cache_k.shape[1]
    past = page_table.shape[1] * PAGE_SIZE
    R = SUBLANES
    kvw = N_KV_HEADS * HEAD_DIM
    half, ihalf = HEAD_DIM // 2, IDX_DIM // 2
    Mp, Ms = B * S, Bd * R
    tms = [min(Mp, 1024, S // 2), Ms]
    tms_down = [min(Mp, 2048), Ms]

    c_q, c_k = 0, N_HEADS * HEAD_DIM
    c_v = c_k + kvw
    c_qi = c_v + kvw
    c_kiwi = c_qi + IDX_HEADS * IDX_DIM
    c_gate = c_kiwi + IDX_DIM + IDX_HEADS

    pos_p = np.arange(max(S, tms[0])) % S
    pos_s = np.tile(past + np.arange(R), Bd)
    r128 = [_rope_tables(pos_p, half), _rope_tables(pos_s, half)]
    r64 = [_rope_tables(pos_p, ihalf), _rope_tables(pos_s, ihalf)]
    rkiwi = [_rope_tables(pos_p, ihalf, IDX_DIM), _rope_tables(pos_s, ihalf, IDX_DIM)]

    xp = x_prompt.reshape(Mp, D)
    xs = jnp.pad(x_sample, ((0, 0), (0, R - T), (0, 0))).reshape(Ms, D)
    w_in_t = jnp.swapaxes(w_in, 1, 2)
    pool_k = cache_k.reshape(depth * n_pool * PAGE_SIZE * N_KV_HEADS, HEAD_DIM)
    pool_v = cache_v.reshape(depth * n_pool * PAGE_SIZE * N_KV_HEADS, HEAD_DIM)
    pool_ki_t = jnp.swapaxes(cache_kidx, 2, 3).reshape(depth * n_pool, IDX_DIM, PAGE_SIZE)

    outs = {n: [] for n in ("kp", "vp", "kip", "cp", "ks", "vs", "kis", "cs")}
    for l in range(depth):
        hs = [_rmsnorm(xp, g_mix[l], _BF16), _rmsnorm(xs, g_mix[l], _BF16)]
        mm = functools.partial(_matmul, hs, w_in_t, l, tms=tms, w_t=True)
        qp, qs = mm(col_start=c_q, n_cols=N_HEADS * HEAD_DIM, out_dtype=_BF16,
                    rope=(half, r128), scale=HEAD_DIM ** -0.5)
        kvp, kvs = mm(col_start=c_k, n_cols=2 * kvw, out_dtype=_F32, rope=(half, r128),
                      tn=min(TN, kvw), rope_blocks=kvw // min(TN, kvw))
        qip, qis = mm(col_start=c_qi, n_cols=IDX_HEADS * IDX_DIM, out_dtype=_BF16,
                      rope=(ihalf, r64), pair_major=[True, False])
        kwp, kws = mm(col_start=c_kiwi, n_cols=LANES, out_dtype=_F32, rope=(ihalf, rkiwi))
        page0 = l * n_pool

        kip = kwp[:, :IDX_DIM]
        wit = kwp[:, IDX_DIM:IDX_DIM + IDX_HEADS].T
        attn_p = _prompt_attention(qip, wit, kip, qp, kvp, batch=B, seq=S)

        qi_hm = qis.reshape(Bd, R, IDX_HEADS, IDX_DIM).transpose(0, 2, 1, 3).reshape(
            Bd, IDX_HEADS * R, IDX_DIM)
        kis = kws[:, :IDX_DIM]
        w_hm = kws[:, IDX_DIM:IDX_DIM + IDX_HEADS].reshape(Bd, R, IDX_HEADS).transpose(
            0, 2, 1).reshape(Bd, IDX_HEADS * R, 1)
        bias = _sample_index(page_table, qi_hm, w_hm, kis, pool_ki_t, page0, 0, t_real=T)
        attn_s = _sample_attention(page_table, qs.astype(_F32), bias, kvs, pool_k, pool_v, page0)

        s0, s1 = [jnp.repeat(state_conv[l][:, r:r + 1, :], R, axis=1).reshape(Ms, D)
                  for r in range(CONV_WIDTH - 1)]
        (mg_p, tail_p), (mg_s, tail_s) = _gate_merge(
            hs, w_in_t, l, [attn_p, attn_s], [None, (s0, s1)], conv_w[l], col_start=c_gate,
            tms=tms, seq_lens=[S, R], out_dtypes=[_BF16, _F32])
        xp, xs = _matmul([mg_p, mg_s.astype(_BF16)], w_o, l, col_start=0, n_cols=D,
                         out_dtype=_F32, tms=tms, epi="residual", residuals=[xp, xs])
        h2 = [_rmsnorm(xp, g_mlp[l], _BF16), _rmsnorm(xs, g_mlp[l], _BF16)]
        hid_p, hid_s, w_down_bf16 = _matmul(h2, w_up, l, col_start=0, n_cols=w_up.shape[2],
                                            out_dtype=_BF16, tms=tms, epi="relu2",
                                            side_cast=(w_down, l))
        xp, xs = _matmul([hid_p, hid_s], w_down_bf16, 0, col_start=0, n_cols=D, out_dtype=_F32,
                         tms=tms_down, epi="residual", residuals=[xp, xs], tn=min(D, 1024),
                         tk=min(w_down.shape[1], 2048))

        outs["kp"].append(kvp[:, :kvw].reshape(B, S, N_KV_HEADS, HEAD_DIM))
        outs["vp"].append(kvp[:, kvw:].reshape(B, S, N_KV_HEADS, HEAD_DIM))
        outs["kip"].append(kip.reshape(B, S, IDX_DIM))
        outs["cp"].append(tail_p[:, R - (CONV_WIDTH - 1):, :])
        outs["ks"].append(kvs[:, :kvw].reshape(Bd, R, N_KV_HEADS, HEAD_DIM)[:, :T])
        outs["vs"].append(kvs[:, kvw:].reshape(Bd, R, N_KV_HEADS, HEAD_DIM)[:, :T])
        outs["kis"].append(kis.reshape(Bd, R, IDX_DIM)[:, :T])
        outs["cs"].append(tail_s[:, T - (CONV_WIDTH - 1):T, :])

    y_prompt = _rmsnorm(xp, g_final, _F32).reshape(B, S, D)
    y_sample = _rmsnorm(xs, g_final, _F32).reshape(Bd, R, D)[:, :T]
    st = lambda n: jnp.stack(outs[n])
    return (y_prompt, y_sample, st("kp"), st("vp"), st("kip"), st("cp"),
            st("ks"), st("vs"), st("kis"), st("cs"))
```

```python
import functools

import numpy as np
import jax
import jax.numpy as jnp
from jax import lax
from jax.experimental import pallas as pl
from jax.experimental.pallas import tpu as pltpu

HEAD_DIM = 128
N_HEADS = 32
N_KV_HEADS = 8
IDX_HEADS = 32
IDX_DIM = 64
TOPK_MAX = 256
CONV_WIDTH = 3
PAGE_SIZE = 128
ROPE_THETA = 10000.0
NORM_EPS = 1e-6
NEG = -1e30

LANES = 128
SUBLANES = 8
VMEM_LIMIT = 60 * 1024 * 1024
PAGES_PER_STEP = 16
INT_MIN = -2 ** 31
TN = 1024
SUB_TN = 512
SUB_TM = 512
GATE_CW = 256

_F32 = jnp.float32
_BF16 = jnp.bfloat16
_NT = (((1,), (1,)), ((), ()))


def _cparams(sem):
    return pltpu.CompilerParams(dimension_semantics=sem, vmem_limit_bytes=VMEM_LIMIT)


def _rmsnorm_kernel(x_ref, g_ref, o_ref):
    x = x_ref[...]
    y = x * lax.rsqrt(jnp.mean(x * x, axis=-1, keepdims=True) + NORM_EPS)
    o_ref[...] = (y * g_ref[...]).astype(o_ref.dtype)


def _rmsnorm(x, g, out_dtype):
    m, d = x.shape
    tm = min(m, 256)
    return pl.pallas_call(
        _rmsnorm_kernel,
        grid=(m // tm,),
        in_specs=[pl.BlockSpec((tm, d), lambda i: (i, 0)),
                  pl.BlockSpec((1, d), lambda i: (0, 0))],
        out_specs=pl.BlockSpec((tm, d), lambda i: (i, 0)),
        out_shape=jax.ShapeDtypeStruct((m, d), out_dtype),
        compiler_params=_cparams(("arbitrary",)),
        name="rmsnorm",
    )(x, g.reshape(1, d))


def _rope_slab(z, cos, sin, half):
    if 2 * half == LANES:
        partner = pltpu.roll(z, half, 1)
    else:
        lane = lax.broadcasted_iota(jnp.int32, z.shape, 1)
        first = (lane % (2 * half)) < half
        partner = jnp.where(first, pltpu.roll(z, LANES - half, 1), pltpu.roll(z, half, 1))
    return z * cos + partner * sin


def _epilogue(g, z, c0, *, epi, rope_half, scale):
    o_ref = g["o"]
    width = z.shape[1]
    if rope_half:
        cos, sin = g["cos"][...], g["sin"][...]
        for s in range(width // LANES):
            slab = _rope_slab(z[:, s * LANES:(s + 1) * LANES], cos, sin, rope_half)
            if scale != 1.0:
                slab = slab * scale
            if g["pm"]:
                o_ref[c0 // LANES + s] = slab.astype(o_ref.dtype)
            else:
                o_ref[:, c0 + s * LANES:c0 + (s + 1) * LANES] = slab.astype(o_ref.dtype)
        return
    if epi == "relu2":
        r = jnp.maximum(z, 0.0)
        z = r * r
    elif epi == "residual":
        z = g["res"][:, c0:c0 + width] + z
    o_ref[:, c0:c0 + width] = z.astype(o_ref.dtype)


def _unpack_groups(it, ng, rope_half, epi, pair_major):
    groups = []
    for _ in range(ng):
        g = dict(x=next(it))
        if rope_half:
            g["cos"], g["sin"] = next(it), next(it)
        if epi == "residual":
            g["res"] = next(it)
        groups.append(g)
    for gi, g in enumerate(groups):
        g["o"] = next(it)
        g["pm"] = pair_major[gi]
    return groups


def _mm_ws_kernel(*refs, starts, ntiles, npieces, epi, rope_half, rope_blocks, scale, pair_major,
                  w_t, side_cast):
    it = iter(refs)
    wp_ref = next(it)
    side_in = next(it) if side_cast else None
    groups = _unpack_groups(it, len(starts), rope_half, epi, pair_major)
    side_out = next(it) if side_cast else None
    wb_ref = next(it)
    jj = pl.program_id(0)
    i = pl.program_id(1)
    nj = pl.num_programs(0) - 1

    if side_cast:
        @pl.when((jj > 0) & (i < ntiles[0]))
        def _():
            side_out[...] = side_in[...].astype(side_out.dtype)

    @pl.when((jj < nj) & (i < npieces))
    def _():
        piece = wp_ref[0].astype(_BF16)
        rows = piece.shape[0]
        wb_ref[jj % 2, pl.ds(pl.multiple_of(i * rows, rows), rows), :] = piece

    for gi, g in enumerate(groups):
        active = (jj > 0) & (i >= starts[gi]) & (i < starts[gi] + ntiles[gi])

        @pl.when(active)
        def _(g=g):
            slot = (jj + 1) % 2
            tn = wb_ref.shape[1] if w_t else wb_ref.shape[2]
            sub = min(tn, SUB_TN)
            for c0 in range(0, tn, sub):
                if w_t:
                    z = lax.dot_general(g["x"][...], wb_ref[slot, c0:c0 + sub, :], _NT,
                                        preferred_element_type=_F32)
                else:
                    z = jnp.dot(g["x"][...], wb_ref[slot, :, c0:c0 + sub],
                                preferred_element_type=_F32)
                if rope_blocks is None:
                    _epilogue(g, z, c0, epi=epi, rope_half=rope_half, scale=scale)
                else:
                    @pl.when(jj <= rope_blocks)
                    def _():
                        _epilogue(g, z, c0, epi=epi, rope_half=rope_half, scale=scale)

                    @pl.when(jj > rope_blocks)
                    def _():
                        _epilogue(g, z, c0, epi=epi, rope_half=0, scale=1.0)


def _mm_kt_kernel(*refs, starts, ntiles):
    it = iter(refs)
    w_ref = next(it)
    groups = _unpack_groups(it, len(starts), 0, "residual", [False] * len(starts))
    i = pl.program_id(1)
    k = pl.program_id(2)
    for gi, g in enumerate(groups):
        active = (i >= starts[gi]) & (i < starts[gi] + ntiles[gi])

        @pl.when(active)
        def _(g=g):
            @pl.when(k == 0)
            def _():
                g["o"][...] = g["res"][...]

            tm = g["x"].shape[0]
            sub = min(tm, SUB_TM)
            for r0 in range(0, tm, sub):
                g["o"][r0:r0 + sub, :] += jnp.dot(g["x"][r0:r0 + sub, :], w_ref[0].astype(_BF16),
                                                   preferred_element_type=_F32)


def _matmul(xs, w, layer, *, col_start, n_cols, out_dtype, tms, epi="none", rope=None,
            scale=1.0, residuals=None, pair_major=None, tn=TN, tk=None, w_t=False,
            rope_blocks=None, side_cast=None):
    kdim = w.shape[2] if w_t else w.shape[1]
    tn = min(tn, n_cols)
    nj = n_cols // tn
    cb0 = col_start // tn
    assert n_cols % tn == 0
    assert col_start % (SUBLANES if w_t else tn) == 0
    ng = len(xs)
    pair_major = [False] * ng if pair_major is None else pair_major
    ntiles = [x.shape[0] // tm for x, tm in zip(xs, tms)]
    starts = [int(s) for s in np.concatenate([[0], np.cumsum(ntiles)[:-1]])]
    ni = sum(ntiles)
    ws = tk is None
    if ws:
        tk = kdim
        npieces = max(p for p in (8, 4, 2, 1) if p <= ni)
    else:
        assert epi == "residual" and out_dtype == _F32 and not w_t and kdim % tk == 0
        assert rope is None and not any(pair_major)

    def tile(gi):
        return lambda i: jnp.clip(i - starts[gi], 0, ntiles[gi] - 1)

    if ws:
        col = lambda a: jnp.maximum(a[0] - 1, 0)
        krow = lambda a: 0
        rowsel = lambda a, t: jnp.where(a[0] > 0, t(a[1]), 0)
        nxt = lambda a: jnp.minimum(a[0], nj - 1)
        pc = lambda a: jnp.minimum(a[1], npieces - 1)
        if w_t:
            pr = tn // npieces
            w_spec = pl.BlockSpec(
                (pl.Element(1), pl.Element(pr), pl.Element(tk)),
                lambda *a: (layer, pl.multiple_of(col_start + nxt(a) * tn + pc(a) * pr, SUBLANES),
                            0))
        else:
            w_spec = pl.BlockSpec((1, tk // npieces, tn), lambda *a: (layer, pc(a), cb0 + nxt(a)))
    else:
        col = lambda a: a[0]
        krow = lambda a: a[2]
        rowsel = lambda a, t: t(a[1])
        w_spec = pl.BlockSpec((1, tk, tn), lambda *a: (layer, a[2], cb0 + a[0]))

    in_specs, args = [w_spec], [w]
    if side_cast is not None:
        side, side_layer = side_cast
        n_side = nj * ntiles[0]
        sr = side.shape[1] // n_side
        assert ws and starts[0] == 0 and side.shape[1] % n_side == 0 and sr % SUBLANES == 0
        side_blk = lambda a: jnp.clip((a[0] - 1) * ntiles[0] + jnp.minimum(a[1], ntiles[0] - 1),
                                      0, n_side - 1)
        in_specs.append(pl.BlockSpec((1, sr, side.shape[2]),
                                     lambda *a: (side_layer, side_blk(a), 0)))
        args.append(side)
    rope_half = 0
    for gi in range(ng):
        tm, t = tms[gi], tile(gi)
        in_specs.append(pl.BlockSpec((tm, tk), lambda *a, t=t: (rowsel(a, t), krow(a))))
        args.append(xs[gi])
        if rope is not None:
            rope_half = rope[0]
            cos, sin = rope[1][gi]
            nper = cos.shape[0] // tm
            tspec = pl.BlockSpec((tm, LANES),
                                 lambda *a, t=t, nper=nper: (rowsel(a, t) % nper, 0))
            in_specs += [tspec, tspec]
            args += [cos, sin]
        if epi == "residual":
            in_specs.append(pl.BlockSpec((tm, tn), lambda *a, t=t: (rowsel(a, t), col(a))))
            args.append(residuals[gi])
    out_shapes, out_specs = [], []
    for gi in range(ng):
        tm, t, m = tms[gi], tile(gi), xs[gi].shape[0]
        if pair_major[gi]:
            out_shapes.append(jax.ShapeDtypeStruct((n_cols // LANES, m, LANES), out_dtype))
            out_specs.append(pl.BlockSpec((tn // LANES, tm, LANES),
                                          lambda *a, t=t: (col(a), rowsel(a, t), 0)))
        else:
            out_shapes.append(jax.ShapeDtypeStruct((m, n_cols), out_dtype))
            out_specs.append(pl.BlockSpec((tm, tn), lambda *a, t=t: (rowsel(a, t), col(a))))
    if side_cast is not None:
        out_shapes.append(jax.ShapeDtypeStruct((1,) + side.shape[1:], _BF16))
        out_specs.append(pl.BlockSpec((1, sr, side.shape[2]), lambda *a: (0, side_blk(a), 0)))
    if ws:
        body = functools.partial(_mm_ws_kernel, starts=starts, ntiles=ntiles, npieces=npieces,
                                 epi=epi, rope_half=rope_half, rope_blocks=rope_blocks,
                                 scale=scale, pair_major=tuple(pair_major), w_t=w_t,
                                 side_cast=side_cast is not None)
        grid = (nj + 1, ni)
        scratch = [pltpu.VMEM((2, tn, tk) if w_t else (2, tk, tn), _BF16)]
    else:
        body = functools.partial(_mm_kt_kernel, starts=starts, ntiles=ntiles)
        grid = (nj, ni, kdim // tk)
        scratch = []
    return pl.pallas_call(
        body,
        grid=grid,
        in_specs=in_specs,
        out_specs=out_specs,
        out_shape=out_shapes,
        scratch_shapes=scratch,
        compiler_params=_cparams(("arbitrary",) * len(grid)),
        name="matmul_" + epi + ("_rope%d" % rope_half if rope_half else ""),
    )(*args)


def _sortable(x):
    bits = pltpu.bitcast(x, jnp.int32)
    return jnp.where(bits < 0, bits ^ jnp.int32(0x7FFFFFFF), bits)


def _kth_largest(keys, n_sel, count_fn):
    cnt = count_fn((keys >= 0).astype(jnp.int32))
    thr0 = jnp.where(cnt >= n_sel, jnp.int32(0), jnp.int32(INT_MIN))

    def body(i, thr):
        cand = thr | jnp.left_shift(jnp.int32(1), 30 - i)
        cnt = count_fn((keys >= cand).astype(jnp.int32))
        return jnp.where(cnt >= n_sel, cand, thr)

    return lax.fori_loop(0, 31, body, thr0)


def _select_topk(keys, idx, n_sel, count_fn, idx_bits):
    thr = _kth_largest(keys, n_sel, count_fn)
    gt = keys > thr
    eq = keys == thr
    need = n_sel - count_fn(gt.astype(jnp.int32))
    surplus = count_fn(eq.astype(jnp.int32)) - need

    def tie_cut():
        def body(b, lo):
            cand = lo + jnp.left_shift(jnp.int32(1), idx_bits - 1 - b)
            below = count_fn((eq & (idx < cand)).astype(jnp.int32))
            return jnp.where(below < need, cand, lo)

        return lax.fori_loop(0, idx_bits, body, jnp.zeros_like(thr))

    cut = lax.cond(jnp.max(surplus) > 0, tie_cut,
                   lambda: jnp.full_like(thr, jnp.iinfo(jnp.int32).max))
    return gt | (eq & (idx <= cut))


def _pattn_kernel(qi_ref, wit_ref, ki_ref, q_ref, k_ref, v_ref, o_ref, sc_ref, bias_ref, *,
                  n_sel, kv_rep):
    jq = pl.program_id(1)
    s_len, tq = sc_ref.shape
    n_pairs = qi_ref.shape[0]

    def select(klen):
        key_pos = lax.broadcasted_iota(jnp.int32, (klen, tq), 0)
        q_pos = jq * tq + lax.broadcasted_iota(jnp.int32, (klen, tq), 1)
        adm = key_pos <= q_pos
        if klen <= n_sel:
            bias_ref[:, 0:klen] = jnp.where(adm, 0.0, NEG).astype(_F32).T
            return
        kb = ki_ref[0:klen, :].astype(_BF16)
        zeros = jnp.zeros_like(kb)
        k_even = jnp.concatenate([kb, zeros], axis=1)
        k_odd = jnp.concatenate([zeros, kb], axis=1)
        w_scale = (IDX_HEADS ** -0.5) * (IDX_DIM ** -0.5)
        sc_ref[0:klen, :] = jnp.zeros((klen, tq), _F32)

        def body(p, carry):
            x = qi_ref[p]
            d0 = lax.dot_general(k_even, x, _NT, preferred_element_type=_F32)
            d1 = lax.dot_general(k_odd, x, _NT, preferred_element_type=_F32)
            w0 = wit_ref[pl.ds(2 * p, 1), :] * w_scale
            w1 = wit_ref[pl.ds(2 * p + 1, 1), :] * w_scale
            sc_ref[0:klen, :] += jnp.maximum(d0, 0.0) * w0 + jnp.maximum(d1, 0.0) * w1
            return carry

        lax.fori_loop(0, n_pairs, body, 0)

        keys = _sortable(jnp.where(adm, sc_ref[0:klen, :], NEG))
        sel = _select_topk(keys, key_pos, n_sel, lambda c: jnp.sum(c, axis=0, keepdims=True),
                           (s_len - 1).bit_length()) & adm
        bias_ref[:, 0:klen] = jnp.where(sel, 0.0, NEG).astype(_F32).T

    def attend(klen):
        bias = bias_ref[:, 0:klen]

        def group(gq, carry):
            head = lambda h: pl.ds(pl.multiple_of(h * HEAD_DIM, HEAD_DIM), HEAD_DIM)
            kg = k_ref[0:klen, head(gq)].astype(_BF16)
            vg = v_ref[0:klen, head(gq)].astype(_BF16)
            for r in range(kv_rep):
                cols = head(gq * kv_rep + r)
                qh = q_ref[:, cols]
                s = lax.dot_general(qh, kg, _NT, preferred_element_type=_F32) + bias
                m = jnp.max(s, axis=1, keepdims=True)
                p = jnp.exp(s - m)
                l = jnp.sum(p, axis=1, keepdims=True)
                o = jnp.dot(p.astype(_BF16), vg, preferred_element_type=_F32) / l
                o_ref[:, cols] = o.astype(o_ref.dtype)
            return carry

        lax.fori_loop(0, k_ref.shape[1] // HEAD_DIM, group, 0)

    for n in range(1, s_len // tq + 1):
        @pl.when(jq == n - 1)
        def _(klen=n * tq):
            select(klen)
            attend(klen)


def _prompt_attention(qi_pm, wit, ki, q, kv, *, batch, seq):
    m, d = q.shape
    n_kv = kv.shape[1] // (2 * HEAD_DIM)
    kv_rep = (d // HEAD_DIM) // n_kv
    tq = min(seq, 256)
    nq = seq // tq
    n_sel = min(TOPK_MAX, seq // 4)
    n_pairs = qi_pm.shape[0]
    kvw = n_kv * HEAD_DIM
    return pl.pallas_call(
        functools.partial(_pattn_kernel, n_sel=n_sel, kv_rep=kv_rep),
        grid=(batch, nq),
        in_specs=[
            pl.BlockSpec((n_pairs, tq, LANES), lambda b, j: (0, b * nq + j, 0)),
            pl.BlockSpec((wit.shape[0], tq), lambda b, j: (0, b * nq + j)),
            pl.BlockSpec((seq, ki.shape[1]), lambda b, j: (b, 0)),
            pl.BlockSpec((tq, d), lambda b, j: (b * nq + j, 0)),
            pl.BlockSpec((seq, kvw), lambda b, j: (b, 0)),
            pl.BlockSpec((seq, kvw), lambda b, j: (b, 1)),
        ],
        out_specs=pl.BlockSpec((tq, d), lambda b, j: (b * nq + j, 0)),
        out_shape=jax.ShapeDtypeStruct((m, d), _BF16),
        scratch_shapes=[pltpu.VMEM((seq, tq), _F32), pltpu.VMEM((tq, seq), _F32)],
        compiler_params=_cparams(("arbitrary", "arbitrary")),
        name="prompt_attention",
    )(qi_pm, wit, ki, q, kv, kv)


def _sidx_kernel(pt_ref, qi_ref, w_ref, kin_ref, *rest, n_sel, t_real, n_chunks):
    pages = rest[:PAGES_PER_STEP]
    bias_ref, sc_ref = rest[PAGES_PER_STEP], rest[PAGES_PER_STEP + 1]
    c = pl.program_id(1)
    rows = SUBLANES
    chunk = PAGES_PER_STEP * PAGE_SIZE
    w = w_ref[0] * ((IDX_HEADS ** -0.5) * (IDX_DIM ** -0.5))
    qi = qi_ref[0]

    def scores(d):
        d = jnp.maximum(d, 0.0) * w
        acc = d[0:rows]
        for h in range(1, IDX_HEADS):
            acc = acc + d[h * rows:(h + 1) * rows]
        return acc

    @pl.when(c < n_chunks)
    def _():
        keys_t = jnp.concatenate([p[0] for p in pages], axis=1).astype(_BF16)
        sc_ref[c] = scores(jnp.dot(qi, keys_t, preferred_element_type=_F32))

    @pl.when(c == n_chunks)
    def _():
        kn = kin_ref[...].astype(_BF16)
        kn = jnp.concatenate([kn, jnp.zeros((PAGE_SIZE - rows, kn.shape[1]), _BF16)], axis=0)
        s_new = scores(lax.dot_general(qi, kn, _NT, preferred_element_type=_F32))
        s_new = jnp.concatenate([s_new, jnp.zeros((rows, chunk - PAGE_SIZE), _F32)], axis=1)
        shape = (n_chunks + 1, rows, chunk)
        ci = lax.broadcasted_iota(jnp.int32, shape, 0)
        qrow = lax.broadcasted_iota(jnp.int32, shape, 1)
        lane = lax.broadcasted_iota(jnp.int32, shape, 2)
        adm = (ci < n_chunks) | ((lane <= qrow) & (lane < t_real))
        sc_ref[n_chunks] = s_new
        keys = _sortable(jnp.where(adm, sc_ref[...], NEG))
        count = lambda x: jnp.sum(jnp.sum(x, axis=0, keepdims=True), axis=2, keepdims=True)
        key_idx = ci * chunk + lane
        n_keys = (n_chunks + 1) * chunk
        sel = _select_topk(keys, key_idx, n_sel, count, (n_keys - 1).bit_length()) & adm
        bias_ref[0] = jnp.where(sel, 0.0, NEG).astype(_F32)


def _page_spec(block, n_chunks, page0, i):
    zeros = (0,) * (len(block) - 1)
    return pl.BlockSpec(
        block,
        lambda b, c, pt: (page0 + pt[b, jnp.minimum(c, n_chunks - 1) * PAGES_PER_STEP + i],)
        + zeros)


def _sample_index(page_table, qi_hm, w_hm, ki_new, pool_ki, page0, row0, *, t_real):
    bd, n_pages = page_table.shape
    n_chunks = n_pages // PAGES_PER_STEP
    chunk = PAGES_PER_STEP * PAGE_SIZE
    past = n_pages * PAGE_SIZE
    n_sel = min(TOPK_MAX, (past + t_real) // 4)
    hr = qi_hm.shape[1]
    grid_spec = pltpu.PrefetchScalarGridSpec(
        num_scalar_prefetch=1,
        grid=(bd, n_chunks + 1),
        in_specs=[pl.BlockSpec((1, hr, IDX_DIM), lambda b, c, pt: (b, 0, 0)),
                  pl.BlockSpec((1, hr, 1), lambda b, c, pt: (b, 0, 0)),
                  pl.BlockSpec((SUBLANES, IDX_DIM), lambda b, c, pt: (row0 + b, 0))]
                 + [_page_spec((1, IDX_DIM, PAGE_SIZE), n_chunks, page0, i)
                    for i in range(PAGES_PER_STEP)],
        out_specs=pl.BlockSpec((1, n_chunks + 1, SUBLANES, chunk), lambda b, c, pt: (b, 0, 0, 0)),
        scratch_shapes=[pltpu.VMEM((n_chunks + 1, SUBLANES, chunk), _F32)],
    )
    return pl.pallas_call(
        functools.partial(_sidx_kernel, n_sel=n_sel, t_real=t_real, n_chunks=n_chunks),
        grid_spec=grid_spec,
        out_shape=jax.ShapeDtypeStruct((bd, n_chunks + 1, SUBLANES, chunk), _F32),
        compiler_params=_cparams(("arbitrary", "arbitrary")),
        name="sample_index",
    )(page_table, qi_hm, w_hm, ki_new, *([pool_ki] * PAGES_PER_STEP))


def _sattn_kernel(pt_ref, q_ref, bias_ref, kn_ref, vn_ref, *rest, n_chunks, n_kv, kv_rep):
    kpages = rest[:PAGES_PER_STEP]
    vpages = rest[PAGES_PER_STEP:2 * PAGES_PER_STEP]
    o_ref, m_ref, l_ref, acc_ref = rest[2 * PAGES_PER_STEP:]
    c = pl.program_id(1)
    rows = SUBLANES

    @pl.when(c == 0)
    def _():
        m_ref[...] = jnp.full_like(m_ref, -jnp.inf)
        l_ref[...] = jnp.zeros_like(l_ref)
        acc_ref[...] = jnp.zeros_like(acc_ref)

    gr = kv_rep * rows

    def update(kgs, vgs, bias):
        parts = []
        for g in range(n_kv):
            qg = jnp.concatenate(
                [q_ref[:, (g * kv_rep + r) * HEAD_DIM:(g * kv_rep + r + 1) * HEAD_DIM]
                 for r in range(kv_rep)], axis=0).astype(_BF16)
            parts.append(lax.dot_general(qg, kgs[g], _NT, preferred_element_type=_F32))
        s = jnp.concatenate(parts, axis=0) + jnp.concatenate([bias] * (n_kv * kv_rep), axis=0)
        m_old = m_ref[...]
        m_new = jnp.maximum(m_old, jnp.max(s, axis=1, keepdims=True))
        alpha = jnp.exp(m_old - m_new)
        p = jnp.exp(s - m_new)
        l_ref[...] = alpha * l_ref[...] + jnp.sum(p, axis=1, keepdims=True)
        pb = p.astype(_BF16)
        pv = jnp.concatenate(
            [jnp.dot(pb[g * gr:(g + 1) * gr], vgs[g], preferred_element_type=_F32)
             for g in range(n_kv)], axis=0)
        acc_ref[...] = alpha * acc_ref[...] + pv
        m_ref[...] = m_new

    @pl.when(c < n_chunks)
    def _():
        def head_rows(pages, g):
            rows_g = pl.ds(g, PAGE_SIZE, stride=n_kv)
            return jnp.concatenate([p[rows_g, :] for p in pages], axis=0).astype(_BF16)

        update([head_rows(kpages, g) for g in range(n_kv)],
               [head_rows(vpages, g) for g in range(n_kv)], bias_ref[0, 0])

    @pl.when(c == n_chunks)
    def _():
        pad = jnp.zeros((PAGE_SIZE - rows, HEAD_DIM), _BF16)

        def new_rows(ref, g):
            return jnp.concatenate(
                [ref[:, g * HEAD_DIM:(g + 1) * HEAD_DIM].astype(_BF16), pad], axis=0)

        update([new_rows(kn_ref, g) for g in range(n_kv)],
               [new_rows(vn_ref, g) for g in range(n_kv)], bias_ref[0, 0][:, :PAGE_SIZE])
        o = acc_ref[...] / l_ref[...]
        for h in range(n_kv * kv_rep):
            o_ref[:, h * HEAD_DIM:(h + 1) * HEAD_DIM] = (
                o[h * rows:(h + 1) * rows].astype(o_ref.dtype))


def _sample_attention(page_table, q, bias, kv_new, pool_k, pool_v, page0):
    bd, n_pages = page_table.shape
    n_chunks = n_pages // PAGES_PER_STEP
    chunk = PAGES_PER_STEP * PAGE_SIZE
    m, d = q.shape
    kvw = kv_new.shape[1] // 2
    n_kv = kvw // HEAD_DIM
    kv_rep = (d // HEAD_DIM) // n_kv
    grid_spec = pltpu.PrefetchScalarGridSpec(
        num_scalar_prefetch=1,
        grid=(bd, n_chunks + 1),
        in_specs=[pl.BlockSpec((SUBLANES, d), lambda b, c, pt: (b, 0)),
                  pl.BlockSpec((1, 1, SUBLANES, chunk), lambda b, c, pt: (b, c, 0, 0)),
                  pl.BlockSpec((SUBLANES, kvw), lambda b, c, pt: (b, 0)),
                  pl.BlockSpec((SUBLANES, kvw), lambda b, c, pt: (b, 1))]
                 + [_page_spec((PAGE_SIZE * n_kv, HEAD_DIM), n_chunks, page0, i)
                    for i in range(PAGES_PER_STEP)] * 2,
        out_specs=pl.BlockSpec((SUBLANES, d), lambda b, c, pt: (b, 0)),
        scratch_shapes=[pltpu.VMEM((n_kv * kv_rep * SUBLANES, 1), _F32),
                        pltpu.VMEM((n_kv * kv_rep * SUBLANES, 1), _F32),
                        pltpu.VMEM((n_kv * kv_rep * SUBLANES, HEAD_DIM), _F32)],
    )
    return pl.pallas_call(
        functools.partial(_sattn_kernel, n_chunks=n_chunks, n_kv=n_kv, kv_rep=kv_rep),
        grid_spec=grid_spec,
        out_shape=jax.ShapeDtypeStruct((m, d), _F32),
        compiler_params=_cparams(("arbitrary", "arbitrary")),
        name="sample_attention",
    )(page_table, q, bias, kv_new, kv_new,
      *([pool_k] * PAGES_PER_STEP), *([pool_v] * PAGES_PER_STEP))


N_GATE = 5


def _gm_kernel(*refs, starts, ntiles, tms, seq_lens, has_state, cw):
    ng = len(starts)
    it = iter(refs)
    wp_ref, taps_ref = next(it), next(it)
    groups = []
    for gi in range(ng):
        g = dict(x=next(it), attn=next(it))
        if has_state[gi]:
            g["s0"], g["s1"] = next(it), next(it)
        groups.append(g)
    for g in groups:
        g["o"], g["tail"] = next(it), next(it)
    wb_ref = next(it)
    for g in groups:
        g["u"] = next(it)
    jj = pl.program_id(0)
    i = pl.program_id(1)
    nj = pl.num_programs(0) - 1

    @pl.when((jj < nj) & (i < N_GATE))
    def _():
        wb_ref[jj % 2, pl.ds(pl.multiple_of(i * cw, cw), cw), :] = wp_ref[0].astype(_BF16)

    for gi, g in enumerate(groups):
        tm, seq = tms[gi], seq_lens[gi]
        active = (jj > 0) & (i >= starts[gi]) & (i < starts[gi] + ntiles[gi])

        @pl.when(active)
        def _(g=g, gi=gi, tm=tm, seq=seq):
            u_ref = g["u"]
            first = i == starts[gi]

            @pl.when(first)
            def _():
                u_ref[0:SUBLANES, :] = jnp.zeros((SUBLANES, cw), _F32)

            @pl.when(jnp.logical_not(first))
            def _():
                u_ref[0:SUBLANES, :] = u_ref[tm:tm + SUBLANES, :]

            sub = min(tm, SUB_TM)
            for r0 in range(0, tm, sub):
                rs = slice(r0, r0 + sub)
                z = lax.dot_general(g["x"][rs, :], wb_ref[(jj + 1) % 2], _NT,
                                    preferred_element_type=_F32)
                cb, cc, ch, ga, gc = [z[:, s * cw:(s + 1) * cw] for s in range(N_GATE)]
                u = cc * ch
                u_ref[SUBLANES + r0:SUBLANES + r0 + sub, :] = u
                t = ((i - starts[gi]) * tm + r0
                     + lax.broadcasted_iota(jnp.int32, (sub, 1), 0)) % seq
                if "s0" in g:
                    s0, s1 = g["s0"][rs, :], g["s1"][rs, :]
                else:
                    s0 = s1 = jnp.zeros((sub, cw), _F32)
                h1 = SUBLANES - 1 + r0
                u1 = jnp.where(t >= 1, u_ref[h1:h1 + sub, :], s1)
                u2 = jnp.where(t >= 2, u_ref[h1 - 1:h1 - 1 + sub, :], jnp.where(t == 1, s1, s0))
                y = u * taps_ref[2:3, :] + u1 * taps_ref[1:2, :] + u2 * taps_ref[0:1, :]
                merged = (jax.nn.sigmoid(ga) * g["attn"][rs, :].astype(_F32)
                          + jax.nn.sigmoid(gc) * (cb * y))
                g["o"][rs, :] = merged.astype(g["o"].dtype)
            if tm >= seq:
                for s in range(tm // seq):
                    g["tail"][s] = u_ref[(s + 1) * seq:(s + 1) * seq + SUBLANES, :]
            else:
                g["tail"][0] = u_ref[tm:tm + SUBLANES, :]


def _gate_merge(xs, w_t, layer, attns, states, taps, *, col_start, tms, seq_lens, out_dtypes):
    assert CONV_WIDTH == 3
    kdim = w_t.shape[2]
    d = attns[0].shape[1]
    cw = min(d, GATE_CW)
    nj = d // cw
    ng = len(xs)
    ntiles = [x.shape[0] // tm for x, tm in zip(xs, tms)]
    starts = [int(s) for s in np.concatenate([[0], np.cumsum(ntiles)[:-1]])]
    ni = sum(ntiles)
    assert ni >= N_GATE and col_start % SUBLANES == 0 and d % cw == 0

    def tile(gi):
        return lambda i: jnp.clip(i - starts[gi], 0, ntiles[gi] - 1)

    col = lambda jj: jnp.maximum(jj - 1, 0)
    rowsel = lambda jj, i, t: jnp.where(jj > 0, t(i), 0)
    nxt = lambda jj: jnp.minimum(jj, nj - 1)
    seg = lambda i: jnp.minimum(i, N_GATE - 1)
    in_specs = [
        pl.BlockSpec((pl.Element(1), pl.Element(cw), pl.Element(kdim)),
                     lambda jj, i: (layer, pl.multiple_of(col_start + seg(i) * d + nxt(jj) * cw,
                                                          SUBLANES), 0)),
        pl.BlockSpec((CONV_WIDTH, cw), lambda jj, i: (0, col(jj)))]
    args = [w_t, taps]
    for gi in range(ng):
        tm, t = tms[gi], tile(gi)
        in_specs.append(pl.BlockSpec((tm, kdim), lambda jj, i, t=t: (rowsel(jj, i, t), 0)))
        act = pl.BlockSpec((tm, cw), lambda jj, i, t=t: (rowsel(jj, i, t), col(jj)))
        in_specs.append(act)
        args += [xs[gi], attns[gi]]
        if states[gi] is not None:
            in_specs += [act, act]
            args += list(states[gi])
    out_shapes, out_specs = [], []
    for gi in range(ng):
        tm, t, m, seq = tms[gi], tile(gi), xs[gi].shape[0], seq_lens[gi]
        spt, tps = max(tm // seq, 1), max(seq // tm, 1)
        out_shapes += [jax.ShapeDtypeStruct((m, d), out_dtypes[gi]),
                       jax.ShapeDtypeStruct((m // seq, SUBLANES, d), _F32)]
        out_specs += [
            pl.BlockSpec((tm, cw), lambda jj, i, t=t: (rowsel(jj, i, t), col(jj))),
            pl.BlockSpec((spt, SUBLANES, cw),
                         lambda jj, i, t=t, tps=tps: (rowsel(jj, i, t) // tps, 0, col(jj)))]
    scratch = [pltpu.VMEM((2, N_GATE * cw, kdim), _BF16)]
    scratch += [pltpu.VMEM((tm + SUBLANES, cw), _F32) for tm in tms]
    res = pl.pallas_call(
        functools.partial(_gm_kernel, starts=starts, ntiles=ntiles, tms=tuple(tms),
                          seq_lens=tuple(seq_lens),
                          has_state=tuple(s is not None for s in states), cw=cw),
        grid=(nj + 1, ni),
        in_specs=in_specs,
        out_specs=out_specs,
        out_shape=out_shapes,
        scratch_shapes=scratch,
        compiler_params=_cparams(("arbitrary", "arbitrary")),
        name="gate_merge",
    )(*args)
    return [(res[2 * gi], res[2 * gi + 1]) for gi in range(ng)]


def _rope_tables(positions, half, n_rot_lanes=LANES):
    pos = np.asarray(positions, np.float64)[:, None]
    lane = np.arange(LANES)
    inv = ROPE_THETA ** (-(lane % half).astype(np.float64) / half)
    ang = pos * inv[None, :]
    sign = np.where((lane % (2 * half)) < half, -1.0, 1.0)
    rot = (lane < n_rot_lanes)[None, :]
    cos = np.where(rot, np.cos(ang), 1.0)
    sin = np.where(rot, np.sin(ang) * sign[None, :], 0.0)
    return jnp.asarray(cos, _F32), jnp.asarray(sin, _F32)


def kernel(x_prompt, x_sample, cache_k, cache_v, cache_kidx, state_conv, page_table,
           w_in, conv_w, w_o, g_mix, g_mlp, w_up, w_down, g_final):
    B, S, D = x_prompt.shape
    Bd, T, _ = x_sample.shape
    depth = w_in.shape[0]
    n_pool = cache_k.shape[1]
    past = page_table.shape[1] * PAGE_SIZE
    R = SUBLANES
    kvw = N_KV_HEADS * HEAD_DIM
    half, ihalf = HEAD_DIM // 2, IDX_DIM // 2
    Mp, Ms = B * S, Bd * R
    tms = [min(Mp, 1024, S // 2), Ms]
    tms_down = [min(Mp, 2048), Ms]

    c_q, c_k = 0, N_HEADS * HEAD_DIM
    c_v = c_k + kvw
    c_qi = c_v + kvw
    c_kiwi = c_qi + IDX_HEADS * IDX_DIM
    c_gate = c_kiwi + IDX_DIM + IDX_HEADS

    pos_p = np.arange(max(S, tms[0])) % S
    pos_s = np.tile(past + np.arange(R), Bd)
    r128 = [_rope_tables(pos_p, half), _rope_tables(pos_s, half)]
    r64 = [_rope_tables(pos_p, ihalf), _rope_tables(pos_s, ihalf)]
    rkiwi = [_rope_tables(pos_p, ihalf, IDX_DIM), _rope_tables(pos_s, ihalf, IDX_DIM)]

    xp = x_prompt.reshape(Mp, D)
    xs = jnp.pad(x_sample, ((0, 0), (0, R - T), (0, 0))).reshape(Ms, D)
    w_in_t = jnp.swapaxes(w_in, 1, 2)
    pool_k = cache_k.reshape(depth * n_pool * PAGE_SIZE * N_KV_HEADS, HEAD_DIM)
    pool_v = cache_v.reshape(depth * n_pool * PAGE_SIZE * N_KV_HEADS, HEAD_DIM)
    pool_ki_t = jnp.swapaxes(cache_kidx, 2, 3).reshape(depth * n_pool, IDX_DIM, PAGE_SIZE)

    outs = {n: [] for n in ("kp", "vp", "kip", "cp", "ks", "vs", "kis", "cs")}
    for l in range(depth):
        hs = [_rmsnorm(xp, g_mix[l], _BF16), _rmsnorm(xs, g_mix[l], _BF16)]
        mm = functools.partial(_matmul, hs, w_in_t, l, tms=tms, w_t=True)
        qp, qs = mm(col_start=c_q, n_cols=N_HEADS * HEAD_DIM, out_dtype=_BF16,
                    rope=(half, r128), scale=HEAD_DIM ** -0.5)
        kvp, kvs = mm(col_start=c_k, n_cols=2 * kvw, out_dtype=_F32, rope=(half, r128),
                      tn=min(TN, kvw), rope_blocks=kvw // min(TN, kvw))
        qip, qis = mm(col_start=c_qi, n_cols=IDX_HEADS * IDX_DIM, out_dtype=_BF16,
                      rope=(ihalf, r64), pair_major=[True, False])
        kwp, kws = mm(col_start=c_kiwi, n_cols=LANES, out_dtype=_F32, rope=(ihalf, rkiwi))
        page0 = l * n_pool

        kip = kwp[:, :IDX_DIM]
        wit = kwp[:, IDX_DIM:IDX_DIM + IDX_HEADS].T
        attn_p = _prompt_attention(qip, wit, kip, qp, kvp, batch=B, seq=S)

        qi_hm = qis.reshape(Bd, R, IDX_HEADS, IDX_DIM).transpose(0, 2, 1, 3).reshape(
            Bd, IDX_HEADS * R, IDX_DIM)
        kis = kws[:, :IDX_DIM]
        w_hm = kws[:, IDX_DIM:IDX_DIM + IDX_HEADS].reshape(Bd, R, IDX_HEADS).transpose(
            0, 2, 1).reshape(Bd, IDX_HEADS * R, 1)
        bias = _sample_index(page_table, qi_hm, w_hm, kis, pool_ki_t, page0, 0, t_real=T)
        attn_s = _sample_attention(page_table, qs.astype(_F32), bias, kvs, pool_k, pool_v, page0)

        s0, s1 = [jnp.repeat(state_conv[l][:, r:r + 1, :], R, axis=1).reshape(Ms, D)
                  for r in range(CONV_WIDTH - 1)]
        (mg_p, tail_p), (mg_s, tail_s) = _gate_merge(
            hs, w_in_t, l, [attn_p, attn_s], [None, (s0, s1)], conv_w[l], col_start=c_gate,
            tms=tms, seq_lens=[S, R], out_dtypes=[_BF16, _F32])
        xp, xs = _matmul([mg_p, mg_s.astype(_BF16)], w_o, l, col_start=0, n_cols=D,
                         out_dtype=_F32, tms=tms, epi="residual", residuals=[xp, xs])
        h2 = [_rmsnorm(xp, g_mlp[l], _BF16), _rmsnorm(xs, g_mlp[l], _BF16)]
        hid_p, hid_s, w_down_bf16 = _matmul(h2, w_up, l, col_start=0, n_cols=w_up.shape[2],
                                            out_dtype=_BF16, tms=tms, epi="relu2",
                                            side_cast=(w_down, l))
        xp, xs = _matmul([hid_p, hid_s], w_down_bf16, 0, col_start=0, n_cols=D, out_dtype=_F32,
                         tms=tms_down, epi="residual", residuals=[xp, xs], tn=min(D, 1024),
                         tk=min(w_down.shape[1], 2048))

        outs["kp"].append(kvp[:, :kvw].reshape(B, S, N_KV_HEADS, HEAD_DIM))
        outs["vp"].append(kvp[:, kvw:].reshape(B, S, N_KV_HEADS, HEAD_DIM))
        outs["kip"].append(kip.reshape(B, S, IDX_DIM))
        outs["cp"].append(tail_p[:, R - (CONV_WIDTH - 1):, :])
        outs["ks"].append(kvs[:, :kvw].reshape(Bd, R, N_KV_HEADS, HEAD_DIM)[:, :T])
        outs["vs"].append(kvs[:, kvw:].reshape(Bd, R, N_KV_HEADS, HEAD_DIM)[:, :T])
        outs["kis"].append(kis.reshape(Bd, R, IDX_DIM)[:, :T])
        outs["cs"].append(tail_s[:, T - (CONV_WIDTH - 1):T, :])

    y_prompt = _rmsnorm(xp, g_final, _F32).reshape(B, S, D)
    y_sample = _rmsnorm(xs, g_final, _F32).reshape(Bd, R, D)[:, :T]
    st = lambda n: jnp.stack(outs[n])
    return (y_prompt, y_sample, st("kp"), st("vp"), st("kip"), st("cp"),
            st("ks"), st("vs"), st("kis"), st("cs"))
```
